```python
import math
import jax, jax.numpy as jnp
from jax import lax
import numpy as np

D_MODEL = 2048
BATCH = 2
SEQ = 8192
DEPTH = 2

GDN_HEADS = 8
GDN_DK = 128
GDN_DV = 128
RET_HEADS = 8
RET_DK = 128
RET_DV = 128
CONV_WIDTH = 4
LA_CHUNK = 64
ROPE_BASE = 10000.0
GDN_QK = GDN_HEADS * GDN_DK
GDN_V = GDN_HEADS * GDN_DV
RET_QK = RET_HEADS * RET_DK
RET_V = RET_HEADS * RET_DV
MIX_IN = 2 * GDN_QK + 2 * GDN_V + 2 * GDN_HEADS + 2 * RET_QK + 2 * RET_V
MIX_OUT = GDN_V + RET_V
SG_CHUNK = 128
SG_GROUPS = 8
SG_WIDTH = 2 * D_MODEL
SG_GROUP_DIM = SG_WIDTH // SG_GROUPS
FFN_HIDDEN = 4 * D_MODEL
EPS = 1e-6
N_EVEN = (DEPTH + 1) // 2
N_ODD = DEPTH // 2

kernel_name = "hybrid_gdn_retention_gmlp_block"


def rmsnorm(x, w):
    xf = x.astype(jnp.float32)
    y = xf * lax.rsqrt(jnp.mean(jnp.square(xf), axis=-1, keepdims=True) + EPS)
    return (y * w.astype(jnp.float32)).astype(x.dtype)


def head_rmsnorm(x):
    return x * lax.rsqrt(jnp.mean(jnp.square(x), axis=-1, keepdims=True) + EPS)


def layernorm(x, w, b):
    xf = x.astype(jnp.float32)
    mu = jnp.mean(xf, axis=-1, keepdims=True)
    xc = xf - mu
    var = jnp.mean(jnp.square(xc), axis=-1, keepdims=True)
    y = xc * lax.rsqrt(var + EPS) * w.astype(jnp.float32) + b.astype(jnp.float32)
    return y.astype(x.dtype)


def l2norm(x):
    return x * lax.rsqrt(jnp.sum(jnp.square(x), axis=-1, keepdims=True) + 1e-6)


def causal_conv(x, w):
    k_w = w.shape[-1]
    kern = jnp.transpose(w)[:, None, :].astype(x.dtype)
    return lax.conv_general_dilated(x, kern, window_strides=(1,), padding=[(k_w - 1, 0)],
                                    dimension_numbers=('NWC', 'WIO', 'NWC'),
                                    feature_group_count=x.shape[-1])


def rotary(x, pos):
    half = x.shape[-1] // 2
    inv_freq = 1.0 / (ROPE_BASE ** jnp.linspace(0.0, 1.0, half, dtype=jnp.float32))
    ang = pos[:, None] * inv_freq[None, :]
    cos = jnp.cos(ang)[None, :, None, :]
    sin = jnp.sin(ang)[None, :, None, :]
    x1, x2 = x[..., :half], x[..., half:]
    return jnp.concatenate([x1 * cos - x2 * sin, x2 * cos + x1 * sin], axis=-1)


def to_chunks(t, c):
    b_, l_ = t.shape[:2]
    t = t.reshape(b_, l_ // c, c, *t.shape[2:])
    return jnp.swapaxes(t, 2, 3)


def from_chunks(t):
    t = jnp.swapaxes(t, 2, 3)
    return t.reshape(t.shape[0], -1, *t.shape[3:])


def gated_delta_rule(q, k, v, beta, g):
    c = LA_CHUNK
    dk = q.shape[-1]
    dv = v.shape[-1]
    q, k, v, beta, g = (to_chunks(t, c) for t in (q * dk ** -0.5, k, v, beta, g))
    g = jnp.cumsum(g, axis=-1)
    causal = jnp.tril(jnp.ones((c, c), dtype=bool))
    strict = jnp.tril(jnp.ones((c, c), dtype=bool), k=-1)
    decay = jnp.exp(jnp.where(causal, g[..., :, None] - g[..., None, :], -jnp.inf))
    kb = k * beta[..., None]
    a = jnp.where(strict, jnp.einsum('bnhtk,bnhsk->bnhts', kb, k) * decay, 0.0)
    eye = jnp.eye(c, dtype=q.dtype)
    t_inv = lax.linalg.triangular_solve(a + eye, jnp.broadcast_to(eye, a.shape), left_side=True,
                                        lower=True, unit_diagonal=True)
    u = jnp.einsum('bnhts,bnhsv->bnhtv', t_inv, v * beta[..., None])
    w = jnp.einsum('bnhts,bnhsk->bnhtk', t_inv, kb * jnp.exp(g)[..., None])
    qk = jnp.where(causal, jnp.einsum('bnhtk,bnhsk->bnhts', q, k) * decay, 0.0)
    q_dec = q * jnp.exp(g)[..., None]
    k_tail = k * jnp.exp(g[..., -1:] - g)[..., None]
    chunk_decay = jnp.exp(g[..., -1])

    def step(state, xs):
        q_c, k_c, u_c, w_c, qk_c, d_c = xs
        v_new = u_c - jnp.einsum('bhtk,bhkv->bhtv', w_c, state)
        o = jnp.einsum('bhtk,bhkv->bhtv', q_c, state) + jnp.einsum('bhts,bhsv->bhtv', qk_c, v_new)
        state = state * d_c[..., None, None] + jnp.einsum('bhtk,bhtv->bhkv', k_c, v_new)
        return state, o

    b_, _, h_ = g.shape[:3]
    state0 = jnp.zeros((b_, h_, dk, dv), q.dtype)
    xs = tuple(jnp.moveaxis(t, 1, 0) for t in (q_dec, k_tail, u, w, qk, chunk_decay))
    _, o = lax.scan(step, state0, xs)
    return from_chunks(jnp.moveaxis(o, 0, 1))


def retention(q, k, v, log_gamma):
    c = LA_CHUNK
    dk = q.shape[-1]
    dv = v.shape[-1]
    q, k, v = (to_chunks(t, c) for t in (q, k, v))
    pos = jnp.arange(c, dtype=jnp.float32)
    causal = jnp.tril(jnp.ones((c, c), dtype=bool))
    lg = log_gamma[:, None]
    d_mat = jnp.exp(jnp.where(causal, (pos[:, None] - pos[None, :]) * log_gamma[:, None, None], -jnp.inf))
    inner = jnp.einsum('bnhts,bnhsv->bnhtv', jnp.einsum('bnhtk,bnhsk->bnhts', q, k) * d_mat, v)
    q_dec = q * jnp.exp((pos + 1.0) * lg)[..., None]
    k_dec = k * jnp.exp((c - 1.0 - pos) * lg)[..., None]
    chunk_decay = jnp.exp(c * log_gamma)[:, None, None]

    def step(state, xs):
        q_c, k_c, v_c = xs
        o = jnp.einsum('bhtk,bhkv->bhtv', q_c, state)
        state = state * chunk_decay + jnp.einsum('bhtk,bhtv->bhkv', k_c, v_c)
        return state, o

    state0 = jnp.zeros((q.shape[0], q.shape[2], dk, dv), q.dtype)
    xs = tuple(jnp.moveaxis(t, 1, 0) for t in (q_dec, k_dec, v))
    _, cross = lax.scan(step, state0, xs)
    return from_chunks(inner + jnp.moveaxis(cross, 0, 1))


def linear_attention_mixer(h, w_in, conv_w, a_log, dt_bias, out_norm_w, w_out):
    f32 = jnp.float32
    b_, l_, _ = h.shape
    proj = h @ w_in
    sizes = [GDN_QK, GDN_QK, GDN_V, GDN_V, GDN_HEADS, GDN_HEADS, RET_QK, RET_QK, RET_V, RET_V]
    cuts = [int(s) for s in np.cumsum(sizes)[:-1]]
    gq, gk, gv, gz, gb, ga, rq, rk, rv, rg = jnp.split(proj, cuts, axis=-1)

    qkv = jax.nn.silu(causal_conv(jnp.concatenate([gq, gk, gv], axis=-1), conv_w))
    gq, gk, gv = jnp.split(qkv, [GDN_QK, 2 * GDN_QK], axis=-1)
    q = l2norm(gq.astype(f32).reshape(b_, l_, GDN_HEADS, GDN_DK))
    k = l2norm(gk.astype(f32).reshape(b_, l_, GDN_HEADS, GDN_DK))
    v = gv.astype(f32).reshape(b_, l_, GDN_HEADS, GDN_DV)
    beta = jax.nn.sigmoid(gb.astype(f32))
    g = -jnp.exp(a_log.astype(f32)) * jax.nn.softplus(ga.astype(f32) + dt_bias.astype(f32))
    o_a = gated_delta_rule(q, k, v, beta, g)
    z = gz.astype(f32).reshape(b_, l_, GDN_HEADS, GDN_DV)
    o_a = head_rmsnorm(o_a) * out_norm_w.astype(f32) * jax.nn.silu(z)
    o_a = o_a.reshape(b_, l_, GDN_V)

    pos = jnp.arange(l_, dtype=f32)
    log_gamma = jnp.log1p(-jnp.power(2.0, -5.0 - jnp.arange(RET_HEADS, dtype=f32)))
    q = rotary(rq.astype(f32).reshape(b_, l_, RET_HEADS, RET_DK), pos)
    k = rotary(rk.astype(f32).reshape(b_, l_, RET_HEADS, RET_DK), pos) * RET_DK ** -0.5
    v = rv.astype(f32).reshape(b_, l_, RET_HEADS, RET_DV)
    o_b = head_rmsnorm(retention(q, k, v, log_gamma)).reshape(b_, l_, RET_V)
    o_b = jax.nn.silu(rg.astype(f32)) * o_b

    o = jnp.concatenate([o_a, o_b], axis=-1).astype(h.dtype)
    return o @ w_out


def spatial_gating_mixer(h, w_in, ln_w, ln_b, w_s, b_s, w_out):
    b_, l_, _ = h.shape
    proj = jax.nn.gelu(h @ w_in)
    u, v = jnp.split(proj, 2, axis=-1)
    v = layernorm(v, ln_w, ln_b)
    vc = v.reshape(b_, l_ // SG_CHUNK, SG_CHUNK, SG_GROUPS, SG_GROUP_DIM)
    causal = jnp.tril(jnp.ones((SG_CHUNK, SG_CHUNK), dtype=bool))
    ws = jnp.where(causal, w_s, 0.0)
    s = jnp.einsum('gts,bnsgd->bntgd', ws, vc) + jnp.swapaxes(b_s, 0, 1)[:, :, None]
    s = s.reshape(b_, l_, SG_WIDTH)
    return (u * s) @ w_out


def squared_relu_mlp(h, w_up, w_down):
    return jnp.square(jax.nn.relu(h @ w_up)) @ w_down


def setup_inputs(seed: int = 0) -> dict:
    key = jax.random.key(seed)
    ks = jax.random.split(key, 16)
    f32 = jnp.float32

    def dense(k, shape, fan_in):
        return jax.random.normal(k, shape, f32) * fan_in ** -0.5

    x = jax.random.normal(ks[0], (BATCH, SEQ, D_MODEL), f32)
    norm_w = 1.0 + 0.1 * jax.random.normal(ks[1], (DEPTH, 4, D_MODEL), f32)
    la_w_in = dense(ks[2], (N_EVEN, D_MODEL, MIX_IN), D_MODEL)
    la_conv_w = dense(ks[3], (N_EVEN, 2 * GDN_QK + GDN_V, CONV_WIDTH), CONV_WIDTH)
    la_a_log = jnp.log(jax.random.uniform(ks[4], (N_EVEN, GDN_HEADS), f32, 1.0, 16.0))
    dt = jnp.exp(jax.random.uniform(ks[5], (N_EVEN, GDN_HEADS), f32) * (math.log(0.1) - math.log(0.001))
                 + math.log(0.001))
    la_dt_bias = dt + jnp.log(-jnp.expm1(-dt))
    la_out_norm_w = 1.0 + 0.1 * jax.random.normal(ks[6], (N_EVEN, GDN_DV), f32)
    la_w_out = dense(ks[7], (N_EVEN, MIX_OUT, D_MODEL), MIX_OUT)
    sg_w_in = dense(ks[8], (N_ODD, D_MODEL, 2 * SG_WIDTH), D_MODEL)
    sg_ln_w = 1.0 + 0.1 * jax.random.normal(ks[9], (N_ODD, SG_WIDTH), f32)
    sg_ln_b = 0.02 * jax.random.normal(ks[10], (N_ODD, SG_WIDTH), f32)
    sg_w_s = dense(ks[11], (N_ODD, SG_GROUPS, SG_CHUNK, SG_CHUNK), SG_CHUNK)
    sg_b_s = 1.0 + 0.1 * jax.random.normal(ks[12], (N_ODD, SG_GROUPS, SG_CHUNK), f32)
    sg_w_out = dense(ks[13], (N_ODD, SG_WIDTH, D_MODEL), SG_WIDTH)
    ffn_w_up = dense(ks[14], (DEPTH, D_MODEL, FFN_HIDDEN), D_MODEL)
    ffn_w_down = dense(ks[15], (DEPTH, FFN_HIDDEN, D_MODEL), FFN_HIDDEN)
    return {"x": x, "norm_w": norm_w, "la_w_in": la_w_in, "la_conv_w": la_conv_w,
            "la_a_log": la_a_log, "la_dt_bias": la_dt_bias, "la_out_norm_w": la_out_norm_w,
            "la_w_out": la_w_out, "sg_w_in": sg_w_in, "sg_ln_w": sg_ln_w, "sg_ln_b": sg_ln_b,
            "sg_w_s": sg_w_s, "sg_b_s": sg_b_s, "sg_w_out": sg_w_out,
            "ffn_w_up": ffn_w_up, "ffn_w_down": ffn_w_down}


def reference(x, norm_w, la_w_in, la_conv_w, la_a_log, la_dt_bias, la_out_norm_w, la_w_out,
              sg_w_in, sg_ln_w, sg_ln_b, sg_w_s, sg_b_s, sg_w_out, ffn_w_up, ffn_w_down):
    h = x
    for layer in range(DEPTH):
        i = layer // 2
        y = rmsnorm(h, norm_w[layer, 0])
        if layer % 2 == 0:
            y = linear_attention_mixer(y, la_w_in[i], la_conv_w[i], la_a_log[i], la_dt_bias[i],
                                       la_out_norm_w[i], la_w_out[i])
        else:
            y = spatial_gating_mixer(y, sg_w_in[i], sg_ln_w[i], sg_ln_b[i], sg_w_s[i], sg_b_s[i],
                                     sg_w_out[i])
        h = h + rmsnorm(y, norm_w[layer, 1])
        y = squared_relu_mlp(rmsnorm(h, norm_w[layer, 2]), ffn_w_up[layer], ffn_w_down[layer])
        h = h + rmsnorm(y, norm_w[layer, 3])
    return h
```

```python
import functools
import math

import jax
import jax.numpy as jnp
from jax import lax
from jax.experimental import pallas as pl
from jax.experimental.pallas import tpu as pltpu

F32 = jnp.float32
BF16 = jnp.bfloat16

D_MODEL = 2048
N_HEADS = 8
HEAD_DIM = 128
LA_CHUNK = 64
PAIR = 2 * LA_CHUNK
CONV_WIDTH = 4
ROPE_BASE = 10000.0
SG_CHUNK = 128
SG_GROUPS = 8
SG_WIDTH = 2 * D_MODEL
SG_GROUP_DIM = SG_WIDTH // SG_GROUPS
FFN_HIDDEN = 4 * D_MODEL
EPS = 1e-6
LANES = 128
MIB = 1024 * 1024

NT_DIMS = (((1,), (1,)), ((), ()))


def _params(n_grid_axes, vmem_mib):
    return pltpu.CompilerParams(dimension_semantics=("arbitrary",) * n_grid_axes,
                                vmem_limit_bytes=vmem_mib * MIB)


def _dot(a, b):
    return jnp.dot(a, b, preferred_element_type=F32)


def _silu(x):
    return x * jax.nn.sigmoid(x)


def _softplus(x):
    return jnp.maximum(x, 0.0) + jnp.log1p(jnp.exp(-jnp.abs(x)))


def _gelu_tanh(x):
    c = math.sqrt(2.0 / math.pi)
    return x * (0.5 * (1.0 + jnp.tanh(c * (x + 0.044715 * (x * x * x)))))


def _rmsnorm_rows_to(src_ref, nw_ref, dst_ref, rows):
    nw = nw_ref[...]
    step = 16

    def body(r, carry):
        sl = pl.ds(pl.multiple_of(r * step, step), step)
        x = src_ref[sl, :]
        ms = jnp.mean(x * x, axis=-1, keepdims=True)
        dst_ref[sl, :] = (x * lax.rsqrt(ms + EPS) * nw).astype(dst_ref.dtype)
        return carry

    lax.fori_loop(0, rows // step, body, 0)


def _rms_residual_rows(acc_ref, h_ref, nw_ref, o_ref, rows):
    nw = nw_ref[...]
    step = 16

    def body(r, carry):
        sl = pl.ds(pl.multiple_of(r * step, step), step)
        y = acc_ref[sl, :]
        ms = jnp.mean(y * y, axis=-1, keepdims=True)
        o_ref[sl, :] = h_ref[sl, :] + y * lax.rsqrt(ms + EPS) * nw
        return carry

    lax.fori_loop(0, rows // step, body, 0)


def _inproj_la_kernel(x_ref, nw_ref, w_ref, wg_ref, o_ref, g_ref, xn_ref, *, tm):
    @pl.when(pl.program_id(1) == 0)
    def _():
        _rmsnorm_rows_to(x_ref, nw_ref, xn_ref, tm)
        g_ref[...] = _dot(xn_ref[...], wg_ref[...])

    o_ref[...] = _dot(xn_ref[...], w_ref[...])


def _inproj_la(x, nw, w, wg, *, tm=1024, tn=1024):
    t, k = x.shape
    n = w.shape[1]
    return pl.pallas_call(
        functools.partial(_inproj_la_kernel, tm=tm),
        grid=(t // tm, n // tn),
        in_specs=[
            pl.BlockSpec((tm, k), lambda i, j: (i, 0)),
            pl.BlockSpec((1, k), lambda i, j: (0, 0)),
            pl.BlockSpec((k, tn), lambda i, j: (0, j)),
            pl.BlockSpec((k, LANES), lambda i, j: (0, 0)),
        ],
        out_specs=[
            pl.BlockSpec((tm, tn), lambda i, j: (i, j)),
            pl.BlockSpec((tm, LANES), lambda i, j: (i, 0)),
        ],
        out_shape=[jax.ShapeDtypeStruct((t, n), F32), jax.ShapeDtypeStruct((t, LANES), F32)],
        scratch_shapes=[pltpu.VMEM((tm, k), BF16)],
        compiler_params=_params(2, 48),
        name="inproj_la",
    )(x, nw, w, wg)


def _inproj_gelu_kernel(x_ref, nw_ref, w_ref, o_ref, xn_ref, *, tm):
    @pl.when(pl.program_id(1) == 0)
    def _():
        _rmsnorm_rows_to(x_ref, nw_ref, xn_ref, tm)

    o_ref[...] = _gelu_tanh(_dot(xn_ref[...], w_ref[...]))


def _inproj_gelu(x, nw, w, *, tm=1024, tn=1024):
    t, k = x.shape
    n = w.shape[1]
    return pl.pallas_call(
        functools.partial(_inproj_gelu_kernel, tm=tm),
        grid=(t // tm, n // tn),
        in_specs=[
            pl.BlockSpec((tm, k), lambda i, j: (i, 0)),
            pl.BlockSpec((1, k), lambda i, j: (0, 0)),
            pl.BlockSpec((k, tn), lambda i, j: (0, j)),
        ],
        out_specs=pl.BlockSpec((tm, tn), lambda i, j: (i, j)),
        out_shape=jax.ShapeDtypeStruct((t, n), F32),
        scratch_shapes=[pltpu.VMEM((tm, k), BF16)],
        compiler_params=_params(2, 48),
        name="inproj_gelu",
    )(x, nw, w)


def _out_rms_res_kernel(*refs, n_pairs, tm):
    a_refs = refs[:n_pairs]
    w_refs = refs[n_pairs:2 * n_pairs]
    h_ref, nw_ref, o_ref, acc_ref = refs[2 * n_pairs:]
    y = _dot(a_refs[0][...], w_refs[0][...])
    for a_ref, w_ref in zip(a_refs[1:], w_refs[1:]):
        y = y + _dot(a_ref[...], w_ref[...])
    acc_ref[...] = y
    _rms_residual_rows(acc_ref, h_ref, nw_ref, o_ref, tm)


def _out_rms_res(a_list, w_list, h, nw, *, tm, vmem_mib):
    t, n = h.shape
    n_pairs = len(a_list)
    in_specs = [pl.BlockSpec((tm, a.shape[1]), lambda i: (i, 0)) for a in a_list]
    in_specs += [pl.BlockSpec(w.shape, lambda i: (0, 0), pipeline_mode=pl.Buffered(1)) for w in w_list]
    in_specs += [pl.BlockSpec((tm, n), lambda i: (i, 0)), pl.BlockSpec((1, n), lambda i: (0, 0))]
    return pl.pallas_call(
        functools.partial(_out_rms_res_kernel, n_pairs=n_pairs, tm=tm),
        grid=(t // tm,),
        in_specs=in_specs,
        out_specs=pl.BlockSpec((tm, n), lambda i: (i, 0)),
        out_shape=jax.ShapeDtypeStruct((t, n), F32),
        scratch_shapes=[pltpu.VMEM((tm, n), F32)],
        compiler_params=_params(1, vmem_mib),
        name="out_rms_res",
    )(*a_list, *w_list, h, nw)


def _ffn_kernel(h_ref, nw_in_ref, wup_ref, wdn_ref, nw_out_ref, o_ref, xn_ref, acc_ref, *, tm):
    j = pl.program_id(1)

    @pl.when(j == 0)
    def _():
        _rmsnorm_rows_to(h_ref, nw_in_ref, xn_ref, tm)
        acc_ref[...] = jnp.zeros_like(acc_ref)

    u = _dot(xn_ref[...], wup_ref[...])
    u = jnp.square(jnp.maximum(u, 0.0)).astype(BF16)
    acc_ref[...] += _dot(u, wdn_ref[...])

    @pl.when(j == pl.num_programs(1) - 1)
    def _():
        _rms_residual_rows(acc_ref, h_ref, nw_out_ref, o_ref, tm)


def _ffn(h, nw_in, wup, wdn, nw_out, *, tm=512, th=512):
    t, d = h.shape
    hidden = wup.shape[1]
    return pl.pallas_call(
        functools.partial(_ffn_kernel, tm=tm),
        grid=(t // tm, hidden // th),
        in_specs=[
            pl.BlockSpec((tm, d), lambda i, j: (i, 0)),
            pl.BlockSpec((1, d), lambda i, j: (0, 0)),
            pl.BlockSpec((d, th), lambda i, j: (0, j)),
            pl.BlockSpec((th, d), lambda i, j: (j, 0)),
            pl.BlockSpec((1, d), lambda i, j: (0, 0)),
        ],
        out_specs=pl.BlockSpec((tm, d), lambda i, j: (i, 0)),
        out_shape=jax.ShapeDtypeStruct((t, d), F32),
        scratch_shapes=[pltpu.VMEM((tm, d), BF16), pltpu.VMEM((tm, d), F32)],
        compiler_params=_params(2, 48),
        name="ffn",
    )(h, nw_in, wup, wdn, nw_out)


def _shift_rows(x, prev8, j):
    r = pltpu.roll(x, j, axis=0)
    rp = pltpu.roll(prev8, j, axis=0)
    row = lax.broadcasted_iota(jnp.int32, (8, LANES), 0)
    top = jnp.where(row < j, rp, r[0:8])
    return jnp.concatenate([top, r[8:]], axis=0)


def _pair_masks():
    row = lax.broadcasted_iota(jnp.int32, (PAIR, PAIR), 0)
    col = lax.broadcasted_iota(jnp.int32, (PAIR, PAIR), 1)
    same_chunk = (row >> 6) == (col >> 6)
    return row, col, same_chunk


def _gdn_kernel(q_ref, k_ref, v_ref, z_ref, gate_ref, cw_ref, gp_ref, onw_ref, o_ref,
                carry_ref, s_ref, *, tl):
    head = pl.program_id(1)

    @pl.when(pl.program_id(2) == 0)
    def _():
        carry_ref[...] = jnp.zeros_like(carry_ref)
        s_ref[...] = jnp.zeros_like(s_ref)

    row, col, same_chunk = _pair_masks()
    causal = same_chunk & (col <= row)
    strict = same_chunk & (col < row)
    eye = (row == col).astype(F32)
    row_in_chunk = row & (LA_CHUNK - 1)
    rowc = row[:, 0:1]
    first_chunk = rowc < LA_CHUNK

    a_coef = -jnp.exp(gp_ref[0:1, :])
    dt_bias = gp_ref[1:2, :]
    onw = onw_ref[...]
    s = s_ref[...]

    for sc in range(tl // PAIR):
        r0 = sc * PAIR

        def conv(x_ref, ci):
            x = x_ref[r0:r0 + PAIR, :]
            prev = carry_ref[ci] if sc == 0 else x_ref[r0 - 8:r0, :]
            w = cw_ref[ci]
            acc = x * w[CONV_WIDTH - 1:CONV_WIDTH, :]
            for j in range(1, CONV_WIDTH):
                acc = acc + _shift_rows(x, prev, j) * w[CONV_WIDTH - 1 - j:CONV_WIDTH - j, :]
            return _silu(acc)

        q = conv(q_ref, 0)
        k = conv(k_ref, 1)
        v = conv(v_ref, 2)
        q = q * lax.rsqrt(jnp.sum(q * q, axis=-1, keepdims=True) + 1e-6) * (HEAD_DIM ** -0.5)
        k = k * lax.rsqrt(jnp.sum(k * k, axis=-1, keepdims=True) + 1e-6)

        pre = gate_ref[r0:r0 + PAIR, :]
        beta_all = jax.nn.sigmoid(pre)
        g_all = a_coef * _softplus(pre + dt_bias)
        for sh in (1, 2, 4, 8, 16, 32):
            g_all = g_all + jnp.where(row_in_chunk >= sh, pltpu.roll(g_all, sh, axis=0), 0.0)
        g_all_t = g_all.T
        gc_col = jnp.sum(jnp.where(col == head + N_HEADS, g_all, 0.0), axis=1, keepdims=True)
        gc_row = jnp.sum(jnp.where(row == head + N_HEADS, g_all_t, 0.0), axis=0, keepdims=True)
        beta = jnp.sum(jnp.where(col == head, beta_all, 0.0), axis=1, keepdims=True)
        glast = [jnp.sum(jnp.where(rowc == (a + 1) * LA_CHUNK - 1, gc_col, 0.0), axis=0, keepdims=True)
                 for a in range(2)]
        gc_last = jnp.where(first_chunk, glast[0], glast[1])

        eg = jnp.exp(gc_col)
        kb = k * beta
        vb = v * beta
        kbg = kb * eg
        q_dec = q * eg
        k_tail = k * jnp.exp(gc_last - gc_col)
        decay = jnp.exp(jnp.where(causal, gc_col - gc_row, -jnp.inf))

        qkb = jnp.concatenate([q, kb], axis=0).astype(BF16)
        r = lax.dot_general(qkb, k.astype(BF16), NT_DIMS, preferred_element_type=F32)
        qk = r[0:PAIR] * decay
        a_mat = jnp.where(strict, r[PAIR:2 * PAIR] * decay, 0.0)

        x_inv = eye - a_mat
        p = a_mat
        for _ in range(5):
            pb = p.astype(BF16)
            p = _dot(pb, pb)
            x_inv = x_inv + _dot(x_inv.astype(BF16), p.astype(BF16))

        rhs = jnp.concatenate([vb, kbg], axis=1).astype(BF16)
        uw = _dot(x_inv.astype(BF16), rhs)
        u = uw[:, 0:HEAD_DIM]
        w = uw[:, HEAD_DIM:2 * HEAD_DIM]
        k_tail_t = k_tail.T.astype(BF16)
        zeros = jnp.zeros((LA_CHUNK, HEAD_DIM), F32)

        v_new_parts = []
        qs_parts = []
        for a in range(2):
            ra = slice(a * LA_CHUNK, (a + 1) * LA_CHUNK)
            lhs = jnp.concatenate([w[ra], q_dec[ra]], axis=0).astype(BF16)
            r2 = _dot(lhs, s.astype(BF16))
            v_new = u[ra] - r2[0:LA_CHUNK]
            qs_parts.append(r2[LA_CHUNK:2 * LA_CHUNK])
            v_new_parts.append(v_new)
            v_pad = jnp.concatenate([v_new, zeros] if a == 0 else [zeros, v_new], axis=0).astype(BF16)
            s = s * jnp.exp(glast[a]) + _dot(k_tail_t, v_pad)

        v_new_pair = jnp.concatenate(v_new_parts, axis=0).astype(BF16)
        o = jnp.concatenate(qs_parts, axis=0) + _dot(qk.astype(BF16), v_new_pair)
        zz = z_ref[r0:r0 + PAIR, :]
        o = o * lax.rsqrt(jnp.mean(o * o, axis=-1, keepdims=True) + EPS) * onw * _silu(zz)
        o_ref[r0:r0 + PAIR, :] = o.astype(o_ref.dtype)

    s_ref[...] = s
    carry_ref[0] = q_ref[tl - 8:tl, :]
    carry_ref[1] = k_ref[tl - 8:tl, :]
    carry_ref[2] = v_ref[tl - 8:tl, :]


def _gdn(proj, gates, cw, gp, onw, *, batch, seq, tl=512):
    nblk = seq // tl
    hcols = N_HEADS

    def seg(s_idx):
        return pl.BlockSpec((tl, HEAD_DIM), lambda b, h, i: (b * nblk + i, s_idx * hcols + h))

    return pl.pallas_call(
        functools.partial(_gdn_kernel, tl=tl),
        grid=(batch, N_HEADS, nblk),
        in_specs=[
            seg(0), seg(1), seg(2), seg(3),
            pl.BlockSpec((tl, LANES), lambda b, h, i: (b * nblk + i, 0)),
            pl.BlockSpec((3, None, CONV_WIDTH, HEAD_DIM), lambda b, h, i: (0, h, 0, 0)),
            pl.BlockSpec((8, LANES), lambda b, h, i: (0, 0)),
            pl.BlockSpec((1, HEAD_DIM), lambda b, h, i: (0, 0)),
        ],
        out_specs=pl.BlockSpec((tl, HEAD_DIM), lambda b, h, i: (b * nblk + i, h)),
        out_shape=jax.ShapeDtypeStruct((batch * seq, N_HEADS * HEAD_DIM), BF16),
        scratch_shapes=[pltpu.VMEM((3, 8, HEAD_DIM), F32), pltpu.VMEM((HEAD_DIM, HEAD_DIM), F32)],
        compiler_params=_params(3, 32),
        name="gdn",
    )(proj, proj, proj, proj, gates, cw, gp, onw)


def _ret_kernel(q_ref, k_ref, v_ref, g_ref, cos_ref, sin_ref, dmat_ref, rp_ref, o_ref, s_ref, *, tl):
    @pl.when(pl.program_id(2) == 0)
    def _():
        s_ref[...] = jnp.zeros_like(s_ref)

    dmat = dmat_ref[...]
    q_scale = rp_ref[:, 0:1]
    k_scale = rp_ref[:, 1:2]
    chunk_decay = rp_ref[:, 2:3]
    s = s_ref[...]
    zeros = jnp.zeros((LA_CHUNK, HEAD_DIM), F32)

    for sc in range(tl // PAIR):
        rows = slice(sc * PAIR, (sc + 1) * PAIR)
        cosf = cos_ref[rows, :]
        sinf = sin_ref[rows, :]
        q = q_ref[rows, :]
        k = k_ref[rows, :]
        v = v_ref[rows, :]
        q = q * cosf + pltpu.roll(q, HEAD_DIM // 2, axis=1) * sinf
        k = (k * cosf + pltpu.roll(k, HEAD_DIM // 2, axis=1) * sinf) * (HEAD_DIM ** -0.5)

        qk = lax.dot_general(q.astype(BF16), k.astype(BF16), NT_DIMS, preferred_element_type=F32) * dmat
        vb = v.astype(BF16)
        inner = _dot(qk.astype(BF16), vb)
        q_dec = (q * q_scale).astype(BF16)
        k_dec_t = (k * k_scale).T.astype(BF16)

        cross = []
        for a in range(2):
            ra = slice(a * LA_CHUNK, (a + 1) * LA_CHUNK)
            cross.append(_dot(q_dec[ra], s.astype(BF16)))
            v_pad = jnp.concatenate([v[ra], zeros] if a == 0 else [zeros, v[ra]], axis=0).astype(BF16)
            s = s * chunk_decay + _dot(k_dec_t, v_pad)

        o = inner + jnp.concatenate(cross, axis=0)
        o = o * lax.rsqrt(jnp.mean(o * o, axis=-1, keepdims=True) + EPS)
        o_ref[rows, :] = (_silu(g_ref[rows, :]) * o).astype(o_ref.dtype)

    s_ref[...] = s


def _ret(proj, cosf, sinf, dmat, rp, *, batch, seq, tl=512):
    nblk = seq // tl
    hcols = N_HEADS

    def seg(s_idx):
        return pl.BlockSpec((tl, HEAD_DIM), lambda b, h, i: (b * nblk + i, s_idx * hcols + h))

    return pl.pallas_call(
        functools.partial(_ret_kernel, tl=tl),
        grid=(batch, N_HEADS, nblk),
        in_specs=[
            seg(4), seg(5), seg(6), seg(7),
            pl.BlockSpec((tl, HEAD_DIM), lambda b, h, i: (i, 0)),
            pl.BlockSpec((tl, HEAD_DIM), lambda b, h, i: (i, 0)),
            pl.BlockSpec((None, PAIR, PAIR), lambda b, h, i: (h, 0, 0)),
            pl.BlockSpec((None, PAIR, LANES), lambda b, h, i: (h, 0, 0)),
        ],
        out_specs=pl.BlockSpec((tl, HEAD_DIM), lambda b, h, i: (b * nblk + i, h)),
        out_shape=jax.ShapeDtypeStruct((batch * seq, N_HEADS * HEAD_DIM), BF16),
        scratch_shapes=[pltpu.VMEM((HEAD_DIM, HEAD_DIM), F32)],
        compiler_params=_params(3, 32),
        name="retention",
    )(proj, proj, proj, proj, cosf, sinf, dmat, rp)


def _sg_gate_kernel(u_ref, v_ref, lnw_ref, lnb_ref, ws_ref, bs_ref, o_ref, vn_ref, *, tm):
    lnw = lnw_ref[...]
    lnb = lnb_ref[...]
    step = 16

    def body(r, carry):
        sl = pl.ds(pl.multiple_of(r * step, step), step)
        x = v_ref[sl, :]
        mu = jnp.mean(x, axis=-1, keepdims=True)
        xc = x - mu
        var = jnp.mean(xc * xc, axis=-1, keepdims=True)
        vn_ref[sl, :] = (xc * lax.rsqrt(var + EPS) * lnw + lnb).astype(vn_ref.dtype)
        return carry

    lax.fori_loop(0, tm // step, body, 0)

    row = lax.broadcasted_iota(jnp.int32, (SG_CHUNK, SG_CHUNK), 0)
    col = lax.broadcasted_iota(jnp.int32, (SG_CHUNK, SG_CHUNK), 1)
    for g in range(SG_GROUPS):
        wg = jnp.where(col <= row, ws_ref[g], 0.0).astype(BF16)
        bias = bs_ref[:, g:g + 1]
        cols = slice(g * SG_GROUP_DIM, (g + 1) * SG_GROUP_DIM)
        for c in range(tm // SG_CHUNK):
            rows = slice(c * SG_CHUNK, (c + 1) * SG_CHUNK)
            sgate = _dot(wg, vn_ref[rows, cols]) + bias
            o_ref[rows, cols] = (u_ref[rows, cols] * sgate).astype(o_ref.dtype)


def _sg_gate(proj, lnw, lnb, ws, bs_t, *, tm=256):
    t = proj.shape[0]
    return pl.pallas_call(
        functools.partial(_sg_gate_kernel, tm=tm),
        grid=(t // tm,),
        in_specs=[
            pl.BlockSpec((tm, SG_WIDTH), lambda i: (i, 0)),
            pl.BlockSpec((tm, SG_WIDTH), lambda i: (i, 1)),
            pl.BlockSpec((1, SG_WIDTH), lambda i: (0, 0)),
            pl.BlockSpec((1, SG_WIDTH), lambda i: (0, 0)),
            pl.BlockSpec((SG_GROUPS, SG_CHUNK, SG_CHUNK), lambda i: (0, 0, 0)),
            pl.BlockSpec((SG_CHUNK, SG_GROUPS), lambda i: (0, 0)),
        ],
        out_specs=pl.BlockSpec((tm, SG_WIDTH), lambda i: (i, 0)),
        out_shape=jax.ShapeDtypeStruct((t, SG_WIDTH), BF16),
        scratch_shapes=[pltpu.VMEM((tm, SG_WIDTH), BF16)],
        compiler_params=_params(1, 40),
        name="sg_gate",
    )(proj, proj, lnw, lnb, ws, bs_t)


def _retention_tables(seq):
    half = HEAD_DIM // 2
    inv_freq = 1.0 / (ROPE_BASE ** jnp.linspace(0.0, 1.0, half, dtype=F32))
    ang = jnp.arange(seq, dtype=F32)[:, None] * inv_freq[None, :]
    cos = jnp.cos(ang)
    sin = jnp.sin(ang)
    cosf = jnp.concatenate([cos, cos], axis=1)
    sinf = jnp.concatenate([-sin, sin], axis=1)

    log_gamma = jnp.log1p(-jnp.power(2.0, -5.0 - jnp.arange(N_HEADS, dtype=F32)))
    t = jnp.arange(PAIR)
    pos = (t % LA_CHUNK).astype(F32)
    mask = ((t[:, None] // LA_CHUNK) == (t[None, :] // LA_CHUNK)) & (t[None, :] <= t[:, None])
    dmat = jnp.exp(jnp.where(mask[None], (pos[:, None] - pos[None, :])[None] * log_gamma[:, None, None], -jnp.inf))
    q_scale = jnp.exp((pos[None, :] + 1.0) * log_gamma[:, None])
    k_scale = jnp.exp((LA_CHUNK - 1.0 - pos[None, :]) * log_gamma[:, None])
    chunk_decay = jnp.broadcast_to(jnp.exp(LA_CHUNK * log_gamma)[:, None], (N_HEADS, PAIR))
    rp = jnp.stack([q_scale, k_scale, chunk_decay], axis=-1)
    rp = jnp.pad(rp, ((0, 0), (0, 0), (0, LANES - 3)))
    return cosf, sinf, dmat, rp


def kernel(x, norm_w, la_w_in, la_conv_w, la_a_log, la_dt_bias, la_out_norm_w, la_w_out, sg_w_in, sg_ln_w,
           sg_ln_b, sg_w_s, sg_b_s, sg_w_out, ffn_w_up, ffn_w_down):
    batch, seq, d = x.shape
    t = batch * seq
    h = x.reshape(t, d)
    qk_w = N_HEADS * HEAD_DIM
    gate0 = 4 * qk_w

    w_in = la_w_in[0]
    w_main = jnp.concatenate([w_in[:, :gate0], w_in[:, gate0 + 2 * N_HEADS:]], axis=1).astype(BF16)
    w_gate = jnp.pad(w_in[:, gate0:gate0 + 2 * N_HEADS], ((0, 0), (0, LANES - 2 * N_HEADS))).astype(BF16)
    proj, gates = _inproj_la(h, norm_w[0, 0][None, :], w_main, w_gate)

    cw = la_conv_w[0].reshape(3, N_HEADS, HEAD_DIM, CONV_WIDTH).transpose(0, 1, 3, 2)
    gp = jnp.zeros((8, LANES), F32)
    gp = gp.at[0, N_HEADS:2 * N_HEADS].set(la_a_log[0]).at[1, N_HEADS:2 * N_HEADS].set(la_dt_bias[0])
    o_a = _gdn(proj, gates, cw, gp, la_out_norm_w[0][None, :], batch=batch, seq=seq)

    cosf, sinf, dmat, rp = _retention_tables(seq)
    o_b = _ret(proj, cosf, sinf, dmat, rp, batch=batch, seq=seq)

    w_out = la_w_out[0].astype(BF16)
    h = _out_rms_res([o_a, o_b], [w_out[:qk_w], w_out[qk_w:]], h, norm_w[0, 1][None, :], tm=512, vmem_mib=48)
    h = _ffn(h, norm_w[0, 2][None, :], ffn_w_up[0].astype(BF16), ffn_w_down[0].astype(BF16), norm_w[0, 3][None, :])

    proj1 = _inproj_gelu(h, norm_w[1, 0][None, :], sg_w_in[0].astype(BF16))
    gated = _sg_gate(proj1, sg_ln_w[0][None, :], sg_ln_b[0][None, :], sg_w_s[0], sg_b_s[0].T)
    h = _out_rms_res([gated], [sg_w_out[0].astype(BF16)], h, norm_w[1, 1][None, :], tm=256, vmem_mib=48)
    h = _ffn(h, norm_w[1, 2][None, :], ffn_w_up[1].astype(BF16), ffn_w_down[1].astype(BF16), norm_w[1, 3][None, :])
    return h.reshape(batch, seq, d)
```

```python
import functools
import math

import jax
import jax.numpy as jnp
from jax import lax
from jax.experimental import pallas as pl
from jax.experimental.pallas import tpu as pltpu

F32 = jnp.float32
BF16 = jnp.bfloat16

D_MODEL = 2048
N_HEADS = 8
HEAD_DIM = 128
QK_WIDTH = N_HEADS * HEAD_DIM
LA_CHUNK = 64
PAIR = 2 * LA_CHUNK
CONV_WIDTH = 4
HALO = 8
ROPE_BASE = 10000.0
SG_CHUNK = 128
SG_GROUPS = 8
SG_WIDTH = 2 * D_MODEL
SG_GROUP_DIM = SG_WIDTH // SG_GROUPS
FFN_HIDDEN = 4 * D_MODEL
EPS = 1e-6
LANES = 128
MIB = 1024 * 1024

NT_DIMS = (((1,), (1,)), ((), ()))


def _params(n_grid_axes, vmem_mib):
    return pltpu.CompilerParams(dimension_semantics=("arbitrary",) * n_grid_axes,
                                vmem_limit_bytes=vmem_mib * MIB)


def _dot(a, b):
    return jnp.dot(a, b, preferred_element_type=F32)


def _dot_nt(a, b):
    return lax.dot_general(a, b, NT_DIMS, preferred_element_type=F32)


def _silu(x):
    return x * jax.nn.sigmoid(x)


def _softplus(x):
    return jnp.maximum(x, 0.0) + jnp.log1p(jnp.exp(-jnp.abs(x)))


def _gelu_tanh(x):
    c = math.sqrt(2.0 / math.pi)
    return x * (0.5 * (1.0 + jnp.tanh(c * (x + 0.044715 * (x * x * x)))))


def _rmsnorm_rows_to(src_ref, nw_ref, dst_ref, rows):
    nw = nw_ref[...]
    step = 16

    def body(r, carry):
        sl = pl.ds(pl.multiple_of(r * step, step), step)
        x = src_ref[sl, :]
        ms = jnp.mean(x * x, axis=-1, keepdims=True)
        dst_ref[sl, :] = (x * lax.rsqrt(ms + EPS) * nw).astype(dst_ref.dtype)
        return carry

    lax.fori_loop(0, rows // step, body, 0)


def _rms_residual_rows(acc_ref, h_ref, nw_ref, o_ref, rows):
    nw = nw_ref[...]
    step = 16

    def body(r, carry):
        sl = pl.ds(pl.multiple_of(r * step, step), step)
        y = acc_ref[sl, :]
        ms = jnp.mean(y * y, axis=-1, keepdims=True)
        o_ref[sl, :] = h_ref[sl, :] + y * lax.rsqrt(ms + EPS) * nw
        return carry

    lax.fori_loop(0, rows // step, body, 0)


def _inproj_la_kernel(x_ref, nw_ref, w_ref, wg_ref, o_ref, g_ref, xn_ref, *, tm):
    @pl.when(pl.program_id(1) == 0)
    def _():
        _rmsnorm_rows_to(x_ref, nw_ref, xn_ref, tm)
        g_ref[...] = _dot(xn_ref[...], wg_ref[...])

    o_ref[...] = _dot(xn_ref[...], w_ref[...])


def _inproj_la(x, nw, w, wg, *, tm=1024, tn=1024):
    t, k = x.shape
    n = w.shape[1]
    return pl.pallas_call(
        functools.partial(_inproj_la_kernel, tm=tm),
        grid=(t // tm, n // tn),
        in_specs=[
            pl.BlockSpec((tm, k), lambda i, j: (i, 0)),
            pl.BlockSpec((1, k), lambda i, j: (0, 0)),
            pl.BlockSpec((k, tn), lambda i, j: (0, j)),
            pl.BlockSpec((k, LANES), lambda i, j: (0, 0)),
        ],
        out_specs=[
            pl.BlockSpec((tm, tn), lambda i, j: (i, j)),
            pl.BlockSpec((tm, LANES), lambda i, j: (i, 0)),
        ],
        out_shape=[jax.ShapeDtypeStruct((t, n), F32), jax.ShapeDtypeStruct((t, LANES), F32)],
        scratch_shapes=[pltpu.VMEM((tm, k), BF16)],
        compiler_params=_params(2, 48),
        name="inproj_la",
    )(x, nw, w, wg)


def _inproj_gelu_kernel(x_ref, nw_ref, w_ref, o_ref, xn_ref, *, tm):
    @pl.when(pl.program_id(1) == 0)
    def _():
        _rmsnorm_rows_to(x_ref, nw_ref, xn_ref, tm)

    o_ref[...] = _gelu_tanh(_dot(xn_ref[...], w_ref[...]))


def _inproj_gelu(x, nw, w, *, tm=1024, tn=1024):
    t, k = x.shape
    n = w.shape[1]
    return pl.pallas_call(
        functools.partial(_inproj_gelu_kernel, tm=tm),
        grid=(t // tm, n // tn),
        in_specs=[
            pl.BlockSpec((tm, k), lambda i, j: (i, 0)),
            pl.BlockSpec((1, k), lambda i, j: (0, 0)),
            pl.BlockSpec((k, tn), lambda i, j: (0, j)),
        ],
        out_specs=pl.BlockSpec((tm, tn), lambda i, j: (i, j)),
        out_shape=jax.ShapeDtypeStruct((t, n), F32),
        scratch_shapes=[pltpu.VMEM((tm, k), BF16)],
        compiler_params=_params(2, 48),
        name="inproj_gelu",
    )(x, nw, w)


def _out_rms_res_kernel(*refs, n_pairs, tm):
    a_refs = refs[:n_pairs]
    w_refs = refs[n_pairs:2 * n_pairs]
    h_ref, nw_ref, o_ref, acc_ref = refs[2 * n_pairs:]
    y = _dot(a_refs[0][...], w_refs[0][...])
    for a_ref, w_ref in zip(a_refs[1:], w_refs[1:]):
        y = y + _dot(a_ref[...], w_ref[...])
    acc_ref[...] = y
    _rms_residual_rows(acc_ref, h_ref, nw_ref, o_ref, tm)


def _out_rms_res(a_list, w_list, h, nw, *, tm, vmem_mib):
    t, n = h.shape
    n_pairs = len(a_list)
    in_specs = [pl.BlockSpec((tm, a.shape[1]), lambda i: (i, 0)) for a in a_list]
    in_specs += [pl.BlockSpec(w.shape, lambda i: (0, 0), pipeline_mode=pl.Buffered(1)) for w in w_list]
    in_specs += [pl.BlockSpec((tm, n), lambda i: (i, 0)), pl.BlockSpec((1, n), lambda i: (0, 0))]
    return pl.pallas_call(
        functools.partial(_out_rms_res_kernel, n_pairs=n_pairs, tm=tm),
        grid=(t // tm,),
        in_specs=in_specs,
        out_specs=pl.BlockSpec((tm, n), lambda i: (i, 0)),
        out_shape=jax.ShapeDtypeStruct((t, n), F32),
        scratch_shapes=[pltpu.VMEM((tm, n), F32)],
        compiler_params=_params(1, vmem_mib),
        name="out_rms_res",
    )(*a_list, *w_list, h, nw)


def _ffn_kernel(h_ref, nw_in_ref, wup_ref, wdn_ref, nw_out_ref, o_ref, xn_ref, acc_ref, *, tm):
    j = pl.program_id(1)

    @pl.when(j == 0)
    def _():
        _rmsnorm_rows_to(h_ref, nw_in_ref, xn_ref, tm)
        acc_ref[...] = jnp.zeros_like(acc_ref)

    u = _dot(xn_ref[...], wup_ref[...])
    u = jnp.square(jnp.maximum(u, 0.0)).astype(BF16)
    acc_ref[...] += _dot(u, wdn_ref[...])

    @pl.when(j == pl.num_programs(1) - 1)
    def _():
        _rms_residual_rows(acc_ref, h_ref, nw_out_ref, o_ref, tm)


def _ffn(h, nw_in, wup, wdn, nw_out, *, tm=512, th=512):
    t, d = h.shape
    hidden = wup.shape[1]
    return pl.pallas_call(
        functools.partial(_ffn_kernel, tm=tm),
        grid=(t // tm, hidden // th),
        in_specs=[
            pl.BlockSpec((tm, d), lambda i, j: (i, 0)),
            pl.BlockSpec((1, d), lambda i, j: (0, 0)),
            pl.BlockSpec((d, th), lambda i, j: (0, j)),
            pl.BlockSpec((th, d), lambda i, j: (j, 0)),
            pl.BlockSpec((1, d), lambda i, j: (0, 0)),
        ],
        out_specs=pl.BlockSpec((tm, d), lambda i, j: (i, 0)),
        out_shape=jax.ShapeDtypeStruct((t, d), F32),
        scratch_shapes=[pltpu.VMEM((tm, d), BF16), pltpu.VMEM((tm, d), F32)],
        compiler_params=_params(2, 48),
        name="ffn",
    )(h, nw_in, wup, wdn, nw_out)


def _pair_masks():
    row = lax.broadcasted_iota(jnp.int32, (PAIR, PAIR), 0)
    col = lax.broadcasted_iota(jnp.int32, (PAIR, PAIR), 1)
    same_chunk = (row >> 6) == (col >> 6)
    return row, col, same_chunk


def _chunk_halves(parts, a):
    zeros = jnp.zeros_like(parts)
    return jnp.concatenate([parts, zeros] if a == 0 else [zeros, parts], axis=0)


def _gdn_kernel(q_ref, k_ref, v_ref, z_ref, gate_ref, cw_ref, gp_ref, onw_ref, o_ref,
                xpad_ref, s_ref, *, tl):
    nh = N_HEADS
    heads = range(nh)

    @pl.when(pl.program_id(1) == 0)
    def _():
        xpad_ref[:, 0:HALO, :] = jnp.zeros((3, HALO, QK_WIDTH), F32)
        s_ref[...] = jnp.zeros_like(s_ref)

    xpad_ref[0, HALO:HALO + tl, :] = q_ref[...]
    xpad_ref[1, HALO:HALO + tl, :] = k_ref[...]
    xpad_ref[2, HALO:HALO + tl, :] = v_ref[...]

    row, col, same_chunk = _pair_masks()
    causal = same_chunk & (col <= row)
    strict = same_chunk & (col < row)
    eye = (row == col).astype(F32)
    row_in_chunk = row & (LA_CHUNK - 1)
    lane8 = lax.broadcasted_iota(jnp.int32, (nh, PAIR), 1)

    a_coef = -jnp.exp(gp_ref[0:1, :])
    dt_bias = gp_ref[1:2, :]
    onw = onw_ref[...]
    s = [s_ref[h] for h in heads]

    for p in range(tl // PAIR):
        r0 = p * PAIR
        rows = slice(r0, r0 + PAIR)

        pre = gate_ref[rows, :]
        beta_all = jax.nn.sigmoid(pre)
        g = a_coef * _softplus(pre + dt_bias)
        for sh in (1, 2, 4, 8, 16, 32):
            g = g + jnp.where(row_in_chunk >= sh, pltpu.roll(g, sh, axis=0), 0.0)
        gt8 = g.T[nh:2 * nh, :]
        bt8 = beta_all.T[0:nh, :]
        g_last = [jnp.sum(jnp.where(lane8 == (a + 1) * LA_CHUNK - 1, gt8, 0.0), axis=1, keepdims=True)
                  for a in range(2)]
        k_tail8 = jnp.exp(jnp.where(lane8 < LA_CHUNK, g_last[0], g_last[1]) - gt8)
        be8 = bt8 * jnp.exp(gt8)
        chunk_decay = [jnp.exp(gl) for gl in g_last]

        def conv(ci, cs):
            w = cw_ref[ci, :, cs]
            acc = xpad_ref[ci, HALO + r0:HALO + r0 + PAIR, cs] * w[CONV_WIDTH - 1:CONV_WIDTH, :]
            for j in range(1, CONV_WIDTH):
                acc = acc + xpad_ref[ci, HALO + r0 - j:HALO + r0 - j + PAIR, cs] * w[CONV_WIDTH - 1 - j:CONV_WIDTH - j, :]
            return _silu(acc)

        q_l, k_l, v_l, kbf_l, decay_l, m1_l, lhs_l = [], [], [], [], [], [], []
        for h in heads:
            cs = slice(h * HEAD_DIM, (h + 1) * HEAD_DIM)
            q = conv(0, cs)
            k = conv(1, cs)
            v = conv(2, cs)
            q = q * lax.rsqrt(jnp.sum(q * q, axis=-1, keepdims=True) + 1e-6) * (HEAD_DIM ** -0.5)
            k = k * lax.rsqrt(jnp.sum(k * k, axis=-1, keepdims=True) + 1e-6)
            gc_col = jnp.sum(jnp.where(col == nh + h, g, 0.0), axis=1, keepdims=True)
            beta_col = jnp.sum(jnp.where(col == h, beta_all, 0.0), axis=1, keepdims=True)
            decay = jnp.exp(jnp.where(causal, gc_col - gt8[h:h + 1, :], -jnp.inf))
            q_l.append(q * jnp.exp(gc_col))
            k_l.append(k)
            v_l.append(v.astype(BF16))
            kbf = k.astype(BF16)
            kbf_l.append(kbf)
            decay_l.append(decay)
            m1_l.append(beta_col * decay)
            lhs_l.append(jnp.concatenate([q.astype(BF16), kbf], axis=0))

        r_l = [_dot_nt(lhs_l[h], kbf_l[h]) for h in heads]
        qk_l = [(r_l[h][0:PAIR] * decay_l[h]).astype(BF16) for h in heads]
        a_l = [jnp.where(strict, r_l[h][PAIR:2 * PAIR] * m1_l[h], 0.0) for h in heads]
        x_l = [eye - a_l[h] for h in heads]
        pb_l = [a_l[h].astype(BF16) for h in heads]
        for _ in range(5):
            pb_l = [_dot(pb_l[h], pb_l[h]).astype(BF16) for h in heads]
            x_l = [x_l[h] + _dot(x_l[h].astype(BF16), pb_l[h]) for h in heads]

        u_l = [_dot((x_l[h] * bt8[h:h + 1, :]).astype(BF16), v_l[h]) for h in heads]
        w_l = [_dot((x_l[h] * be8[h:h + 1, :]).astype(BF16), kbf_l[h]) for h in heads]
        kt_l = [(k_l[h].T * k_tail8[h:h + 1, :]).astype(BF16) for h in heads]

        v_new_l = [[], []]
        qs_l = [[], []]
        for a in range(2):
            ra = slice(a * LA_CHUNK, (a + 1) * LA_CHUNK)
            r2_l = [_dot(jnp.concatenate([w_l[h][ra], q_l[h][ra]], axis=0).astype(BF16), s[h].astype(BF16))
                    for h in heads]
            for h in heads:
                v_new = u_l[h][ra] - r2_l[h][0:LA_CHUNK]
                v_new_l[a].append(v_new)
                qs_l[a].append(r2_l[h][LA_CHUNK:2 * LA_CHUNK])
            upd_l = [_dot(kt_l[h], _chunk_halves(v_new_l[a][h], a).astype(BF16)) for h in heads]
            s = [s[h] * chunk_decay[a][h:h + 1, :] + upd_l[h] for h in heads]

        o_l = [jnp.concatenate([qs_l[0][h], qs_l[1][h]], axis=0)
               + _dot(qk_l[h], jnp.concatenate([v_new_l[0][h], v_new_l[1][h]], axis=0).astype(BF16))
               for h in heads]
        for h in heads:
            cs = slice(h * HEAD_DIM, (h + 1) * HEAD_DIM)
            o = o_l[h]
            o = o * lax.rsqrt(jnp.mean(o * o, axis=-1, keepdims=True) + EPS) * onw * _silu(z_ref[rows, cs])
            o_ref[rows, cs] = o.astype(o_ref.dtype)

    for h in heads:
        s_ref[h] = s[h]
    xpad_ref[:, 0:HALO, :] = xpad_ref[:, tl:tl + HALO, :]


def _gdn(proj, gates, cw, gp, onw, *, batch, seq, tl=128):
    nblk = seq // tl

    def seg(s_idx):
        return pl.BlockSpec((tl, QK_WIDTH), lambda b, i: (b * nblk + i, s_idx))

    return pl.pallas_call(
        functools.partial(_gdn_kernel, tl=tl),
        grid=(batch, nblk),
        in_specs=[
            seg(0), seg(1), seg(2), seg(3),
            pl.BlockSpec((tl, LANES), lambda b, i: (b * nblk + i, 0)),
            pl.BlockSpec((3, CONV_WIDTH, QK_WIDTH), lambda b, i: (0, 0, 0)),
            pl.BlockSpec((8, LANES), lambda b, i: (0, 0)),
            pl.BlockSpec((1, HEAD_DIM), lambda b, i: (0, 0)),
        ],
        out_specs=pl.BlockSpec((tl, QK_WIDTH), lambda b, i: (b * nblk + i, 0)),
        out_shape=jax.ShapeDtypeStruct((batch * seq, QK_WIDTH), BF16),
        scratch_shapes=[pltpu.VMEM((3, HALO + tl, QK_WIDTH), F32), pltpu.VMEM((N_HEADS, HEAD_DIM, HEAD_DIM), F32)],
        compiler_params=_params(2, 32),
        name="gdn",
    )(proj, proj, proj, proj, gates, cw, gp, onw)


def _ret_kernel(q_ref, k_ref, v_ref, g_ref, cos_ref, sin_ref, dmat_ref, qs_ref, ks_ref, cd_ref, o_ref, s_ref, *, tl):
    heads = range(N_HEADS)

    @pl.when(pl.program_id(1) == 0)
    def _():
        s_ref[...] = jnp.zeros_like(s_ref)

    s = [s_ref[h] for h in heads]

    for p in range(tl // PAIR):
        rows = slice(p * PAIR, (p + 1) * PAIR)
        cosf = cos_ref[rows, :]
        sinf = sin_ref[rows, :]

        q_l, kb_l, kdt_l, v_l = [], [], [], []
        for h in heads:
            cs = slice(h * HEAD_DIM, (h + 1) * HEAD_DIM)
            q = q_ref[rows, cs]
            k = k_ref[rows, cs]
            q = q * cosf + pltpu.roll(q, HEAD_DIM // 2, axis=1) * sinf
            k = (k * cosf + pltpu.roll(k, HEAD_DIM // 2, axis=1) * sinf) * (HEAD_DIM ** -0.5)
            q_l.append(q)
            kb_l.append(k.astype(BF16))
            kdt_l.append((k * ks_ref[h]).T.astype(BF16))
            v_l.append(v_ref[rows, cs])

        qk_l = [(_dot_nt(q_l[h].astype(BF16), kb_l[h]) * dmat_ref[h]).astype(BF16) for h in heads]
        inner_l = [_dot(qk_l[h], v_l[h].astype(BF16)) for h in heads]
        qd_l = [(q_l[h] * qs_ref[h]).astype(BF16) for h in heads]

        cross_l = [[], []]
        for a in range(2):
            ra = slice(a * LA_CHUNK, (a + 1) * LA_CHUNK)
            cross_l[a] = [_dot(qd_l[h][ra], s[h].astype(BF16)) for h in heads]
            upd_l = [_dot(kdt_l[h], _chunk_halves(v_l[h][ra], a).astype(BF16)) for h in heads]
            s = [s[h] * cd_ref[h] + upd_l[h] for h in heads]

        for h in heads:
            cs = slice(h * HEAD_DIM, (h + 1) * HEAD_DIM)
            o = inner_l[h] + jnp.concatenate([cross_l[0][h], cross_l[1][h]], axis=0)
            o = o * lax.rsqrt(jnp.mean(o * o, axis=-1, keepdims=True) + EPS)
            o_ref[rows, cs] = (_silu(g_ref[rows, cs]) * o).astype(o_ref.dtype)

    for h in heads:
        s_ref[h] = s[h]


def _ret(proj, cosf, sinf, dmat, q_scale, k_scale, chunk_decay, *, batch, seq, tl=128):
    nblk = seq // tl

    def seg(s_idx):
        return pl.BlockSpec((tl, QK_WIDTH), lambda b, i: (b * nblk + i, s_idx))

    def table():
        return pl.BlockSpec((N_HEADS, PAIR, HEAD_DIM), lambda b, i: (0, 0, 0))

    return pl.pallas_call(
        functools.partial(_ret_kernel, tl=tl),
        grid=(batch, nblk),
        in_specs=[
            seg(4), seg(5), seg(6), seg(7),
            pl.BlockSpec((tl, HEAD_DIM), lambda b, i: (i, 0)),
            pl.BlockSpec((tl, HEAD_DIM), lambda b, i: (i, 0)),
            table(), table(), table(), table(),
        ],
        out_specs=pl.BlockSpec((tl, QK_WIDTH), lambda b, i: (b * nblk + i, 0)),
        out_shape=jax.ShapeDtypeStruct((batch * seq, QK_WIDTH), BF16),
        scratch_shapes=[pltpu.VMEM((N_HEADS, HEAD_DIM, HEAD_DIM), F32)],
        compiler_params=_params(2, 32),
        name="retention",
    )(proj, proj, proj, proj, cosf, sinf, dmat, q_scale, k_scale, chunk_decay)


def _sg_gate_kernel(u_ref, v_ref, lnw_ref, lnb_ref, ws_ref, bs_ref, o_ref, vn_ref, *, tm):
    lnw = lnw_ref[...]
    lnb = lnb_ref[...]
    step = 16

    def body(r, carry):
        sl = pl.ds(pl.multiple_of(r * step, step), step)
        x = v_ref[sl, :]
        mu = jnp.mean(x, axis=-1, keepdims=True)
        xc = x - mu
        var = jnp.mean(xc * xc, axis=-1, keepdims=True)
        vn_ref[sl, :] = (xc * lax.rsqrt(var + EPS) * lnw + lnb).astype(vn_ref.dtype)
        return carry

    lax.fori_loop(0, tm // step, body, 0)

    row = lax.broadcasted_iota(jnp.int32, (SG_CHUNK, SG_CHUNK), 0)
    col = lax.broadcasted_iota(jnp.int32, (SG_CHUNK, SG_CHUNK), 1)
    for g in range(SG_GROUPS):
        wg = jnp.where(col <= row, ws_ref[g], 0.0).astype(BF16)
        bias = bs_ref[:, g:g + 1]
        cols = slice(g * SG_GROUP_DIM, (g + 1) * SG_GROUP_DIM)
        for c in range(tm // SG_CHUNK):
            rows = slice(c * SG_CHUNK, (c + 1) * SG_CHUNK)
            sgate = _dot(wg, vn_ref[rows, cols]) + bias
            o_ref[rows, cols] = (u_ref[rows, cols] * sgate).astype(o_ref.dtype)


def _sg_gate(proj, lnw, lnb, ws, bs_t, *, tm=256):
    t = proj.shape[0]
    return pl.pallas_call(
        functools.partial(_sg_gate_kernel, tm=tm),
        grid=(t // tm,),
        in_specs=[
            pl.BlockSpec((tm, SG_WIDTH), lambda i: (i, 0)),
            pl.BlockSpec((tm, SG_WIDTH), lambda i: (i, 1)),
            pl.BlockSpec((1, SG_WIDTH), lambda i: (0, 0)),
            pl.BlockSpec((1, SG_WIDTH), lambda i: (0, 0)),
            pl.BlockSpec((SG_GROUPS, SG_CHUNK, SG_CHUNK), lambda i: (0, 0, 0)),
            pl.BlockSpec((SG_CHUNK, SG_GROUPS), lambda i: (0, 0)),
        ],
        out_specs=pl.BlockSpec((tm, SG_WIDTH), lambda i: (i, 0)),
        out_shape=jax.ShapeDtypeStruct((t, SG_WIDTH), BF16),
        scratch_shapes=[pltpu.VMEM((tm, SG_WIDTH), BF16)],
        compiler_params=_params(1, 40),
        name="sg_gate",
    )(proj, proj, lnw, lnb, ws, bs_t)


def _retention_tables(seq):
    half = HEAD_DIM // 2
    inv_freq = 1.0 / (ROPE_BASE ** jnp.linspace(0.0, 1.0, half, dtype=F32))
    ang = jnp.arange(seq, dtype=F32)[:, None] * inv_freq[None, :]
    cos = jnp.cos(ang)
    sin = jnp.sin(ang)
    cosf = jnp.concatenate([cos, cos], axis=1)
    sinf = jnp.concatenate([-sin, sin], axis=1)

    log_gamma = jnp.log1p(-jnp.power(2.0, -5.0 - jnp.arange(N_HEADS, dtype=F32)))
    t = jnp.arange(PAIR)
    pos = (t % LA_CHUNK).astype(F32)
    mask = ((t[:, None] // LA_CHUNK) == (t[None, :] // LA_CHUNK)) & (t[None, :] <= t[:, None])
    dmat = jnp.exp(jnp.where(mask[None], (pos[:, None] - pos[None, :])[None] * log_gamma[:, None, None], -jnp.inf))
    full = (N_HEADS, PAIR, HEAD_DIM)
    q_scale = jnp.broadcast_to(jnp.exp((pos[None, :] + 1.0) * log_gamma[:, None])[:, :, None], full)
    k_scale = jnp.broadcast_to(jnp.exp((LA_CHUNK - 1.0 - pos[None, :]) * log_gamma[:, None])[:, :, None], full)
    chunk_decay = jnp.broadcast_to(jnp.exp(LA_CHUNK * log_gamma)[:, None, None], full)
    return cosf, sinf, dmat, q_scale, k_scale, chunk_decay


def kernel(x, norm_w, la_w_in, la_conv_w, la_a_log, la_dt_bias, la_out_norm_w, la_w_out, sg_w_in, sg_ln_w,
           sg_ln_b, sg_w_s, sg_b_s, sg_w_out, ffn_w_up, ffn_w_down):
    batch, seq, d = x.shape
    t = batch * seq
    h = x.reshape(t, d)
    gate0 = 4 * QK_WIDTH

    w_in = la_w_in[0]
    w_main = jnp.concatenate([w_in[:, :gate0], w_in[:, gate0 + 2 * N_HEADS:]], axis=1).astype(BF16)
    w_gate = jnp.pad(w_in[:, gate0:gate0 + 2 * N_HEADS], ((0, 0), (0, LANES - 2 * N_HEADS))).astype(BF16)
    proj, gates = _inproj_la(h, norm_w[0, 0][None, :], w_main, w_gate)

    cw = la_conv_w[0].reshape(3, QK_WIDTH, CONV_WIDTH).transpose(0, 2, 1)
    gp = jnp.zeros((8, LANES), F32)
    gp = gp.at[0, N_HEADS:2 * N_HEADS].set(la_a_log[0]).at[1, N_HEADS:2 * N_HEADS].set(la_dt_bias[0])
    o_a = _gdn(proj, gates, cw, gp, la_out_norm_w[0][None, :], batch=batch, seq=seq)

    o_b = _ret(proj, *_retention_tables(seq), batch=batch, seq=seq)

    w_out = la_w_out[0].astype(BF16)
    h = _out_rms_res([o_a, o_b], [w_out[:QK_WIDTH], w_out[QK_WIDTH:]], h, norm_w[0, 1][None, :], tm=512, vmem_mib=48)
    h = _ffn(h, norm_w[0, 2][None, :], ffn_w_up[0].astype(BF16), ffn_w_down[0].astype(BF16), norm_w[0, 3][None, :])

    proj1 = _inproj_gelu(h, norm_w[1, 0][None, :], sg_w_in[0].astype(BF16))
    gated = _sg_gate(proj1, sg_ln_w[0][None, :], sg_ln_b[0][None, :], sg_w_s[0], sg_b_s[0].T)
    h = _out_rms_res([gated], [sg_w_out[0].astype(BF16)], h, norm_w[1, 1][None, :], tm=256, vmem_mib=48)
    h = _ffn(h, norm_w[1, 2][None, :], ffn_w_up[1].astype(BF16), ffn_w_down[1].astype(BF16), norm_w[1, 3][None, :])
    return h.reshape(batch, seq, d)
```

```python
import functools
import math

import jax
import jax.numpy as jnp
from jax import lax
from jax.experimental import pallas as pl
from jax.experimental.pallas import tpu as pltpu

F32 = jnp.float32
BF16 = jnp.bfloat16

D_MODEL = 2048
N_HEADS = 8
HEAD_DIM = 128
QK_WIDTH = N_HEADS * HEAD_DIM
LA_CHUNK = 64
PAIR = 2 * LA_CHUNK
CONV_WIDTH = 4
HALO = 8
ROPE_BASE = 10000.0
SG_CHUNK = 128
SG_GROUPS = 8
SG_WIDTH = 2 * D_MODEL
SG_GROUP_DIM = SG_WIDTH // SG_GROUPS
FFN_HIDDEN = 4 * D_MODEL
EPS = 1e-6
LANES = 128
ROW_LOOP_UNROLL = 8
MIB = 1024 * 1024

NT_DIMS = (((1,), (1,)), ((), ()))


def _params(n_grid_axes, vmem_mib):
    return pltpu.CompilerParams(dimension_semantics=("arbitrary",) * n_grid_axes,
                                vmem_limit_bytes=vmem_mib * MIB)


def _dot(a, b):
    return jnp.dot(a, b, preferred_element_type=F32)


def _dot_nt(a, b):
    return lax.dot_general(a, b, NT_DIMS, preferred_element_type=F32)


def _silu(x):
    return x * jax.nn.sigmoid(x)


def _softplus(x):
    return jnp.maximum(x, 0.0) + jnp.log1p(jnp.exp(-jnp.abs(x)))


def _gelu_tanh(x):
    c = math.sqrt(2.0 / math.pi)
    return x * (0.5 * (1.0 + jnp.tanh(c * (x + 0.044715 * (x * x * x)))))


def _rmsnorm_rows_to(src_ref, nw_ref, dst_ref, rows):
    nw = nw_ref[...]
    step = 16

    def body(r, carry):
        sl = pl.ds(pl.multiple_of(r * step, step), step)
        x = src_ref[sl, :]
        ms = jnp.mean(x * x, axis=-1, keepdims=True)
        dst_ref[sl, :] = (x * lax.rsqrt(ms + EPS) * nw).astype(dst_ref.dtype)
        return carry

    lax.fori_loop(0, rows // step, body, 0, unroll=ROW_LOOP_UNROLL)


def _rms_residual_rows(acc_ref, h_ref, nw_ref, o_ref, rows):
    nw = nw_ref[...]
    step = 16

    def body(r, carry):
        sl = pl.ds(pl.multiple_of(r * step, step), step)
        y = acc_ref[sl, :]
        ms = jnp.mean(y * y, axis=-1, keepdims=True)
        o_ref[sl, :] = h_ref[sl, :] + y * lax.rsqrt(ms + EPS) * nw
        return carry

    lax.fori_loop(0, rows // step, body, 0, unroll=ROW_LOOP_UNROLL)


def _inproj_la_kernel(x_ref, nw_ref, wa_ref, wb_ref, wg_ref, o_ref, g_ref, xn_ref, *, tm, na):
    j = pl.program_id(1)

    @pl.when(j == 0)
    def _():
        _rmsnorm_rows_to(x_ref, nw_ref, xn_ref, tm)
        g_ref[...] = _dot(xn_ref[...], wg_ref[...])

    @pl.when(j < na)
    def _():
        o_ref[...] = _dot(xn_ref[...], wa_ref[...])

    @pl.when(j >= na)
    def _():
        o_ref[...] = _dot(xn_ref[...], wb_ref[...])


def _inproj_la(x, nw, wa, wb, wg, *, tm=1024, tn=1024):
    t, k = x.shape
    na = wa.shape[1] // tn
    n = wa.shape[1] + wb.shape[1]
    return pl.pallas_call(
        functools.partial(_inproj_la_kernel, tm=tm, na=na),
        grid=(t // tm, n // tn),
        in_specs=[
            pl.BlockSpec((tm, k), lambda i, j: (i, 0)),
            pl.BlockSpec((1, k), lambda i, j: (0, 0)),
            pl.BlockSpec((k, tn), lambda i, j: (0, jnp.minimum(j, na - 1))),
            pl.BlockSpec((k, tn), lambda i, j: (0, jnp.maximum(j - na, 0))),
            pl.BlockSpec((k, LANES), lambda i, j: (0, 0)),
        ],
        out_specs=[
            pl.BlockSpec((tm, tn), lambda i, j: (i, j)),
            pl.BlockSpec((tm, LANES), lambda i, j: (i, 0)),
        ],
        out_shape=[jax.ShapeDtypeStruct((t, n), F32), jax.ShapeDtypeStruct((t, LANES), F32)],
        scratch_shapes=[pltpu.VMEM((tm, k), BF16)],
        compiler_params=_params(2, 56),
        name="inproj_la",
    )(x, nw, wa, wb, wg)


def _inproj_gelu_kernel(x_ref, nw_ref, w_ref, o_ref, xn_ref, *, tm):
    @pl.when(pl.program_id(1) == 0)
    def _():
        _rmsnorm_rows_to(x_ref, nw_ref, xn_ref, tm)

    o_ref[...] = _gelu_tanh(_dot(xn_ref[...], w_ref[...])).astype(o_ref.dtype)


def _inproj_gelu(x, nw, w, *, tm=1024, tn=1024):
    t, k = x.shape
    n = w.shape[1]
    return pl.pallas_call(
        functools.partial(_inproj_gelu_kernel, tm=tm),
        grid=(t // tm, n // tn),
        in_specs=[
            pl.BlockSpec((tm, k), lambda i, j: (i, 0)),
            pl.BlockSpec((1, k), lambda i, j: (0, 0)),
            pl.BlockSpec((k, tn), lambda i, j: (0, j)),
        ],
        out_specs=pl.BlockSpec((tm, tn), lambda i, j: (i, j)),
        out_shape=jax.ShapeDtypeStruct((t, n), BF16),
        scratch_shapes=[pltpu.VMEM((tm, k), BF16)],
        compiler_params=_params(2, 48),
        name="inproj_gelu",
    )(x, nw, w)


def _out_rms_res_kernel(*refs, n_pairs, tm):
    a_refs = refs[:n_pairs]
    w_refs = refs[n_pairs:2 * n_pairs]
    h_ref, nw_ref, o_ref, acc_ref = refs[2 * n_pairs:]
    y = _dot(a_refs[0][...], w_refs[0][...])
    for a_ref, w_ref in zip(a_refs[1:], w_refs[1:]):
        y = y + _dot(a_ref[...], w_ref[...])
    acc_ref[...] = y
    _rms_residual_rows(acc_ref, h_ref, nw_ref, o_ref, tm)


def _out_rms_res(a_list, w_list, h, nw, *, tm, vmem_mib):
    t, n = h.shape
    n_pairs = len(a_list)
    in_specs = [pl.BlockSpec((tm, a.shape[1]), lambda i: (i, 0)) for a in a_list]
    k0 = 0
    for a, w in zip(a_list, w_list):
        ka = a.shape[1]
        row_block = k0 // ka if w.shape[0] != ka else 0
        in_specs.append(pl.BlockSpec((ka, n), lambda i, rb=row_block: (rb, 0)))
        k0 += ka
    in_specs += [pl.BlockSpec((tm, n), lambda i: (i, 0)), pl.BlockSpec((1, n), lambda i: (0, 0))]
    return pl.pallas_call(
        functools.partial(_out_rms_res_kernel, n_pairs=n_pairs, tm=tm),
        grid=(t // tm,),
        in_specs=in_specs,
        out_specs=pl.BlockSpec((tm, n), lambda i: (i, 0)),
        out_shape=jax.ShapeDtypeStruct((t, n), F32),
        scratch_shapes=[pltpu.VMEM((tm, n), F32)],
        compiler_params=_params(1, vmem_mib),
        name="out_rms_res",
    )(*a_list, *w_list, h, nw)


def _ffn_kernel(h_ref, nw_in_ref, wup_ref, wdn_ref, nw_out_ref, o_ref, xn_ref, acc_ref, *, tm):
    j = pl.program_id(1)

    @pl.when(j == 0)
    def _():
        _rmsnorm_rows_to(h_ref, nw_in_ref, xn_ref, tm)
        acc_ref[...] = jnp.zeros_like(acc_ref)

    u = _dot(xn_ref[...], wup_ref[...])
    u = jnp.square(jnp.maximum(u, 0.0)).astype(BF16)
    acc_ref[...] += _dot(u, wdn_ref[...])

    @pl.when(j == pl.num_programs(1) - 1)
    def _():
        _rms_residual_rows(acc_ref, h_ref, nw_out_ref, o_ref, tm)


def _ffn(h, nw_in, wup, wdn, nw_out, layer, *, tm=512, th=1024):
    t, d = h.shape
    hidden = wup.shape[2]
    return pl.pallas_call(
        functools.partial(_ffn_kernel, tm=tm),
        grid=(t // tm, hidden // th),
        in_specs=[
            pl.BlockSpec((tm, d), lambda i, j: (i, 0)),
            pl.BlockSpec((1, d), lambda i, j: (0, 0)),
            pl.BlockSpec((None, d, th), lambda i, j: (layer, 0, j)),
            pl.BlockSpec((None, th, d), lambda i, j: (layer, j, 0)),
            pl.BlockSpec((1, d), lambda i, j: (0, 0)),
        ],
        out_specs=pl.BlockSpec((tm, d), lambda i, j: (i, 0)),
        out_shape=jax.ShapeDtypeStruct((t, d), F32),
        scratch_shapes=[pltpu.VMEM((tm, d), BF16), pltpu.VMEM((tm, d), F32)],
        compiler_params=_params(2, 48),
        name="ffn",
    )(h, nw_in, wup, wdn, nw_out)


def _pair_masks():
    row = lax.broadcasted_iota(jnp.int32, (PAIR, PAIR), 0)
    col = lax.broadcasted_iota(jnp.int32, (PAIR, PAIR), 1)
    same_chunk = (row >> 6) == (col >> 6)
    return row, col, same_chunk


def _chunk_halves(parts, a):
    zeros = jnp.zeros_like(parts)
    return jnp.concatenate([parts, zeros] if a == 0 else [zeros, parts], axis=0)


def _gdn_kernel(q_ref, k_ref, v_ref, z_ref, gate_ref, cw_ref, gp_ref, onw_ref, o_ref,
                xpad_ref, s_ref, *, tl):
    nh = N_HEADS
    heads = range(nh)

    @pl.when(pl.program_id(1) == 0)
    def _():
        xpad_ref[:, 0:HALO, :] = jnp.zeros((3, HALO, QK_WIDTH), F32)
        s_ref[...] = jnp.zeros_like(s_ref)

    xpad_ref[0, HALO:HALO + tl, :] = q_ref[...]
    xpad_ref[1, HALO:HALO + tl, :] = k_ref[...]
    xpad_ref[2, HALO:HALO + tl, :] = v_ref[...]

    row, col, same_chunk = _pair_masks()
    causal = same_chunk & (col <= row)
    strict = same_chunk & (col < row)
    eye = (row == col).astype(F32)
    row_in_chunk = row & (LA_CHUNK - 1)
    lane8 = lax.broadcasted_iota(jnp.int32, (nh, PAIR), 1)

    a_coef = -jnp.exp(gp_ref[0:1, :])
    dt_bias = gp_ref[1:2, :]
    onw = onw_ref[...]
    s = [s_ref[h] for h in heads]

    for p in range(tl // PAIR):
        r0 = p * PAIR
        rows = slice(r0, r0 + PAIR)

        pre = gate_ref[rows, :]
        beta_all = jax.nn.sigmoid(pre)
        g = a_coef * _softplus(pre + dt_bias)
        for sh in (1, 2, 4, 8, 16, 32):
            g = g + jnp.where(row_in_chunk >= sh, pltpu.roll(g, sh, axis=0), 0.0)
        gt8 = g.T[nh:2 * nh, :]
        bt8 = beta_all.T[0:nh, :]
        g_last = [jnp.sum(jnp.where(lane8 == (a + 1) * LA_CHUNK - 1, gt8, 0.0), axis=1, keepdims=True)
                  for a in range(2)]
        k_tail8 = jnp.exp(jnp.where(lane8 < LA_CHUNK, g_last[0], g_last[1]) - gt8)
        be8 = bt8 * jnp.exp(gt8)
        chunk_decay = [jnp.exp(gl) for gl in g_last]

        def conv(ci, cs):
            w = cw_ref[ci, :, cs]
            acc = xpad_ref[ci, HALO + r0:HALO + r0 + PAIR, cs] * w[CONV_WIDTH - 1:CONV_WIDTH, :]
            for j in range(1, CONV_WIDTH):
                acc = acc + xpad_ref[ci, HALO + r0 - j:HALO + r0 - j + PAIR, cs] * w[CONV_WIDTH - 1 - j:CONV_WIDTH - j, :]
            return _silu(acc)

        q_l, k_l, v_l, kbf_l, decay_l, m1_l, lhs_l = [], [], [], [], [], [], []
        for h in heads:
            cs = slice(h * HEAD_DIM, (h + 1) * HEAD_DIM)
            q = conv(0, cs)
            k = conv(1, cs)
            v = conv(2, cs)
            q = q * lax.rsqrt(jnp.sum(q * q, axis=-1, keepdims=True) + 1e-6) * (HEAD_DIM ** -0.5)
            k = k * lax.rsqrt(jnp.sum(k * k, axis=-1, keepdims=True) + 1e-6)
            gc_col = jnp.sum(jnp.where(col == nh + h, g, 0.0), axis=1, keepdims=True)
            beta_col = jnp.sum(jnp.where(col == h, beta_all, 0.0), axis=1, keepdims=True)
            decay = jnp.exp(jnp.where(causal, gc_col - gt8[h:h + 1, :], -jnp.inf))
            q_l.append(q * jnp.exp(gc_col))
            k_l.append(k)
            v_l.append(v.astype(BF16))
            kbf = k.astype(BF16)
            kbf_l.append(kbf)
            decay_l.append(decay)
            m1_l.append(beta_col * decay)
            lhs_l.append(jnp.concatenate([q.astype(BF16), kbf], axis=0))

        r_l = [_dot_nt(lhs_l[h], kbf_l[h]) for h in heads]
        qk_l = [(r_l[h][0:PAIR] * decay_l[h]).astype(BF16) for h in heads]
        a_l = [jnp.where(strict, r_l[h][PAIR:2 * PAIR] * m1_l[h], 0.0) for h in heads]
        x_l = [eye - a_l[h] for h in heads]
        pb_l = [a_l[h].astype(BF16) for h in heads]
        for _ in range(5):
            pb_l = [_dot(pb_l[h], pb_l[h]).astype(BF16) for h in heads]
            x_l = [x_l[h] + _dot(x_l[h].astype(BF16), pb_l[h]) for h in heads]

        u_l = [_dot((x_l[h] * bt8[h:h + 1, :]).astype(BF16), v_l[h]) for h in heads]
        w_l = [_dot((x_l[h] * be8[h:h + 1, :]).astype(BF16), kbf_l[h]) for h in heads]
        kt_l = [(k_l[h].T * k_tail8[h:h + 1, :]).astype(BF16) for h in heads]

        v_new_l = [[], []]
        qs_l = [[], []]
        for a in range(2):
            ra = slice(a * LA_CHUNK, (a + 1) * LA_CHUNK)
            r2_l = [_dot(jnp.concatenate([w_l[h][ra], q_l[h][ra]], axis=0).astype(BF16), s[h].astype(BF16))
                    for h in heads]
            for h in heads:
                v_new = u_l[h][ra] - r2_l[h][0:LA_CHUNK]
                v_new_l[a].append(v_new)
                qs_l[a].append(r2_l[h][LA_CHUNK:2 * LA_CHUNK])
            upd_l = [_dot(kt_l[h], _chunk_halves(v_new_l[a][h], a).astype(BF16)) for h in heads]
            s = [s[h] * chunk_decay[a][h:h + 1, :] + upd_l[h] for h in heads]

        o_l = [jnp.concatenate([qs_l[0][h], qs_l[1][h]], axis=0)
               + _dot(qk_l[h], jnp.concatenate([v_new_l[0][h], v_new_l[1][h]], axis=0).astype(BF16))
               for h in heads]
        for h in heads:
            cs = slice(h * HEAD_DIM, (h + 1) * HEAD_DIM)
            o = o_l[h]
            o = o * lax.rsqrt(jnp.mean(o * o, axis=-1, keepdims=True) + EPS) * onw * _silu(z_ref[rows, cs])
            o_ref[rows, cs] = o.astype(o_ref.dtype)

    for h in heads:
        s_ref[h] = s[h]
    xpad_ref[:, 0:HALO, :] = xpad_ref[:, tl:tl + HALO, :]


def _gdn(proj, gates, cw, gp, onw, *, batch, seq, tl=128):
    nblk = seq // tl

    def seg(s_idx):
        return pl.BlockSpec((tl, QK_WIDTH), lambda b, i: (b * nblk + i, s_idx))

    return pl.pallas_call(
        functools.partial(_gdn_kernel, tl=tl),
        grid=(batch, nblk),
        in_specs=[
            seg(0), seg(1), seg(2), seg(3),
            pl.BlockSpec((tl, LANES), lambda b, i: (b * nblk + i, 0)),
            pl.BlockSpec((3, CONV_WIDTH, QK_WIDTH), lambda b, i: (0, 0, 0)),
            pl.BlockSpec((8, LANES), lambda b, i: (0, 0)),
            pl.BlockSpec((1, HEAD_DIM), lambda b, i: (0, 0)),
        ],
        out_specs=pl.BlockSpec((tl, QK_WIDTH), lambda b, i: (b * nblk + i, 0)),
        out_shape=jax.ShapeDtypeStruct((batch * seq, QK_WIDTH), BF16),
        scratch_shapes=[pltpu.VMEM((3, HALO + tl, QK_WIDTH), F32), pltpu.VMEM((N_HEADS, HEAD_DIM, HEAD_DIM), F32)],
        compiler_params=_params(2, 32),
        name="gdn",
    )(proj, proj, proj, proj, gates, cw, gp, onw)


def _ret_kernel(q_ref, k_ref, v_ref, g_ref, cos_ref, sin_ref, dmat_ref, qs_ref, ks_ref, cd_ref, o_ref, s_ref, *, tl):
    heads = range(N_HEADS)

    @pl.when(pl.program_id(1) == 0)
    def _():
        s_ref[...] = jnp.zeros_like(s_ref)

    s = [s_ref[h] for h in heads]

    for p in range(tl // PAIR):
        rows = slice(p * PAIR, (p + 1) * PAIR)
        cosf = cos_ref[rows, :]
        sinf = sin_ref[rows, :]

        q_l, kb_l, kdt_l, v_l = [], [], [], []
        for h in heads:
            cs = slice(h * HEAD_DIM, (h + 1) * HEAD_DIM)
            q = q_ref[rows, cs]
            k = k_ref[rows, cs]
            q = q * cosf + pltpu.roll(q, HEAD_DIM // 2, axis=1) * sinf
            k = (k * cosf + pltpu.roll(k, HEAD_DIM // 2, axis=1) * sinf) * (HEAD_DIM ** -0.5)
            q_l.append(q)
            kb_l.append(k.astype(BF16))
            kdt_l.append((k * ks_ref[h]).T.astype(BF16))
            v_l.append(v_ref[rows, cs])

        qk_l = [(_dot_nt(q_l[h].astype(BF16), kb_l[h]) * dmat_ref[h]).astype(BF16) for h in heads]
        inner_l = [_dot(qk_l[h], v_l[h].astype(BF16)) for h in heads]
        qd_l = [(q_l[h] * qs_ref[h]).astype(BF16) for h in heads]

        cross_l = [[], []]
        for a in range(2):
            ra = slice(a * LA_CHUNK, (a + 1) * LA_CHUNK)
            cross_l[a] = [_dot(qd_l[h][ra], s[h].astype(BF16)) for h in heads]
            upd_l = [_dot(kdt_l[h], _chunk_halves(v_l[h][ra], a).astype(BF16)) for h in heads]
            s = [s[h] * cd_ref[h] + upd_l[h] for h in heads]

        for h in heads:
            cs = slice(h * HEAD_DIM, (h + 1) * HEAD_DIM)
            o = inner_l[h] + jnp.concatenate([cross_l[0][h], cross_l[1][h]], axis=0)
            o = o * lax.rsqrt(jnp.mean(o * o, axis=-1, keepdims=True) + EPS)
            o_ref[rows, cs] = (_silu(g_ref[rows, cs]) * o).astype(o_ref.dtype)

    for h in heads:
        s_ref[h] = s[h]


def _ret(proj, cosf, sinf, dmat, q_scale, k_scale, chunk_decay, *, batch, seq, tl=128):
    nblk = seq // tl

    def seg(s_idx):
        return pl.BlockSpec((tl, QK_WIDTH), lambda b, i: (b * nblk + i, s_idx))

    def table():
        return pl.BlockSpec((N_HEADS, PAIR, HEAD_DIM), lambda b, i: (0, 0, 0))

    return pl.pallas_call(
        functools.partial(_ret_kernel, tl=tl),
        grid=(batch, nblk),
        in_specs=[
            seg(4), seg(5), seg(6), seg(7),
            pl.BlockSpec((tl, HEAD_DIM), lambda b, i: (i, 0)),
            pl.BlockSpec((tl, HEAD_DIM), lambda b, i: (i, 0)),
            table(), table(), table(), table(),
        ],
        out_specs=pl.BlockSpec((tl, QK_WIDTH), lambda b, i: (b * nblk + i, 0)),
        out_shape=jax.ShapeDtypeStruct((batch * seq, QK_WIDTH), BF16),
        scratch_shapes=[pltpu.VMEM((N_HEADS, HEAD_DIM, HEAD_DIM), F32)],
        compiler_params=_params(2, 32),
        name="retention",
    )(proj, proj, proj, proj, cosf, sinf, dmat, q_scale, k_scale, chunk_decay)


def _sg_gate_kernel(u_ref, v_ref, lnw_ref, lnb_ref, ws_ref, bs_ref, o_ref, vn_ref, *, tm):
    lnw = lnw_ref[...]
    lnb = lnb_ref[...]
    step = 16

    def body(r, carry):
        sl = pl.ds(pl.multiple_of(r * step, step), step)
        x = v_ref[sl, :].astype(F32)
        mu = jnp.mean(x, axis=-1, keepdims=True)
        xc = x - mu
        var = jnp.mean(xc * xc, axis=-1, keepdims=True)
        vn_ref[sl, :] = (xc * lax.rsqrt(var + EPS) * lnw + lnb).astype(vn_ref.dtype)
        return carry

    lax.fori_loop(0, tm // step, body, 0, unroll=ROW_LOOP_UNROLL)

    row = lax.broadcasted_iota(jnp.int32, (SG_CHUNK, SG_CHUNK), 0)
    col = lax.broadcasted_iota(jnp.int32, (SG_CHUNK, SG_CHUNK), 1)
    for g in range(SG_GROUPS):
        wg = jnp.where(col <= row, ws_ref[g], 0.0).astype(BF16)
        bias = bs_ref[:, g:g + 1]
        cols = slice(g * SG_GROUP_DIM, (g + 1) * SG_GROUP_DIM)
        for c in range(tm // SG_CHUNK):
            rows = slice(c * SG_CHUNK, (c + 1) * SG_CHUNK)
            sgate = _dot(wg, vn_ref[rows, cols]) + bias
            o_ref[rows, cols] = (u_ref[rows, cols].astype(F32) * sgate).astype(o_ref.dtype)


def _sg_gate(proj, lnw, lnb, ws, bs_t, *, tm=512):
    t = proj.shape[0]
    return pl.pallas_call(
        functools.partial(_sg_gate_kernel, tm=tm),
        grid=(t // tm,),
        in_specs=[
            pl.BlockSpec((tm, SG_WIDTH), lambda i: (i, 0)),
            pl.BlockSpec((tm, SG_WIDTH), lambda i: (i, 1)),
            pl.BlockSpec((1, SG_WIDTH), lambda i: (0, 0)),
            pl.BlockSpec((1, SG_WIDTH), lambda i: (0, 0)),
            pl.BlockSpec((SG_GROUPS, SG_CHUNK, SG_CHUNK), lambda i: (0, 0, 0)),
            pl.BlockSpec((SG_CHUNK, SG_GROUPS), lambda i: (0, 0)),
        ],
        out_specs=pl.BlockSpec((tm, SG_WIDTH), lambda i: (i, 0)),
        out_shape=jax.ShapeDtypeStruct((t, SG_WIDTH), BF16),
        scratch_shapes=[pltpu.VMEM((tm, SG_WIDTH), BF16)],
        compiler_params=_params(1, 40),
        name="sg_gate",
    )(proj, proj, lnw, lnb, ws, bs_t)


def _retention_tables(seq):
    half = HEAD_DIM // 2
    inv_freq = 1.0 / (ROPE_BASE ** jnp.linspace(0.0, 1.0, half, dtype=F32))
    ang = jnp.arange(seq, dtype=F32)[:, None] * inv_freq[None, :]
    cos = jnp.cos(ang)
    sin = jnp.sin(ang)
    cosf = jnp.concatenate([cos, cos], axis=1)
    sinf = jnp.concatenate([-sin, sin], axis=1)

    log_gamma = jnp.log1p(-jnp.power(2.0, -5.0 - jnp.arange(N_HEADS, dtype=F32)))
    t = jnp.arange(PAIR)
    pos = (t % LA_CHUNK).astype(F32)
    mask = ((t[:, None] // LA_CHUNK) == (t[None, :] // LA_CHUNK)) & (t[None, :] <= t[:, None])
    dmat = jnp.exp(jnp.where(mask[None], (pos[:, None] - pos[None, :])[None] * log_gamma[:, None, None], -jnp.inf))
    full = (N_HEADS, PAIR, HEAD_DIM)
    q_scale = jnp.broadcast_to(jnp.exp((pos[None, :] + 1.0) * log_gamma[:, None])[:, :, None], full)
    k_scale = jnp.broadcast_to(jnp.exp((LA_CHUNK - 1.0 - pos[None, :]) * log_gamma[:, None])[:, :, None], full)
    chunk_decay = jnp.broadcast_to(jnp.exp(LA_CHUNK * log_gamma)[:, None, None], full)
    return cosf, sinf, dmat, q_scale, k_scale, chunk_decay


def kernel(x, norm_w, la_w_in, la_conv_w, la_a_log, la_dt_bias, la_out_norm_w, la_w_out, sg_w_in, sg_ln_w,
           sg_ln_b, sg_w_s, sg_b_s, sg_w_out, ffn_w_up, ffn_w_down):
    batch, seq, d = x.shape
    t = batch * seq
    h = x.reshape(t, d)
    gate0 = 4 * QK_WIDTH

    w_in = la_w_in[0]
    w_a = w_in[:, :gate0].astype(BF16)
    w_b = w_in[:, gate0 + 2 * N_HEADS:].astype(BF16)
    w_gate = jnp.pad(w_in[:, gate0:gate0 + 2 * N_HEADS], ((0, 0), (0, LANES - 2 * N_HEADS))).astype(BF16)
    proj, gates = _inproj_la(h, norm_w[0, 0][None, :], w_a, w_b, w_gate)

    cw = la_conv_w[0].reshape(3, QK_WIDTH, CONV_WIDTH).transpose(0, 2, 1)
    gp = jnp.zeros((8, LANES), F32)
    gp = gp.at[0, N_HEADS:2 * N_HEADS].set(la_a_log[0]).at[1, N_HEADS:2 * N_HEADS].set(la_dt_bias[0])
    o_a = _gdn(proj, gates, cw, gp, la_out_norm_w[0][None, :], batch=batch, seq=seq)

    o_b = _ret(proj, *_retention_tables(seq), batch=batch, seq=seq)

    w_out = la_w_out[0].astype(BF16)
    h = _out_rms_res([o_a, o_b], [w_out, w_out], h, norm_w[0, 1][None, :], tm=512, vmem_mib=48)
    w_up = ffn_w_up.astype(BF16)
    w_down = ffn_w_down.astype(BF16)
    h = _ffn(h, norm_w[0, 2][None, :], w_up, w_down, norm_w[0, 3][None, :], 0)

    proj1 = _inproj_gelu(h, norm_w[1, 0][None, :], sg_w_in[0].astype(BF16))
    gated = _sg_gate(proj1, sg_ln_w[0][None, :], sg_ln_b[0][None, :], sg_w_s[0], sg_b_s[0].T)
    h = _out_rms_res([gated], [sg_w_out[0].astype(BF16)], h, norm_w[1, 1][None, :], tm=256, vmem_mib=56)
    h = _ffn(h, norm_w[1, 2][None, :], w_up, w_down, norm_w[1, 3][None, :], 1)
    return h.reshape(batch, seq, d)
```

```python
import functools
import math

import jax
import jax.numpy as jnp
from jax import lax
from jax.experimental import pallas as pl
from jax.experimental.pallas import tpu as pltpu

F32 = jnp.float32
BF16 = jnp.bfloat16

D_MODEL = 2048
N_HEADS = 8
HEAD_DIM = 128
QK_WIDTH = N_HEADS * HEAD_DIM
LA_CHUNK = 64
PAIR = 2 * LA_CHUNK
CONV_WIDTH = 4
HALO = 8
ROPE_BASE = 10000.0
SG_CHUNK = 128
SG_GROUPS = 8
SG_WIDTH = 2 * D_MODEL
SG_GROUP_DIM = SG_WIDTH // SG_GROUPS
FFN_HIDDEN = 4 * D_MODEL
EPS = 1e-6
LANES = 128
ROW_LOOP_UNROLL = 8
MIB = 1024 * 1024

NT_DIMS = (((1,), (1,)), ((), ()))


def _params(n_grid_axes, vmem_mib):
    return pltpu.CompilerParams(dimension_semantics=("arbitrary",) * n_grid_axes,
                                vmem_limit_bytes=vmem_mib * MIB)


def _dot(a, b):
    return jnp.dot(a, b, preferred_element_type=F32)


def _dot_nt(a, b):
    return lax.dot_general(a, b, NT_DIMS, preferred_element_type=F32)


def _silu(x):
    return x * jax.nn.sigmoid(x)


def _softplus(x):
    return jnp.maximum(x, 0.0) + jnp.log1p(jnp.exp(-jnp.abs(x)))


def _gelu_tanh(x):
    c = math.sqrt(2.0 / math.pi)
    return x * (0.5 * (1.0 + jnp.tanh(c * (x + 0.044715 * (x * x * x)))))


def _rmsnorm_rows_to(src_ref, nw_ref, dst_ref, rows):
    nw = nw_ref[...]
    step = 16

    def body(r, carry):
        sl = pl.ds(pl.multiple_of(r * step, step), step)
        x = src_ref[sl, :]
        ms = jnp.mean(x * x, axis=-1, keepdims=True)
        dst_ref[sl, :] = (x * lax.rsqrt(ms + EPS) * nw).astype(dst_ref.dtype)
        return carry

    lax.fori_loop(0, rows // step, body, 0, unroll=ROW_LOOP_UNROLL)


def _rms_residual_rows(acc_ref, h_ref, nw_ref, o_ref, rows):
    nw = nw_ref[...]
    step = 16

    def body(r, carry):
        sl = pl.ds(pl.multiple_of(r * step, step), step)
        y = acc_ref[sl, :]
        ms = jnp.mean(y * y, axis=-1, keepdims=True)
        o_ref[sl, :] = h_ref[sl, :] + y * lax.rsqrt(ms + EPS) * nw
        return carry

    lax.fori_loop(0, rows // step, body, 0, unroll=ROW_LOOP_UNROLL)


def _inproj_la_kernel(x_ref, nw_ref, wa_ref, wb_ref, wg_ref, o_ref, g_ref, xn_ref, *, tm, na):
    j = pl.program_id(1)

    @pl.when(j == 0)
    def _():
        _rmsnorm_rows_to(x_ref, nw_ref, xn_ref, tm)
        g_ref[...] = _dot(xn_ref[...], wg_ref[...])

    @pl.when(j < na)
    def _():
        o_ref[...] = _dot(xn_ref[...], wa_ref[...]).astype(o_ref.dtype)

    @pl.when(j >= na)
    def _():
        o_ref[...] = _dot(xn_ref[...], wb_ref[...]).astype(o_ref.dtype)


def _inproj_la(x, nw, wa, wb, wg, *, tm=1024, tn=1024):
    t, k = x.shape
    na = wa.shape[1] // tn
    n = wa.shape[1] + wb.shape[1]
    return pl.pallas_call(
        functools.partial(_inproj_la_kernel, tm=tm, na=na),
        grid=(t // tm, n // tn),
        in_specs=[
            pl.BlockSpec((tm, k), lambda i, j: (i, 0)),
            pl.BlockSpec((1, k), lambda i, j: (0, 0)),
            pl.BlockSpec((k, tn), lambda i, j: (0, jnp.minimum(j, na - 1))),
            pl.BlockSpec((k, tn), lambda i, j: (0, jnp.maximum(j - na, 0))),
            pl.BlockSpec((k, LANES), lambda i, j: (0, 0)),
        ],
        out_specs=[
            pl.BlockSpec((tm, tn), lambda i, j: (i, j)),
            pl.BlockSpec((tm, LANES), lambda i, j: (i, 0)),
        ],
        out_shape=[jax.ShapeDtypeStruct((t, n), BF16), jax.ShapeDtypeStruct((t, LANES), F32)],
        scratch_shapes=[pltpu.VMEM((tm, k), BF16)],
        compiler_params=_params(2, 56),
        name="inproj_la",
    )(x, nw, wa, wb, wg)


def _inproj_gelu_kernel(x_ref, nw_ref, w_ref, o_ref, xn_ref, *, tm):
    @pl.when(pl.program_id(1) == 0)
    def _():
        _rmsnorm_rows_to(x_ref, nw_ref, xn_ref, tm)

    o_ref[...] = _gelu_tanh(_dot(xn_ref[...], w_ref[...])).astype(o_ref.dtype)


def _inproj_gelu(x, nw, w, *, tm=1024, tn=1024):
    t, k = x.shape
    n = w.shape[1]
    return pl.pallas_call(
        functools.partial(_inproj_gelu_kernel, tm=tm),
        grid=(t // tm, n // tn),
        in_specs=[
            pl.BlockSpec((tm, k), lambda i, j: (i, 0)),
            pl.BlockSpec((1, k), lambda i, j: (0, 0)),
            pl.BlockSpec((k, tn), lambda i, j: (0, j)),
        ],
        out_specs=pl.BlockSpec((tm, tn), lambda i, j: (i, j)),
        out_shape=jax.ShapeDtypeStruct((t, n), BF16),
        scratch_shapes=[pltpu.VMEM((tm, k), BF16)],
        compiler_params=_params(2, 48),
        name="inproj_gelu",
    )(x, nw, w)


def _out_rms_res_kernel(*refs, n_pairs, tm):
    a_refs = refs[:n_pairs]
    w_refs = refs[n_pairs:2 * n_pairs]
    h_ref, nw_ref, o_ref, acc_ref = refs[2 * n_pairs:]
    y = _dot(a_refs[0][...], w_refs[0][...])
    for a_ref, w_ref in zip(a_refs[1:], w_refs[1:]):
        y = y + _dot(a_ref[...], w_ref[...])
    acc_ref[...] = y
    _rms_residual_rows(acc_ref, h_ref, nw_ref, o_ref, tm)


def _out_rms_res(a_list, w_list, h, nw, *, tm, vmem_mib):
    t, n = h.shape
    n_pairs = len(a_list)
    in_specs = [pl.BlockSpec((tm, a.shape[1]), lambda i: (i, 0)) for a in a_list]
    k0 = 0
    for a, w in zip(a_list, w_list):
        ka = a.shape[1]
        row_block = k0 // ka if w.shape[0] != ka else 0
        in_specs.append(pl.BlockSpec((ka, n), lambda i, rb=row_block: (rb, 0)))
        k0 += ka
    in_specs += [pl.BlockSpec((tm, n), lambda i: (i, 0)), pl.BlockSpec((1, n), lambda i: (0, 0))]
    return pl.pallas_call(
        functools.partial(_out_rms_res_kernel, n_pairs=n_pairs, tm=tm),
        grid=(t // tm,),
        in_specs=in_specs,
        out_specs=pl.BlockSpec((tm, n), lambda i: (i, 0)),
        out_shape=jax.ShapeDtypeStruct((t, n), F32),
        scratch_shapes=[pltpu.VMEM((tm, n), F32)],
        compiler_params=_params(1, vmem_mib),
        name="out_rms_res",
    )(*a_list, *w_list, h, nw)


def _ffn_kernel(h_ref, nw_in_ref, wup_ref, wdn_ref, nw_out_ref, o_ref, xn_ref, acc_ref, *, tm):
    j = pl.program_id(1)

    @pl.when(j == 0)
    def _():
        _rmsnorm_rows_to(h_ref, nw_in_ref, xn_ref, tm)
        acc_ref[...] = jnp.zeros_like(acc_ref)

    u = _dot(xn_ref[...], wup_ref[...])
    u = jnp.square(jnp.maximum(u, 0.0)).astype(BF16)
    acc_ref[...] += _dot(u, wdn_ref[...])

    @pl.when(j == pl.num_programs(1) - 1)
    def _():
        _rms_residual_rows(acc_ref, h_ref, nw_out_ref, o_ref, tm)


def _ffn(h, nw_in, wup, wdn, nw_out, layer, *, tm=512, th=1024):
    t, d = h.shape
    hidden = wup.shape[2]
    return pl.pallas_call(
        functools.partial(_ffn_kernel, tm=tm),
        grid=(t // tm, hidden // th),
        in_specs=[
            pl.BlockSpec((tm, d), lambda i, j: (i, 0)),
            pl.BlockSpec((1, d), lambda i, j: (0, 0)),
            pl.BlockSpec((None, d, th), lambda i, j: (layer, 0, j)),
            pl.BlockSpec((None, th, d), lambda i, j: (layer, j, 0)),
            pl.BlockSpec((1, d), lambda i, j: (0, 0)),
        ],
        out_specs=pl.BlockSpec((tm, d), lambda i, j: (i, 0)),
        out_shape=jax.ShapeDtypeStruct((t, d), F32),
        scratch_shapes=[pltpu.VMEM((tm, d), BF16), pltpu.VMEM((tm, d), F32)],
        compiler_params=_params(2, 48),
        name="ffn",
    )(h, nw_in, wup, wdn, nw_out)


def _cast_specs(arrays, n_steps, step_index):
    in_specs, out_specs, out_shapes = [], [], []
    for a in arrays:
        rows, cols = a.shape[0] // n_steps, a.shape[1]
        in_specs.append(pl.BlockSpec((rows, cols), lambda b, i: (step_index(b, i), 0)))
        out_specs.append(pl.BlockSpec((rows, cols), lambda b, i: (step_index(b, i), 0)))
        out_shapes.append(jax.ShapeDtypeStruct(a.shape, BF16))
    return in_specs, out_specs, out_shapes


def _cast_slabs(in_refs, out_refs):
    for in_ref, out_ref in zip(in_refs, out_refs):
        out_ref[...] = in_ref[...].astype(out_ref.dtype)


def _pair_masks():
    row = lax.broadcasted_iota(jnp.int32, (PAIR, PAIR), 0)
    col = lax.broadcasted_iota(jnp.int32, (PAIR, PAIR), 1)
    same_chunk = (row >> 6) == (col >> 6)
    return row, col, same_chunk


def _chunk_halves(parts, a):
    zeros = jnp.zeros_like(parts)
    return jnp.concatenate([parts, zeros] if a == 0 else [zeros, parts], axis=0)


def _gdn_kernel(*refs, tl, n_cast):
    (q_ref, k_ref, v_ref, z_ref, gate_ref, cw_ref, gp_ref, onw_ref), rest = refs[:8], refs[8:]
    cast_in, o_ref, cast_out, (xpad_ref, s_ref) = rest[:n_cast], rest[n_cast], rest[n_cast + 1:2 * n_cast + 1], rest[2 * n_cast + 1:]
    _cast_slabs(cast_in, cast_out)
    _gdn_body(q_ref, k_ref, v_ref, z_ref, gate_ref, cw_ref, gp_ref, onw_ref, o_ref, xpad_ref, s_ref, tl=tl)


def _gdn_body(q_ref, k_ref, v_ref, z_ref, gate_ref, cw_ref, gp_ref, onw_ref, o_ref,
              xpad_ref, s_ref, *, tl):
    nh = N_HEADS
    heads = range(nh)

    @pl.when(pl.program_id(1) == 0)
    def _():
        xpad_ref[:, 0:HALO, :] = jnp.zeros((3, HALO, QK_WIDTH), F32)
        s_ref[...] = jnp.zeros_like(s_ref)

    xpad_ref[0, HALO:HALO + tl, :] = q_ref[...].astype(F32)
    xpad_ref[1, HALO:HALO + tl, :] = k_ref[...].astype(F32)
    xpad_ref[2, HALO:HALO + tl, :] = v_ref[...].astype(F32)

    row, col, same_chunk = _pair_masks()
    causal = same_chunk & (col <= row)
    strict = same_chunk & (col < row)
    eye = (row == col).astype(F32)
    row_in_chunk = row & (LA_CHUNK - 1)
    lane8 = lax.broadcasted_iota(jnp.int32, (nh, PAIR), 1)

    a_coef = -jnp.exp(gp_ref[0:1, :])
    dt_bias = gp_ref[1:2, :]
    onw = onw_ref[...]
    s = [s_ref[h] for h in heads]

    for p in range(tl // PAIR):
        r0 = p * PAIR
        rows = slice(r0, r0 + PAIR)

        pre = gate_ref[rows, :]
        beta_all = jax.nn.sigmoid(pre)
        g = a_coef * _softplus(pre + dt_bias)
        for sh in (1, 2, 4, 8, 16, 32):
            g = g + jnp.where(row_in_chunk >= sh, pltpu.roll(g, sh, axis=0), 0.0)
        gt8 = g.T[nh:2 * nh, :]
        bt8 = beta_all.T[0:nh, :]
        g_last = [jnp.sum(jnp.where(lane8 == (a + 1) * LA_CHUNK - 1, gt8, 0.0), axis=1, keepdims=True)
                  for a in range(2)]
        k_tail8 = jnp.exp(jnp.where(lane8 < LA_CHUNK, g_last[0], g_last[1]) - gt8)
        be8 = bt8 * jnp.exp(gt8)
        chunk_decay = [jnp.exp(gl) for gl in g_last]

        def conv(ci, cs):
            w = cw_ref[ci, :, cs]
            acc = xpad_ref[ci, HALO + r0:HALO + r0 + PAIR, cs] * w[CONV_WIDTH - 1:CONV_WIDTH, :]
            for j in range(1, CONV_WIDTH):
                acc = acc + xpad_ref[ci, HALO + r0 - j:HALO + r0 - j + PAIR, cs] * w[CONV_WIDTH - 1 - j:CONV_WIDTH - j, :]
            return _silu(acc)

        q_l, k_l, v_l, kbf_l, decay_l, m1_l, lhs_l = [], [], [], [], [], [], []
        for h in heads:
            cs = slice(h * HEAD_DIM, (h + 1) * HEAD_DIM)
            q = conv(0, cs)
            k = conv(1, cs)
            v = conv(2, cs)
            q = q * lax.rsqrt(jnp.sum(q * q, axis=-1, keepdims=True) + 1e-6) * (HEAD_DIM ** -0.5)
            k = k * lax.rsqrt(jnp.sum(k * k, axis=-1, keepdims=True) + 1e-6)
            gc_col = jnp.sum(jnp.where(col == nh + h, g, 0.0), axis=1, keepdims=True)
            beta_col = jnp.sum(jnp.where(col == h, beta_all, 0.0), axis=1, keepdims=True)
            decay = jnp.exp(jnp.where(causal, gc_col - gt8[h:h + 1, :], -jnp.inf))
            q_l.append(q * jnp.exp(gc_col))
            k_l.append(k)
            v_l.append(v.astype(BF16))
            kbf = k.astype(BF16)
            kbf_l.append(kbf)
            decay_l.append(decay)
            m1_l.append(beta_col * decay)
            lhs_l.append(jnp.concatenate([q.astype(BF16), kbf], axis=0))

        r_l = [_dot_nt(lhs_l[h], kbf_l[h]) for h in heads]
        qk_l = [(r_l[h][0:PAIR] * decay_l[h]).astype(BF16) for h in heads]
        a_l = [jnp.where(strict, r_l[h][PAIR:2 * PAIR] * m1_l[h], 0.0) for h in heads]
        x_l = [eye - a_l[h] for h in heads]
        pb_l = [a_l[h].astype(BF16) for h in heads]
        for _ in range(5):
            pb_l = [_dot(pb_l[h], pb_l[h]).astype(BF16) for h in heads]
            x_l = [x_l[h] + _dot(x_l[h].astype(BF16), pb_l[h]) for h in heads]

        u_l = [_dot((x_l[h] * bt8[h:h + 1, :]).astype(BF16), v_l[h]) for h in heads]
        w_l = [_dot((x_l[h] * be8[h:h + 1, :]).astype(BF16), kbf_l[h]) for h in heads]
        kt_l = [(k_l[h].T * k_tail8[h:h + 1, :]).astype(BF16) for h in heads]

        v_new_l = [[], []]
        qs_l = [[], []]
        for a in range(2):
            ra = slice(a * LA_CHUNK, (a + 1) * LA_CHUNK)
            r2_l = [_dot(jnp.concatenate([w_l[h][ra], q_l[h][ra]], axis=0).astype(BF16), s[h].astype(BF16))
                    for h in heads]
            for h in heads:
                v_new = u_l[h][ra] - r2_l[h][0:LA_CHUNK]
                v_new_l[a].append(v_new)
                qs_l[a].append(r2_l[h][LA_CHUNK:2 * LA_CHUNK])
            upd_l = [_dot(kt_l[h], _chunk_halves(v_new_l[a][h], a).astype(BF16)) for h in heads]
            s = [s[h] * chunk_decay[a][h:h + 1, :] + upd_l[h] for h in heads]

        o_l = [jnp.concatenate([qs_l[0][h], qs_l[1][h]], axis=0)
               + _dot(qk_l[h], jnp.concatenate([v_new_l[0][h], v_new_l[1][h]], axis=0).astype(BF16))
               for h in heads]
        for h in heads:
            cs = slice(h * HEAD_DIM, (h + 1) * HEAD_DIM)
            o = o_l[h]
            o = o * lax.rsqrt(jnp.mean(o * o, axis=-1, keepdims=True) + EPS) * onw * _silu(z_ref[rows, cs].astype(F32))
            o_ref[rows, cs] = o.astype(o_ref.dtype)

    for h in heads:
        s_ref[h] = s[h]
    xpad_ref[:, 0:HALO, :] = xpad_ref[:, tl:tl + HALO, :]


def _gdn(proj, gates, cw, gp, onw, cast_arrays, *, batch, seq, tl=256):
    nblk = seq // tl
    cast_in_specs, cast_out_specs, cast_shapes = _cast_specs(cast_arrays, batch * nblk, lambda b, i: b * nblk + i)

    def seg(s_idx):
        return pl.BlockSpec((tl, QK_WIDTH), lambda b, i: (b * nblk + i, s_idx))

    return pl.pallas_call(
        functools.partial(_gdn_kernel, tl=tl, n_cast=len(cast_arrays)),
        grid=(batch, nblk),
        in_specs=[
            seg(0), seg(1), seg(2), seg(3),
            pl.BlockSpec((tl, LANES), lambda b, i: (b * nblk + i, 0)),
            pl.BlockSpec((3, CONV_WIDTH, QK_WIDTH), lambda b, i: (0, 0, 0)),
            pl.BlockSpec((8, LANES), lambda b, i: (0, 0)),
            pl.BlockSpec((1, HEAD_DIM), lambda b, i: (0, 0)),
        ] + cast_in_specs,
        out_specs=[pl.BlockSpec((tl, QK_WIDTH), lambda b, i: (b * nblk + i, 0))] + cast_out_specs,
        out_shape=[jax.ShapeDtypeStruct((batch * seq, QK_WIDTH), BF16)] + cast_shapes,
        scratch_shapes=[pltpu.VMEM((3, HALO + tl, QK_WIDTH), F32), pltpu.VMEM((N_HEADS, HEAD_DIM, HEAD_DIM), F32)],
        compiler_params=_params(2, 48),
        name="gdn",
    )(proj, proj, proj, proj, gates, cw, gp, onw, *cast_arrays)


def _ret_kernel(*refs, tl, n_cast):
    main_in, rest = refs[:10], refs[10:]
    cast_in, o_ref, cast_out, (s_ref,) = rest[:n_cast], rest[n_cast], rest[n_cast + 1:2 * n_cast + 1], rest[2 * n_cast + 1:]
    _cast_slabs(cast_in, cast_out)
    _ret_body(*main_in, o_ref, s_ref, tl=tl)


def _ret_body(q_ref, k_ref, v_ref, g_ref, cos_ref, sin_ref, dmat_ref, qs_ref, ks_ref, cd_ref, o_ref, s_ref, *, tl):
    heads = range(N_HEADS)

    @pl.when(pl.program_id(1) == 0)
    def _():
        s_ref[...] = jnp.zeros_like(s_ref)

    s = [s_ref[h] for h in heads]

    for p in range(tl // PAIR):
        rows = slice(p * PAIR, (p + 1) * PAIR)
        cosf = cos_ref[rows, :]
        sinf = sin_ref[rows, :]

        q_l, kb_l, kdt_l, v_l = [], [], [], []
        for h in heads:
            cs = slice(h * HEAD_DIM, (h + 1) * HEAD_DIM)
            q = q_ref[rows, cs].astype(F32)
            k = k_ref[rows, cs].astype(F32)
            q = q * cosf + pltpu.roll(q, HEAD_DIM // 2, axis=1) * sinf
            k = (k * cosf + pltpu.roll(k, HEAD_DIM // 2, axis=1) * sinf) * (HEAD_DIM ** -0.5)
            q_l.append(q)
            kb_l.append(k.astype(BF16))
            kdt_l.append((k * ks_ref[h]).T.astype(BF16))
            v_l.append(v_ref[rows, cs])

        qk_l = [(_dot_nt(q_l[h].astype(BF16), kb_l[h]) * dmat_ref[h]).astype(BF16) for h in heads]
        inner_l = [_dot(qk_l[h], v_l[h].astype(BF16)) for h in heads]
        qd_l = [(q_l[h] * qs_ref[h]).astype(BF16) for h in heads]

        cross_l = [[], []]
        for a in range(2):
            ra = slice(a * LA_CHUNK, (a + 1) * LA_CHUNK)
            cross_l[a] = [_dot(qd_l[h][ra], s[h].astype(BF16)) for h in heads]
            upd_l = [_dot(kdt_l[h], _chunk_halves(v_l[h][ra], a).astype(BF16)) for h in heads]
            s = [s[h] * cd_ref[h] + upd_l[h] for h in heads]

        for h in heads:
            cs = slice(h * HEAD_DIM, (h + 1) * HEAD_DIM)
            o = inner_l[h] + jnp.concatenate([cross_l[0][h], cross_l[1][h]], axis=0)
            o = o * lax.rsqrt(jnp.mean(o * o, axis=-1, keepdims=True) + EPS)
            o_ref[rows, cs] = (_silu(g_ref[rows, cs].astype(F32)) * o).astype(o_ref.dtype)

    for h in heads:
        s_ref[h] = s[h]


def _ret(proj, cosf, sinf, dmat, q_scale, k_scale, chunk_decay, cast_arrays, *, batch, seq, tl=256):
    nblk = seq // tl
    cast_in_specs, cast_out_specs, cast_shapes = _cast_specs(cast_arrays, batch * nblk, lambda b, i: b * nblk + i)

    def seg(s_idx):
        return pl.BlockSpec((tl, QK_WIDTH), lambda b, i: (b * nblk + i, s_idx))

    def table():
        return pl.BlockSpec((N_HEADS, PAIR, HEAD_DIM), lambda b, i: (0, 0, 0))

    return pl.pallas_call(
        functools.partial(_ret_kernel, tl=tl, n_cast=len(cast_arrays)),
        grid=(batch, nblk),
        in_specs=[
            seg(4), seg(5), seg(6), seg(7),
            pl.BlockSpec((tl, HEAD_DIM), lambda b, i: (i, 0)),
            pl.BlockSpec((tl, HEAD_DIM), lambda b, i: (i, 0)),
            table(), table(), table(), table(),
        ] + cast_in_specs,
        out_specs=[pl.BlockSpec((tl, QK_WIDTH), lambda b, i: (b * nblk + i, 0))] + cast_out_specs,
        out_shape=[jax.ShapeDtypeStruct((batch * seq, QK_WIDTH), BF16)] + cast_shapes,
        scratch_shapes=[pltpu.VMEM((N_HEADS, HEAD_DIM, HEAD_DIM), F32)],
        compiler_params=_params(2, 32),
        name="retention",
    )(proj, proj, proj, proj, cosf, sinf, dmat, q_scale, k_scale, chunk_decay, *cast_arrays)


def _sg_gate_kernel(u_ref, v_ref, lnw_ref, lnb_ref, ws_ref, bs_ref, o_ref, vn_ref, *, tm):
    lnw = lnw_ref[...]
    lnb = lnb_ref[...]
    step = 16

    def body(r, carry):
        sl = pl.ds(pl.multiple_of(r * step, step), step)
        x = v_ref[sl, :].astype(F32)
        mu = jnp.mean(x, axis=-1, keepdims=True)
        xc = x - mu
        var = jnp.mean(xc * xc, axis=-1, keepdims=True)
        vn_ref[sl, :] = (xc * lax.rsqrt(var + EPS) * lnw + lnb).astype(vn_ref.dtype)
        return carry

    lax.fori_loop(0, tm // step, body, 0, unroll=ROW_LOOP_UNROLL)

    row = lax.broadcasted_iota(jnp.int32, (SG_CHUNK, SG_CHUNK), 0)
    col = lax.broadcasted_iota(jnp.int32, (SG_CHUNK, SG_CHUNK), 1)
    for g in range(SG_GROUPS):
        wg = jnp.where(col <= row, ws_ref[g], 0.0).astype(BF16)
        bias = bs_ref[:, g:g + 1]
        cols = slice(g * SG_GROUP_DIM, (g + 1) * SG_GROUP_DIM)
        for c in range(tm // SG_CHUNK):
            rows = slice(c * SG_CHUNK, (c + 1) * SG_CHUNK)
            sgate = _dot(wg, vn_ref[rows, cols]) + bias
            o_ref[rows, cols] = (u_ref[rows, cols].astype(F32) * sgate).astype(o_ref.dtype)


def _sg_gate(proj, lnw, lnb, ws, bs_t, *, tm=512):
    t = proj.shape[0]
    return pl.pallas_call(
        functools.partial(_sg_gate_kernel, tm=tm),
        grid=(t // tm,),
        in_specs=[
            pl.BlockSpec((tm, SG_WIDTH), lambda i: (i, 0)),
            pl.BlockSpec((tm, SG_WIDTH), lambda i: (i, 1)),
            pl.BlockSpec((1, SG_WIDTH), lambda i: (0, 0)),
            pl.BlockSpec((1, SG_WIDTH), lambda i: (0, 0)),
            pl.BlockSpec((SG_GROUPS, SG_CHUNK, SG_CHUNK), lambda i: (0, 0, 0)),
            pl.BlockSpec((SG_CHUNK, SG_GROUPS), lambda i: (0, 0)),
        ],
        out_specs=pl.BlockSpec((tm, SG_WIDTH), lambda i: (i, 0)),
        out_shape=jax.ShapeDtypeStruct((t, SG_WIDTH), BF16),
        scratch_shapes=[pltpu.VMEM((tm, SG_WIDTH), BF16)],
        compiler_params=_params(1, 40),
        name="sg_gate",
    )(proj, proj, lnw, lnb, ws, bs_t)


def _retention_tables(seq):
    half = HEAD_DIM // 2
    inv_freq = 1.0 / (ROPE_BASE ** jnp.linspace(0.0, 1.0, half, dtype=F32))
    ang = jnp.arange(seq, dtype=F32)[:, None] * inv_freq[None, :]
    cos = jnp.cos(ang)
    sin = jnp.sin(ang)
    cosf = jnp.concatenate([cos, cos], axis=1)
    sinf = jnp.concatenate([-sin, sin], axis=1)

    log_gamma = jnp.log1p(-jnp.power(2.0, -5.0 - jnp.arange(N_HEADS, dtype=F32)))
    t = jnp.arange(PAIR)
    pos = (t % LA_CHUNK).astype(F32)
    mask = ((t[:, None] // LA_CHUNK) == (t[None, :] // LA_CHUNK)) & (t[None, :] <= t[:, None])
    dmat = jnp.exp(jnp.where(mask[None], (pos[:, None] - pos[None, :])[None] * log_gamma[:, None, None], -jnp.inf))
    full = (N_HEADS, PAIR, HEAD_DIM)
    q_scale = jnp.broadcast_to(jnp.exp((pos[None, :] + 1.0) * log_gamma[:, None])[:, :, None], full)
    k_scale = jnp.broadcast_to(jnp.exp((LA_CHUNK - 1.0 - pos[None, :]) * log_gamma[:, None])[:, :, None], full)
    chunk_decay = jnp.broadcast_to(jnp.exp(LA_CHUNK * log_gamma)[:, None, None], full)
    return cosf, sinf, dmat, q_scale, k_scale, chunk_decay


def kernel(x, norm_w, la_w_in, la_conv_w, la_a_log, la_dt_bias, la_out_norm_w, la_w_out, sg_w_in, sg_ln_w,
           sg_ln_b, sg_w_s, sg_b_s, sg_w_out, ffn_w_up, ffn_w_down):
    batch, seq, d = x.shape
    t = batch * seq
    h = x.reshape(t, d)
    gate0 = 4 * QK_WIDTH

    w_in = la_w_in[0]
    w_a = w_in[:, :gate0].astype(BF16)
    w_b = w_in[:, gate0 + 2 * N_HEADS:].astype(BF16)
    w_gate = jnp.pad(w_in[:, gate0:gate0 + 2 * N_HEADS], ((0, 0), (0, LANES - 2 * N_HEADS))).astype(BF16)
    proj, gates = _inproj_la(h, norm_w[0, 0][None, :], w_a, w_b, w_gate)

    cw = la_conv_w[0].reshape(3, QK_WIDTH, CONV_WIDTH).transpose(0, 2, 1)
    gp = jnp.zeros((8, LANES), F32)
    gp = gp.at[0, N_HEADS:2 * N_HEADS].set(la_a_log[0]).at[1, N_HEADS:2 * N_HEADS].set(la_dt_bias[0])
    n_layers, _, hidden = ffn_w_up.shape
    o_a, w_up, w_down = _gdn(proj, gates, cw, gp, la_out_norm_w[0][None, :],
                             [ffn_w_up.reshape(n_layers * d, hidden), ffn_w_down.reshape(n_layers * hidden, d)],
                             batch=batch, seq=seq)
    w_up = w_up.reshape(n_layers, d, hidden)
    w_down = w_down.reshape(n_layers, hidden, d)
    o_b, w_out, w_sg_in, w_sg_out = _ret(proj, *_retention_tables(seq), [la_w_out[0], sg_w_in[0], sg_w_out[0]],
                                         batch=batch, seq=seq)

    h = _out_rms_res([o_a, o_b], [w_out, w_out], h, norm_w[0, 1][None, :], tm=512, vmem_mib=48)
    h = _ffn(h, norm_w[0, 2][None, :], w_up, w_down, norm_w[0, 3][None, :], 0)

    proj1 = _inproj_gelu(h, norm_w[1, 0][None, :], w_sg_in)
    gated = _sg_gate(proj1, sg_ln_w[0][None, :], sg_ln_b[0][None, :], sg_w_s[0], sg_b_s[0].T)
    h = _out_rms_res([gated], [w_sg_out], h, norm_w[1, 1][None, :], tm=256, vmem_mib=56)
    h = _ffn(h, norm_w[1, 2][None, :], w_up, w_down, norm_w[1, 3][None, :], 1)
    return h.reshape(batch, seq, d)
```

```python
import functools
import math

import jax
import jax.numpy as jnp
from jax import lax
from jax.experimental import pallas as pl
from jax.experimental.pallas import tpu as pltpu

F32 = jnp.float32
BF16 = jnp.bfloat16

D_MODEL = 2048
N_HEADS = 8
HEAD_DIM = 128
QK_WIDTH = N_HEADS * HEAD_DIM
LA_CHUNK = 64
PAIR = 2 * LA_CHUNK
CONV_WIDTH = 4
HALO = 8
ROPE_BASE = 10000.0
SG_CHUNK = 128
SG_GROUPS = 8
SG_WIDTH = 2 * D_MODEL
SG_GROUP_DIM = SG_WIDTH // SG_GROUPS
FFN_HIDDEN = 4 * D_MODEL
EPS = 1e-6
LANES = 128
ROW_LOOP_UNROLL = 8
MIB = 1024 * 1024

NT_DIMS = (((1,), (1,)), ((), ()))


def _params(n_grid_axes, vmem_mib):
    return pltpu.CompilerParams(dimension_semantics=("arbitrary",) * n_grid_axes,
                                vmem_limit_bytes=vmem_mib * MIB)


def _dot(a, b):
    return jnp.dot(a, b, preferred_element_type=F32)


def _dot_nt(a, b):
    return lax.dot_general(a, b, NT_DIMS, preferred_element_type=F32)


def _silu(x):
    return x * jax.nn.sigmoid(x)


def _softplus(x):
    return jnp.maximum(x, 0.0) + jnp.log1p(jnp.exp(-jnp.abs(x)))


def _gelu_tanh(x):
    c = math.sqrt(2.0 / math.pi)
    return x * (0.5 * (1.0 + jnp.tanh(c * (x + 0.044715 * (x * x * x)))))


def _rmsnorm_rows_to(src_ref, nw_ref, dst_ref, rows):
    nw = nw_ref[...]
    step = 16

    def body(r, carry):
        sl = pl.ds(pl.multiple_of(r * step, step), step)
        x = src_ref[sl, :]
        ms = jnp.mean(x * x, axis=-1, keepdims=True)
        dst_ref[sl, :] = (x * lax.rsqrt(ms + EPS) * nw).astype(dst_ref.dtype)
        return carry

    lax.fori_loop(0, rows // step, body, 0, unroll=ROW_LOOP_UNROLL)


def _rms_residual_rows(acc_ref, h_ref, nw_ref, o_ref, rows):
    nw = nw_ref[...]
    step = 16

    def body(r, carry):
        sl = pl.ds(pl.multiple_of(r * step, step), step)
        y = acc_ref[sl, :]
        ms = jnp.mean(y * y, axis=-1, keepdims=True)
        o_ref[sl, :] = h_ref[sl, :] + y * lax.rsqrt(ms + EPS) * nw
        return carry

    lax.fori_loop(0, rows // step, body, 0, unroll=ROW_LOOP_UNROLL)


def _column_chunks(width, chunk=1024):
    return [slice(c, c + chunk) for c in range(0, width, chunk)]


def _inproj_la_kernel(x_ref, nw_ref, w_ref, wg_ref, o_ref, g_ref, xn_ref, *, tm):
    @pl.when(pl.program_id(1) == 0)
    def _():
        _rmsnorm_rows_to(x_ref, nw_ref, xn_ref, tm)
        g_ref[...] = _dot(xn_ref[...], wg_ref[...])

    for cols in _column_chunks(o_ref.shape[1]):
        o_ref[:, cols] = _dot(xn_ref[...], w_ref[:, cols]).astype(o_ref.dtype)


def _inproj_la(x, nw, w, wg, *, tm=1024, tn=2048):
    t, k = x.shape
    parts, _, cols = w.shape
    per_part = cols // tn
    n = parts * cols
    return pl.pallas_call(
        functools.partial(_inproj_la_kernel, tm=tm),
        grid=(t // tm, n // tn),
        in_specs=[
            pl.BlockSpec((tm, k), lambda i, j: (i, 0)),
            pl.BlockSpec((1, k), lambda i, j: (0, 0)),
            pl.BlockSpec((None, k, tn), lambda i, j: (j // per_part, 0, j % per_part)),
            pl.BlockSpec((k, LANES), lambda i, j: (0, 0)),
        ],
        out_specs=[
            pl.BlockSpec((tm, tn), lambda i, j: (i, j)),
            pl.BlockSpec((tm, LANES), lambda i, j: (i, 0)),
        ],
        out_shape=[jax.ShapeDtypeStruct((t, n), BF16), jax.ShapeDtypeStruct((t, LANES), F32)],
        scratch_shapes=[pltpu.VMEM((tm, k), BF16)],
        compiler_params=_params(2, 56),
        name="inproj_la",
    )(x, nw, w, wg)


def _inproj_gelu_kernel(x_ref, nw_ref, w_ref, o_ref, xn_ref, *, tm):
    @pl.when(pl.program_id(1) == 0)
    def _():
        _rmsnorm_rows_to(x_ref, nw_ref, xn_ref, tm)

    for cols in _column_chunks(o_ref.shape[1]):
        o_ref[:, cols] = _gelu_tanh(_dot(xn_ref[...], w_ref[:, cols])).astype(o_ref.dtype)


def _inproj_gelu(x, nw, w, *, tm=1024, tn=2048):
    t, k = x.shape
    n = w.shape[1]
    return pl.pallas_call(
        functools.partial(_inproj_gelu_kernel, tm=tm),
        grid=(t // tm, n // tn),
        in_specs=[
            pl.BlockSpec((tm, k), lambda i, j: (i, 0)),
            pl.BlockSpec((1, k), lambda i, j: (0, 0)),
            pl.BlockSpec((k, tn), lambda i, j: (0, j)),
        ],
        out_specs=pl.BlockSpec((tm, tn), lambda i, j: (i, j)),
        out_shape=jax.ShapeDtypeStruct((t, n), BF16),
        scratch_shapes=[pltpu.VMEM((tm, k), BF16)],
        compiler_params=_params(2, 56),
        name="inproj_gelu",
    )(x, nw, w)


def _out_rms_res_kernel(*refs, n_pairs, tm):
    a_refs = refs[:n_pairs]
    w_refs = refs[n_pairs:2 * n_pairs]
    h_ref, nw_ref, o_ref, acc_ref = refs[2 * n_pairs:]
    y = _dot(a_refs[0][...], w_refs[0][...])
    for a_ref, w_ref in zip(a_refs[1:], w_refs[1:]):
        y = y + _dot(a_ref[...], w_ref[...])
    acc_ref[...] = y
    _rms_residual_rows(acc_ref, h_ref, nw_ref, o_ref, tm)


def _out_rms_res(a_list, w_list, h, nw, *, tm, vmem_mib):
    t, n = h.shape
    n_pairs = len(a_list)
    in_specs = [pl.BlockSpec((tm, a.shape[1]), lambda i: (i, 0)) for a in a_list]
    k0 = 0
    for a, w in zip(a_list, w_list):
        ka = a.shape[1]
        row_block = k0 // ka if w.shape[0] != ka else 0
        in_specs.append(pl.BlockSpec((ka, n), lambda i, rb=row_block: (rb, 0)))
        k0 += ka
    in_specs += [pl.BlockSpec((tm, n), lambda i: (i, 0)), pl.BlockSpec((1, n), lambda i: (0, 0))]
    return pl.pallas_call(
        functools.partial(_out_rms_res_kernel, n_pairs=n_pairs, tm=tm),
        grid=(t // tm,),
        in_specs=in_specs,
        out_specs=pl.BlockSpec((tm, n), lambda i: (i, 0)),
        out_shape=jax.ShapeDtypeStruct((t, n), F32),
        scratch_shapes=[pltpu.VMEM((tm, n), F32)],
        compiler_params=_params(1, vmem_mib),
        name="out_rms_res",
    )(*a_list, *w_list, h, nw)


def _ffn_kernel(h_ref, h_next_ref, nw_in_ref, wup_ref, wdn_ref, nw_out_ref, o_ref, xn_ref, acc_ref, *, tm, n_j):
    i = pl.program_id(0)
    j = pl.program_id(1)
    slot = lax.rem(i, 2)

    @pl.when((i == 0) & (j == 0))
    def _():
        _rmsnorm_rows_to(h_ref, nw_in_ref, xn_ref.at[0], tm)

    @pl.when(j == 0)
    def _():
        acc_ref[...] = jnp.zeros_like(acc_ref)

    u = _dot(xn_ref[slot], wup_ref[...])
    u = jnp.square(jnp.maximum(u, 0.0)).astype(BF16)
    acc_ref[...] += _dot(u, wdn_ref[...])

    nw_in = nw_in_ref[...]
    share = tm // n_j
    xn_next_ref = xn_ref.at[1 - slot]
    for g in range(share // 16):
        sl = pl.ds(pl.multiple_of(j * share + g * 16, 16), 16)
        x = h_next_ref[sl, :]
        ms = jnp.mean(x * x, axis=-1, keepdims=True)
        xn_next_ref[sl, :] = (x * lax.rsqrt(ms + EPS) * nw_in).astype(BF16)

    @pl.when(j == n_j - 1)
    def _():
        _rms_residual_rows(acc_ref, h_ref, nw_out_ref, o_ref, tm)


def _ffn(h, nw_in, wup, wdn, nw_out, layer, *, tm=512, th=1024):
    t, d = h.shape
    hidden = wup.shape[2]
    n_i, n_j = t // tm, hidden // th
    return pl.pallas_call(
        functools.partial(_ffn_kernel, tm=tm, n_j=n_j),
        grid=(n_i, n_j),
        in_specs=[
            pl.BlockSpec((tm, d), lambda i, j: (i, 0)),
            pl.BlockSpec((tm, d), lambda i, j: (jnp.minimum(i + 1, n_i - 1), 0)),
            pl.BlockSpec((1, d), lambda i, j: (0, 0)),
            pl.BlockSpec((None, d, th), lambda i, j: (layer, 0, j)),
            pl.BlockSpec((None, th, d), lambda i, j: (layer, j, 0)),
            pl.BlockSpec((1, d), lambda i, j: (0, 0)),
        ],
        out_specs=pl.BlockSpec((tm, d), lambda i, j: (i, 0)),
        out_shape=jax.ShapeDtypeStruct((t, d), F32),
        scratch_shapes=[pltpu.VMEM((2, tm, d), BF16), pltpu.VMEM((tm, d), F32)],
        compiler_params=_params(2, 56),
        name="ffn",
    )(h, h, nw_in, wup, wdn, nw_out)


def _cast_specs(arrays, n_steps, step_index):
    in_specs, out_specs, out_shapes = [], [], []
    for a in arrays:
        rows, cols = a.shape[0] // n_steps, a.shape[1]
        in_specs.append(pl.BlockSpec((rows, cols), lambda b, i: (step_index(b, i), 0)))
        out_specs.append(pl.BlockSpec((rows, cols), lambda b, i: (step_index(b, i), 0)))
        out_shapes.append(jax.ShapeDtypeStruct(a.shape, BF16))
    return in_specs, out_specs, out_shapes


def _cast_slabs(in_refs, out_refs):
    for in_ref, out_ref in zip(in_refs, out_refs):
        out_ref[...] = in_ref[...].astype(out_ref.dtype)


def _pair_masks():
    row = lax.broadcasted_iota(jnp.int32, (PAIR, PAIR), 0)
    col = lax.broadcasted_iota(jnp.int32, (PAIR, PAIR), 1)
    same_chunk = (row >> 6) == (col >> 6)
    return row, col, same_chunk


def _chunk_halves(parts, a):
    zeros = jnp.zeros_like(parts)
    return jnp.concatenate([parts, zeros] if a == 0 else [zeros, parts], axis=0)


def _gdn_kernel(*refs, tl, n_cast):
    (q_ref, k_ref, v_ref, z_ref, gate_ref, cw_ref, gp_ref, onw_ref), rest = refs[:8], refs[8:]
    cast_in, o_ref, cast_out, (xpad_ref, s_ref) = rest[:n_cast], rest[n_cast], rest[n_cast + 1:2 * n_cast + 1], rest[2 * n_cast + 1:]
    _cast_slabs(cast_in, cast_out)
    _gdn_body(q_ref, k_ref, v_ref, z_ref, gate_ref, cw_ref, gp_ref, onw_ref, o_ref, xpad_ref, s_ref, tl=tl)


def _gdn_body(q_ref, k_ref, v_ref, z_ref, gate_ref, cw_ref, gp_ref, onw_ref, o_ref,
              xpad_ref, s_ref, *, tl):
    nh = N_HEADS
    heads = range(nh)

    @pl.when(pl.program_id(1) == 0)
    def _():
        xpad_ref[:, 0:HALO, :] = jnp.zeros((3, HALO, QK_WIDTH), F32)
        s_ref[...] = jnp.zeros_like(s_ref)

    xpad_ref[0, HALO:HALO + tl, :] = q_ref[...].astype(F32)
    xpad_ref[1, HALO:HALO + tl, :] = k_ref[...].astype(F32)
    xpad_ref[2, HALO:HALO + tl, :] = v_ref[...].astype(F32)

    row, col, same_chunk = _pair_masks()
    causal = same_chunk & (col <= row)
    strict = same_chunk & (col < row)
    eye = (row == col).astype(F32)
    row_in_chunk = row & (LA_CHUNK - 1)
    lane8 = lax.broadcasted_iota(jnp.int32, (nh, PAIR), 1)

    a_coef = -jnp.exp(gp_ref[0:1, :])
    dt_bias = gp_ref[1:2, :]
    onw = onw_ref[...]
    s = [s_ref[h] for h in heads]

    for p in range(tl // PAIR):
        r0 = p * PAIR
        rows = slice(r0, r0 + PAIR)

        pre = gate_ref[rows, :]
        beta_all = jax.nn.sigmoid(pre)
        g = a_coef * _softplus(pre + dt_bias)
        for sh in (1, 2, 4, 8, 16, 32):
            g = g + jnp.where(row_in_chunk >= sh, pltpu.roll(g, sh, axis=0), 0.0)
        gt8 = g.T[nh:2 * nh, :]
        bt8 = beta_all.T[0:nh, :]
        g_last = [jnp.sum(jnp.where(lane8 == (a + 1) * LA_CHUNK - 1, gt8, 0.0), axis=1, keepdims=True)
                  for a in range(2)]
        k_tail8 = jnp.exp(jnp.where(lane8 < LA_CHUNK, g_last[0], g_last[1]) - gt8)
        be8 = bt8 * jnp.exp(gt8)
        chunk_decay = [jnp.exp(gl) for gl in g_last]

        def conv(ci, cs):
            w = cw_ref[ci, :, cs]
            acc = xpad_ref[ci, HALO + r0:HALO + r0 + PAIR, cs] * w[CONV_WIDTH - 1:CONV_WIDTH, :]
            for j in range(1, CONV_WIDTH):
                acc = acc + xpad_ref[ci, HALO + r0 - j:HALO + r0 - j + PAIR, cs] * w[CONV_WIDTH - 1 - j:CONV_WIDTH - j, :]
            return _silu(acc)

        q_l, k_l, v_l, kbf_l, decay_l, m1_l, lhs_l = [], [], [], [], [], [], []
        for h in heads:
            cs = slice(h * HEAD_DIM, (h + 1) * HEAD_DIM)
            q = conv(0, cs)
            k = conv(1, cs)
            v = conv(2, cs)
            q = q * lax.rsqrt(jnp.sum(q * q, axis=-1, keepdims=True) + 1e-6) * (HEAD_DIM ** -0.5)
            k = k * lax.rsqrt(jnp.sum(k * k, axis=-1, keepdims=True) + 1e-6)
            gc_col = jnp.sum(jnp.where(col == nh + h, g, 0.0), axis=1, keepdims=True)
            beta_col = jnp.sum(jnp.where(col == h, beta_all, 0.0), axis=1, keepdims=True)
            decay = jnp.exp(jnp.where(causal, gc_col - gt8[h:h + 1, :], -jnp.inf))
            q_l.append(q * jnp.exp(gc_col))
            k_l.append(k)
            v_l.append(v.astype(BF16))
            kbf = k.astype(BF16)
            kbf_l.append(kbf)
            decay_l.append(decay)
            m1_l.append(beta_col * decay)
            lhs_l.append(jnp.concatenate([q.astype(BF16), kbf], axis=0))

        r_l = [_dot_nt(lhs_l[h], kbf_l[h]) for h in heads]
        qk_l = [(r_l[h][0:PAIR] * decay_l[h]).astype(BF16) for h in heads]
        a_l = [jnp.where(strict, r_l[h][PAIR:2 * PAIR] * m1_l[h], 0.0) for h in heads]
        x_l = [eye - a_l[h] for h in heads]
        pb_l = [a_l[h].astype(BF16) for h in heads]
        for _ in range(5):
            pb_l = [_dot(pb_l[h], pb_l[h]).astype(BF16) for h in heads]
            x_l = [x_l[h] + _dot(x_l[h].astype(BF16), pb_l[h]) for h in heads]

        u_l = [_dot((x_l[h] * bt8[h:h + 1, :]).astype(BF16), v_l[h]) for h in heads]
        w_l = [_dot((x_l[h] * be8[h:h + 1, :]).astype(BF16), kbf_l[h]) for h in heads]
        kt_l = [(k_l[h].T * k_tail8[h:h + 1, :]).astype(BF16) for h in heads]

        v_new_l = [[], []]
        qs_l = [[], []]
        for a in range(2):
            ra = slice(a * LA_CHUNK, (a + 1) * LA_CHUNK)
            r2_l = [_dot(jnp.concatenate([w_l[h][ra], q_l[h][ra]], axis=0).astype(BF16), s[h].astype(BF16))
                    for h in heads]
            for h in heads:
                v_new = u_l[h][ra] - r2_l[h][0:LA_CHUNK]
                v_new_l[a].append(v_new)
                qs_l[a].append(r2_l[h][LA_CHUNK:2 * LA_CHUNK])
            upd_l = [_dot(kt_l[h], _chunk_halves(v_new_l[a][h], a).astype(BF16)) for h in heads]
            s = [s[h] * chunk_decay[a][h:h + 1, :] + upd_l[h] for h in heads]

        o_l = [jnp.concatenate([qs_l[0][h], qs_l[1][h]], axis=0)
               + _dot(qk_l[h], jnp.concatenate([v_new_l[0][h], v_new_l[1][h]], axis=0).astype(BF16))
               for h in heads]
        for h in heads:
            cs = slice(h * HEAD_DIM, (h + 1) * HEAD_DIM)
            o = o_l[h]
            o = o * lax.rsqrt(jnp.mean(o * o, axis=-1, keepdims=True) + EPS) * onw * _silu(z_ref[rows, cs].astype(F32))
            o_ref[rows, cs] = o.astype(o_ref.dtype)

    for h in heads:
        s_ref[h] = s[h]
    xpad_ref[:, 0:HALO, :] = xpad_ref[:, tl:tl + HALO, :]


def _gdn(proj, gates, cw, gp, onw, cast_arrays, *, batch, seq, tl=256):
    nblk = seq // tl
    cast_in_specs, cast_out_specs, cast_shapes = _cast_specs(cast_arrays, batch * nblk, lambda b, i: b * nblk + i)

    def seg(s_idx):
        return pl.BlockSpec((tl, QK_WIDTH), lambda b, i: (b * nblk + i, s_idx))

    return pl.pallas_call(
        functools.partial(_gdn_kernel, tl=tl, n_cast=len(cast_arrays)),
        grid=(batch, nblk),
        in_specs=[
            seg(0), seg(1), seg(2), seg(3),
            pl.BlockSpec((tl, LANES), lambda b, i: (b * nblk + i, 0)),
            pl.BlockSpec((3, CONV_WIDTH, QK_WIDTH), lambda b, i: (0, 0, 0)),
            pl.BlockSpec((8, LANES), lambda b, i: (0, 0)),
            pl.BlockSpec((1, HEAD_DIM), lambda b, i: (0, 0)),
        ] + cast_in_specs,
        out_specs=[pl.BlockSpec((tl, QK_WIDTH), lambda b, i: (b * nblk + i, 0))] + cast_out_specs,
        out_shape=[jax.ShapeDtypeStruct((batch * seq, QK_WIDTH), BF16)] + cast_shapes,
        scratch_shapes=[pltpu.VMEM((3, HALO + tl, QK_WIDTH), F32), pltpu.VMEM((N_HEADS, HEAD_DIM, HEAD_DIM), F32)],
        compiler_params=_params(2, 48),
        name="gdn",
    )(proj, proj, proj, proj, gates, cw, gp, onw, *cast_arrays)


def _ret_kernel(*refs, tl, n_cast):
    main_in, rest = refs[:10], refs[10:]
    cast_in, o_ref, cast_out, (s_ref,) = rest[:n_cast], rest[n_cast], rest[n_cast + 1:2 * n_cast + 1], rest[2 * n_cast + 1:]
    _cast_slabs(cast_in, cast_out)
    _ret_body(*main_in, o_ref, s_ref, tl=tl)


def _ret_body(q_ref, k_ref, v_ref, g_ref, cos_ref, sin_ref, dmat_ref, qs_ref, ks_ref, cd_ref, o_ref, s_ref, *, tl):
    heads = range(N_HEADS)

    @pl.when(pl.program_id(1) == 0)
    def _():
        s_ref[...] = jnp.zeros_like(s_ref)

    s = [s_ref[h] for h in heads]

    for p in range(tl // PAIR):
        rows = slice(p * PAIR, (p + 1) * PAIR)
        cosf = cos_ref[rows, :]
        sinf = sin_ref[rows, :]

        q_l, kb_l, kdt_l, v_l = [], [], [], []
        for h in heads:
            cs = slice(h * HEAD_DIM, (h + 1) * HEAD_DIM)
            q = q_ref[rows, cs].astype(F32)
            k = k_ref[rows, cs].astype(F32)
            q = q * cosf + pltpu.roll(q, HEAD_DIM // 2, axis=1) * sinf
            k = (k * cosf + pltpu.roll(k, HEAD_DIM // 2, axis=1) * sinf) * (HEAD_DIM ** -0.5)
            q_l.append(q)
            kb_l.append(k.astype(BF16))
            kdt_l.append((k * ks_ref[h]).T.astype(BF16))
            v_l.append(v_ref[rows, cs])

        qk_l = [(_dot_nt(q_l[h].astype(BF16), kb_l[h]) * dmat_ref[h]).astype(BF16) for h in heads]
        inner_l = [_dot(qk_l[h], v_l[h].astype(BF16)) for h in heads]
        qd_l = [(q_l[h] * qs_ref[h]).astype(BF16) for h in heads]

        cross_l = [[], []]
        for a in range(2):
            ra = slice(a * LA_CHUNK, (a + 1) * LA_CHUNK)
            cross_l[a] = [_dot(qd_l[h][ra], s[h].astype(BF16)) for h in heads]
            upd_l = [_dot(kdt_l[h], _chunk_halves(v_l[h][ra], a).astype(BF16)) for h in heads]
            s = [s[h] * cd_ref[h] + upd_l[h] for h in heads]

        for h in heads:
            cs = slice(h * HEAD_DIM, (h + 1) * HEAD_DIM)
            o = inner_l[h] + jnp.concatenate([cross_l[0][h], cross_l[1][h]], axis=0)
            o = o * lax.rsqrt(jnp.mean(o * o, axis=-1, keepdims=True) + EPS)
            o_ref[rows, cs] = (_silu(g_ref[rows, cs].astype(F32)) * o).astype(o_ref.dtype)

    for h in heads:
        s_ref[h] = s[h]


def _ret(proj, cosf, sinf, dmat, q_scale, k_scale, chunk_decay, cast_arrays, *, batch, seq, tl=256):
    nblk = seq // tl
    cast_in_specs, cast_out_specs, cast_shapes = _cast_specs(cast_arrays, batch * nblk, lambda b, i: b * nblk + i)

    def seg(s_idx):
        return pl.BlockSpec((tl, QK_WIDTH), lambda b, i: (b * nblk + i, s_idx))

    def table():
        return pl.BlockSpec((N_HEADS, PAIR, HEAD_DIM), lambda b, i: (0, 0, 0))

    return pl.pallas_call(
        functools.partial(_ret_kernel, tl=tl, n_cast=len(cast_arrays)),
        grid=(batch, nblk),
        in_specs=[
            seg(4), seg(5), seg(6), seg(7),
            pl.BlockSpec((tl, HEAD_DIM), lambda b, i: (i, 0)),
            pl.BlockSpec((tl, HEAD_DIM), lambda b, i: (i, 0)),
            table(), table(), table(), table(),
        ] + cast_in_specs,
        out_specs=[pl.BlockSpec((tl, QK_WIDTH), lambda b, i: (b * nblk + i, 0))] + cast_out_specs,
        out_shape=[jax.ShapeDtypeStruct((batch * seq, QK_WIDTH), BF16)] + cast_shapes,
        scratch_shapes=[pltpu.VMEM((N_HEADS, HEAD_DIM, HEAD_DIM), F32)],
        compiler_params=_params(2, 32),
        name="retention",
    )(proj, proj, proj, proj, cosf, sinf, dmat, q_scale, k_scale, chunk_decay, *cast_arrays)


def _sg_gate_kernel(u_ref, v_ref, lnw_ref, lnb_ref, ws_ref, bs_ref, o_ref, vn_ref, *, tm):
    lnw = lnw_ref[...]
    lnb = lnb_ref[...]
    step = 16

    def body(r, carry):
        sl = pl.ds(pl.multiple_of(r * step, step), step)
        x = v_ref[sl, :].astype(F32)
        mu = jnp.mean(x, axis=-1, keepdims=True)
        xc = x - mu
        var = jnp.mean(xc * xc, axis=-1, keepdims=True)
        vn_ref[sl, :] = (xc * lax.rsqrt(var + EPS) * lnw + lnb).astype(vn_ref.dtype)
        return carry

    lax.fori_loop(0, tm // step, body, 0, unroll=ROW_LOOP_UNROLL)

    row = lax.broadcasted_iota(jnp.int32, (SG_CHUNK, SG_CHUNK), 0)
    col = lax.broadcasted_iota(jnp.int32, (SG_CHUNK, SG_CHUNK), 1)
    for g in range(SG_GROUPS):
        wg = jnp.where(col <= row, ws_ref[g], 0.0).astype(BF16)
        bias = bs_ref[:, g:g + 1]
        cols = slice(g * SG_GROUP_DIM, (g + 1) * SG_GROUP_DIM)
        for c in range(tm // SG_CHUNK):
            rows = slice(c * SG_CHUNK, (c + 1) * SG_CHUNK)
            sgate = _dot(wg, vn_ref[rows, cols]) + bias
            o_ref[rows, cols] = (u_ref[rows, cols].astype(F32) * sgate).astype(o_ref.dtype)


def _sg_gate(proj, lnw, lnb, ws, bs_t, *, tm=512):
    t = proj.shape[0]
    return pl.pallas_call(
        functools.partial(_sg_gate_kernel, tm=tm),
        grid=(t // tm,),
        in_specs=[
            pl.BlockSpec((tm, SG_WIDTH), lambda i: (i, 0)),
            pl.BlockSpec((tm, SG_WIDTH), lambda i: (i, 1)),
            pl.BlockSpec((1, SG_WIDTH), lambda i: (0, 0)),
            pl.BlockSpec((1, SG_WIDTH), lambda i: (0, 0)),
            pl.BlockSpec((SG_GROUPS, SG_CHUNK, SG_CHUNK), lambda i: (0, 0, 0)),
            pl.BlockSpec((SG_CHUNK, SG_GROUPS), lambda i: (0, 0)),
        ],
        out_specs=pl.BlockSpec((tm, SG_WIDTH), lambda i: (i, 0)),
        out_shape=jax.ShapeDtypeStruct((t, SG_WIDTH), BF16),
        scratch_shapes=[pltpu.VMEM((tm, SG_WIDTH), BF16)],
        compiler_params=_params(1, 40),
        name="sg_gate",
    )(proj, proj, lnw, lnb, ws, bs_t)


def _retention_tables(seq):
    half = HEAD_DIM // 2
    inv_freq = 1.0 / (ROPE_BASE ** jnp.linspace(0.0, 1.0, half, dtype=F32))
    ang = jnp.arange(seq, dtype=F32)[:, None] * inv_freq[None, :]
    cos = jnp.cos(ang)
    sin = jnp.sin(ang)
    cosf = jnp.concatenate([cos, cos], axis=1)
    sinf = jnp.concatenate([-sin, sin], axis=1)

    log_gamma = jnp.log1p(-jnp.power(2.0, -5.0 - jnp.arange(N_HEADS, dtype=F32)))
    t = jnp.arange(PAIR)
    pos = (t % LA_CHUNK).astype(F32)
    mask = ((t[:, None] // LA_CHUNK) == (t[None, :] // LA_CHUNK)) & (t[None, :] <= t[:, None])
    dmat = jnp.exp(jnp.where(mask[None], (pos[:, None] - pos[None, :])[None] * log_gamma[:, None, None], -jnp.inf))
    full = (N_HEADS, PAIR, HEAD_DIM)
    q_scale = jnp.broadcast_to(jnp.exp((pos[None, :] + 1.0) * log_gamma[:, None])[:, :, None], full)
    k_scale = jnp.broadcast_to(jnp.exp((LA_CHUNK - 1.0 - pos[None, :]) * log_gamma[:, None])[:, :, None], full)
    chunk_decay = jnp.broadcast_to(jnp.exp(LA_CHUNK * log_gamma)[:, None, None], full)
    return cosf, sinf, dmat, q_scale, k_scale, chunk_decay


def kernel(x, norm_w, la_w_in, la_conv_w, la_a_log, la_dt_bias, la_out_norm_w, la_w_out, sg_w_in, sg_ln_w,
           sg_ln_b, sg_w_s, sg_b_s, sg_w_out, ffn_w_up, ffn_w_down):
    batch, seq, d = x.shape
    t = batch * seq
    h = x.reshape(t, d)
    gate0 = 4 * QK_WIDTH

    w_in = la_w_in[0]
    w_parts = jnp.stack([w_in[:, :gate0], w_in[:, gate0 + 2 * N_HEADS:]]).astype(BF16)
    w_gate = jnp.pad(w_in[:, gate0:gate0 + 2 * N_HEADS], ((0, 0), (0, LANES - 2 * N_HEADS))).astype(BF16)
    proj, gates = _inproj_la(h, norm_w[0, 0][None, :], w_parts, w_gate)

    cw = la_conv_w[0].reshape(3, QK_WIDTH, CONV_WIDTH).transpose(0, 2, 1)
    gp = jnp.zeros((8, LANES), F32)
    gp = gp.at[0, N_HEADS:2 * N_HEADS].set(la_a_log[0]).at[1, N_HEADS:2 * N_HEADS].set(la_dt_bias[0])
    n_layers, _, hidden = ffn_w_up.shape
    o_a, w_up, w_down = _gdn(proj, gates, cw, gp, la_out_norm_w[0][None, :],
                             [ffn_w_up.reshape(n_layers * d, hidden), ffn_w_down.reshape(n_layers * hidden, d)],
                             batch=batch, seq=seq)
    w_up = w_up.reshape(n_layers, d, hidden)
    w_down = w_down.reshape(n_layers, hidden, d)
    o_b, w_out, w_sg_in, w_sg_out = _ret(proj, *_retention_tables(seq), [la_w_out[0], sg_w_in[0], sg_w_out[0]],
                                         batch=batch, seq=seq)

    h = _out_rms_res([o_a, o_b], [w_out, w_out], h, norm_w[0, 1][None, :], tm=512, vmem_mib=48)
    h = _ffn(h, norm_w[0, 2][None, :], w_up, w_down, norm_w[0, 3][None, :], 0)

    proj1 = _inproj_gelu(h, norm_w[1, 0][None, :], w_sg_in)
    gated = _sg_gate(proj1, sg_ln_w[0][None, :], sg_ln_b[0][None, :], sg_w_s[0], sg_b_s[0].T)
    h = _out_rms_res([gated], [w_sg_out], h, norm_w[1, 1][None, :], tm=512, vmem_mib=56)
    h = _ffn(h, norm_w[1, 2][None, :], w_up, w_down, norm_w[1, 3][None, :], 1)
    return h.reshape(batch, seq, d)
```

```python
import functools
import math

import jax
import jax.numpy as jnp
from jax import lax
from jax.experimental import pallas as pl
from jax.experimental.pallas import tpu as pltpu

F32 = jnp.float32
BF16 = jnp.bfloat16

D_MODEL = 2048
N_HEADS = 8
HEAD_DIM = 128
QK_WIDTH = N_HEADS * HEAD_DIM
LA_CHUNK = 64
PAIR = 2 * LA_CHUNK
CONV_WIDTH = 4
HALO = 8
ROPE_BASE = 10000.0
SG_CHUNK = 128
SG_GROUPS = 8
SG_WIDTH = 2 * D_MODEL
SG_GROUP_DIM = SG_WIDTH // SG_GROUPS
FFN_HIDDEN = 4 * D_MODEL
EPS = 1e-6
LANES = 128
ROW_LOOP_UNROLL = 8
MIB = 1024 * 1024

NT_DIMS = (((1,), (1,)), ((), ()))


def _params(n_grid_axes, vmem_mib):
    return pltpu.CompilerParams(dimension_semantics=("arbitrary",) * n_grid_axes,
                                vmem_limit_bytes=vmem_mib * MIB)


def _dot(a, b):
    return jnp.dot(a, b, preferred_element_type=F32)


def _dot_nt(a, b):
    return lax.dot_general(a, b, NT_DIMS, preferred_element_type=F32)


def _silu(x):
    return x * jax.nn.sigmoid(x)


def _softplus(x):
    return jnp.maximum(x, 0.0) + jnp.log1p(jnp.exp(-jnp.abs(x)))


def _gelu_tanh(x):
    c = math.sqrt(2.0 / math.pi)
    return x * (0.5 * (1.0 + jnp.tanh(c * (x + 0.044715 * (x * x * x)))))


def _rmsnorm_rows_to(src_ref, nw_ref, dst_ref, rows):
    nw = nw_ref[...]
    step = 16

    def body(r, carry):
        sl = pl.ds(pl.multiple_of(r * step, step), step)
        x = src_ref[sl, :]
        ms = jnp.mean(x * x, axis=-1, keepdims=True)
        dst_ref[sl, :] = (x * lax.rsqrt(ms + EPS) * nw).astype(dst_ref.dtype)
        return carry

    lax.fori_loop(0, rows // step, body, 0, unroll=ROW_LOOP_UNROLL)


def _rms_residual_rows(acc_ref, h_ref, nw_ref, o_ref, rows):
    nw = nw_ref[...]
    step = 16

    def body(r, carry):
        sl = pl.ds(pl.multiple_of(r * step, step), step)
        y = acc_ref[sl, :]
        ms = jnp.mean(y * y, axis=-1, keepdims=True)
        o_ref[sl, :] = h_ref[sl, :] + y * lax.rsqrt(ms + EPS) * nw
        return carry

    lax.fori_loop(0, rows // step, body, 0, unroll=ROW_LOOP_UNROLL)


def _column_chunks(width, chunk=1024):
    return [slice(c, c + chunk) for c in range(0, width, chunk)]


def _inproj_la_kernel(x_ref, nw_ref, w_ref, wg_ref, o_ref, g_ref, xn_ref, *, tm):
    @pl.when(pl.program_id(1) == 0)
    def _():
        _rmsnorm_rows_to(x_ref, nw_ref, xn_ref, tm)
        g_ref[...] = _dot_nt(xn_ref[...], wg_ref[...])

    for cols in _column_chunks(o_ref.shape[1]):
        o_ref[:, cols] = _dot_nt(xn_ref[...], w_ref[cols, :]).astype(o_ref.dtype)


def _inproj_la(x, nw, w, wg, *, tm=1024, tn=2048):
    t, k = x.shape
    parts, cols, _ = w.shape
    per_part = cols // tn
    n = parts * cols
    return pl.pallas_call(
        functools.partial(_inproj_la_kernel, tm=tm),
        grid=(t // tm, n // tn),
        in_specs=[
            pl.BlockSpec((tm, k), lambda i, j: (i, 0)),
            pl.BlockSpec((1, k), lambda i, j: (0, 0)),
            pl.BlockSpec((None, tn, k), lambda i, j: (j // per_part, j % per_part, 0)),
            pl.BlockSpec((LANES, k), lambda i, j: (0, 0)),
        ],
        out_specs=[
            pl.BlockSpec((tm, tn), lambda i, j: (i, j)),
            pl.BlockSpec((tm, LANES), lambda i, j: (i, 0)),
        ],
        out_shape=[jax.ShapeDtypeStruct((t, n), BF16), jax.ShapeDtypeStruct((t, LANES), F32)],
        scratch_shapes=[pltpu.VMEM((tm, k), BF16)],
        compiler_params=_params(2, 56),
        name="inproj_la",
    )(x, nw, w, wg)


def _inproj_gelu_kernel(x_ref, nw_ref, w_ref, o_ref, xn_ref, *, tm):
    @pl.when(pl.program_id(1) == 0)
    def _():
        _rmsnorm_rows_to(x_ref, nw_ref, xn_ref, tm)

    for cols in _column_chunks(o_ref.shape[1]):
        o_ref[:, cols] = _gelu_tanh(_dot(xn_ref[...], w_ref[:, cols])).astype(o_ref.dtype)


def _inproj_gelu(x, nw, w, *, tm=1024, tn=2048):
    t, k = x.shape
    n = w.shape[1]
    return pl.pallas_call(
        functools.partial(_inproj_gelu_kernel, tm=tm),
        grid=(t // tm, n // tn),
        in_specs=[
            pl.BlockSpec((tm, k), lambda i, j: (i, 0)),
            pl.BlockSpec((1, k), lambda i, j: (0, 0)),
            pl.BlockSpec((k, tn), lambda i, j: (0, j)),
        ],
        out_specs=pl.BlockSpec((tm, tn), lambda i, j: (i, j)),
        out_shape=jax.ShapeDtypeStruct((t, n), BF16),
        scratch_shapes=[pltpu.VMEM((tm, k), BF16)],
        compiler_params=_params(2, 56),
        name="inproj_gelu",
    )(x, nw, w)


def _out_rms_res_kernel(*refs, n_pairs, tm):
    a_refs = refs[:n_pairs]
    w_refs = refs[n_pairs:2 * n_pairs]
    h_ref, nw_ref, o_ref, acc_ref = refs[2 * n_pairs:]
    y = _dot(a_refs[0][...], w_refs[0][...])
    for a_ref, w_ref in zip(a_refs[1:], w_refs[1:]):
        y = y + _dot(a_ref[...], w_ref[...])
    acc_ref[...] = y
    _rms_residual_rows(acc_ref, h_ref, nw_ref, o_ref, tm)


def _out_rms_res(a_list, w_list, h, nw, *, tm, vmem_mib):
    t, n = h.shape
    n_pairs = len(a_list)
    in_specs = [pl.BlockSpec((tm, a.shape[1]), lambda i: (i, 0)) for a in a_list]
    k0 = 0
    for a, w in zip(a_list, w_list):
        ka = a.shape[1]
        row_block = k0 // ka if w.shape[0] != ka else 0
        in_specs.append(pl.BlockSpec((ka, n), lambda i, rb=row_block: (rb, 0)))
        k0 += ka
    in_specs += [pl.BlockSpec((tm, n), lambda i: (i, 0)), pl.BlockSpec((1, n), lambda i: (0, 0))]
    return pl.pallas_call(
        functools.partial(_out_rms_res_kernel, n_pairs=n_pairs, tm=tm),
        grid=(t // tm,),
        in_specs=in_specs,
        out_specs=pl.BlockSpec((tm, n), lambda i: (i, 0)),
        out_shape=jax.ShapeDtypeStruct((t, n), F32),
        scratch_shapes=[pltpu.VMEM((tm, n), F32)],
        compiler_params=_params(1, vmem_mib),
        name="out_rms_res",
    )(*a_list, *w_list, h, nw)


def _ffn_kernel(h_ref, h_next_ref, nw_in_ref, wup_ref, wdn_ref, nw_out_ref, o_ref, xn_ref, acc_ref, *, tm, n_j):
    i = pl.program_id(0)
    j = pl.program_id(1)
    slot = lax.rem(i, 2)

    @pl.when((i == 0) & (j == 0))
    def _():
        _rmsnorm_rows_to(h_ref, nw_in_ref, xn_ref.at[0], tm)

    @pl.when(j == 0)
    def _():
        acc_ref[...] = jnp.zeros_like(acc_ref)

    u = _dot(xn_ref[slot], wup_ref[...])
    u = jnp.square(jnp.maximum(u, 0.0)).astype(BF16)
    acc_ref[...] += _dot(u, wdn_ref[...])

    nw_in = nw_in_ref[...]
    share = tm // n_j
    xn_next_ref = xn_ref.at[1 - slot]
    for g in range(share // 16):
        sl = pl.ds(pl.multiple_of(j * share + g * 16, 16), 16)
        x = h_next_ref[sl, :]
        ms = jnp.mean(x * x, axis=-1, keepdims=True)
        xn_next_ref[sl, :] = (x * lax.rsqrt(ms + EPS) * nw_in).astype(BF16)

    @pl.when(j == n_j - 1)
    def _():
        _rms_residual_rows(acc_ref, h_ref, nw_out_ref, o_ref, tm)


def _ffn(h, nw_in, wup, wdn, nw_out, layer, *, tm=512, th=1024):
    t, d = h.shape
    hidden = wup.shape[2]
    n_i, n_j = t // tm, hidden // th
    return pl.pallas_call(
        functools.partial(_ffn_kernel, tm=tm, n_j=n_j),
        grid=(n_i, n_j),
        in_specs=[
            pl.BlockSpec((tm, d), lambda i, j: (i, 0)),
            pl.BlockSpec((tm, d), lambda i, j: (jnp.minimum(i + 1, n_i - 1), 0)),
            pl.BlockSpec((1, d), lambda i, j: (0, 0)),
            pl.BlockSpec((None, d, th), lambda i, j: (layer, 0, j)),
            pl.BlockSpec((None, th, d), lambda i, j: (layer, j, 0)),
            pl.BlockSpec((1, d), lambda i, j: (0, 0)),
        ],
        out_specs=pl.BlockSpec((tm, d), lambda i, j: (i, 0)),
        out_shape=jax.ShapeDtypeStruct((t, d), F32),
        scratch_shapes=[pltpu.VMEM((2, tm, d), BF16), pltpu.VMEM((tm, d), F32)],
        compiler_params=_params(2, 56),
        name="ffn",
    )(h, h, nw_in, wup, wdn, nw_out)


def _cast_specs(arrays, n_steps, step_index):
    in_specs, out_specs, out_shapes = [], [], []
    for a in arrays:
        rows, cols = a.shape[0] // n_steps, a.shape[1]
        in_specs.append(pl.BlockSpec((rows, cols), lambda b, i: (step_index(b, i), 0)))
        out_specs.append(pl.BlockSpec((rows, cols), lambda b, i: (step_index(b, i), 0)))
        out_shapes.append(jax.ShapeDtypeStruct(a.shape, BF16))
    return in_specs, out_specs, out_shapes


def _cast_slabs(in_refs, out_refs):
    for in_ref, out_ref in zip(in_refs, out_refs):
        out_ref[...] = in_ref[...].astype(out_ref.dtype)


def _pair_masks():
    row = lax.broadcasted_iota(jnp.int32, (PAIR, PAIR), 0)
    col = lax.broadcasted_iota(jnp.int32, (PAIR, PAIR), 1)
    same_chunk = (row >> 6) == (col >> 6)
    return row, col, same_chunk


def _chunk_halves(parts, a):
    zeros = jnp.zeros_like(parts)
    return jnp.concatenate([parts, zeros] if a == 0 else [zeros, parts], axis=0)


def _gdn_kernel(*refs, tl, n_cast):
    (q_ref, k_ref, v_ref, z_ref, gate_ref, cw_ref, gp_ref, onw_ref), rest = refs[:8], refs[8:]
    cast_in, o_ref, cast_out, (xpad_ref, s_ref) = rest[:n_cast], rest[n_cast], rest[n_cast + 1:2 * n_cast + 1], rest[2 * n_cast + 1:]
    _cast_slabs(cast_in, cast_out)
    _gdn_body(q_ref, k_ref, v_ref, z_ref, gate_ref, cw_ref, gp_ref, onw_ref, o_ref, xpad_ref, s_ref, tl=tl)


def _gdn_body(q_ref, k_ref, v_ref, z_ref, gate_ref, cw_ref, gp_ref, onw_ref, o_ref,
              xpad_ref, s_ref, *, tl):
    nh = N_HEADS
    heads = range(nh)

    @pl.when(pl.program_id(1) == 0)
    def _():
        xpad_ref[:, 0:HALO, :] = jnp.zeros((3, HALO, QK_WIDTH), F32)
        s_ref[...] = jnp.zeros_like(s_ref)

    xpad_ref[0, HALO:HALO + tl, :] = q_ref[...].astype(F32)
    xpad_ref[1, HALO:HALO + tl, :] = k_ref[...].astype(F32)
    xpad_ref[2, HALO:HALO + tl, :] = v_ref[...].astype(F32)

    row, col, same_chunk = _pair_masks()
    causal = same_chunk & (col <= row)
    strict = same_chunk & (col < row)
    eye = (row == col).astype(F32)
    row_in_chunk = row & (LA_CHUNK - 1)
    lane8 = lax.broadcasted_iota(jnp.int32, (nh, PAIR), 1)

    a_coef = -jnp.exp(gp_ref[0:1, :])
    dt_bias = gp_ref[1:2, :]
    onw = onw_ref[...]
    s = [s_ref[h] for h in heads]

    for p in range(tl // PAIR):
        r0 = p * PAIR
        rows = slice(r0, r0 + PAIR)

        pre = gate_ref[rows, :]
        beta_all = jax.nn.sigmoid(pre)
        g = a_coef * _softplus(pre + dt_bias)
        for sh in (1, 2, 4, 8, 16, 32):
            g = g + jnp.where(row_in_chunk >= sh, pltpu.roll(g, sh, axis=0), 0.0)
        gt8 = g.T[nh:2 * nh, :]
        bt8 = beta_all.T[0:nh, :]
        g_last = [jnp.sum(jnp.where(lane8 == (a + 1) * LA_CHUNK - 1, gt8, 0.0), axis=1, keepdims=True)
                  for a in range(2)]
        k_tail8 = jnp.exp(jnp.where(lane8 < LA_CHUNK, g_last[0], g_last[1]) - gt8)
        be8 = bt8 * jnp.exp(gt8)
        chunk_decay = [jnp.exp(gl) for gl in g_last]

        def conv(ci, cs):
            w = cw_ref[ci, :, cs]
            acc = xpad_ref[ci, HALO + r0:HALO + r0 + PAIR, cs] * w[CONV_WIDTH - 1:CONV_WIDTH, :]
            for j in range(1, CONV_WIDTH):
                acc = acc + xpad_ref[ci, HALO + r0 - j:HALO + r0 - j + PAIR, cs] * w[CONV_WIDTH - 1 - j:CONV_WIDTH - j, :]
            return _silu(acc)

        q_l, k_l, v_l, kbf_l, decay_l, m1_l, lhs_l = [], [], [], [], [], [], []
        for h in heads:
            cs = slice(h * HEAD_DIM, (h + 1) * HEAD_DIM)
            q = conv(0, cs)
            k = conv(1, cs)
            v = conv(2, cs)
            q = q * lax.rsqrt(jnp.sum(q * q, axis=-1, keepdims=True) + 1e-6) * (HEAD_DIM ** -0.5)
            k = k * lax.rsqrt(jnp.sum(k * k, axis=-1, keepdims=True) + 1e-6)
            gc_col = jnp.sum(jnp.where(col == nh + h, g, 0.0), axis=1, keepdims=True)
            beta_col = jnp.sum(jnp.where(col == h, beta_all, 0.0), axis=1, keepdims=True)
            decay = jnp.exp(jnp.where(causal, gc_col - gt8[h:h + 1, :], -jnp.inf))
            q_l.append(q * jnp.exp(gc_col))
            k_l.append(k)
            v_l.append(v.astype(BF16))
            kbf = k.astype(BF16)
            kbf_l.append(kbf)
            decay_l.append(decay)
            m1_l.append(beta_col * decay)
            lhs_l.append(jnp.concatenate([q.astype(BF16), kbf], axis=0))

        r_l = [_dot_nt(lhs_l[h], kbf_l[h]) for h in heads]
        qk_l = [(r_l[h][0:PAIR] * decay_l[h]).astype(BF16) for h in heads]
        a_l = [jnp.where(strict, r_l[h][PAIR:2 * PAIR] * m1_l[h], 0.0) for h in heads]
        x_l = [eye - a_l[h] for h in heads]
        pb_l = [a_l[h].astype(BF16) for h in heads]
        for _ in range(5):
            pb_l = [_dot(pb_l[h], pb_l[h]).astype(BF16) for h in heads]
            x_l = [x_l[h] + _dot(x_l[h].astype(BF16), pb_l[h]) for h in heads]

        u_l = [_dot((x_l[h] * bt8[h:h + 1, :]).astype(BF16), v_l[h]) for h in heads]
        w_l = [_dot((x_l[h] * be8[h:h + 1, :]).astype(BF16), kbf_l[h]) for h in heads]
        kt_l = [(k_l[h].T * k_tail8[h:h + 1, :]).astype(BF16) for h in heads]

        v_new_l = [[], []]
        qs_l = [[], []]
        for a in range(2):
            ra = slice(a * LA_CHUNK, (a + 1) * LA_CHUNK)
            r2_l = [_dot(jnp.concatenate([w_l[h][ra], q_l[h][ra]], axis=0).astype(BF16), s[h].astype(BF16))
                    for h in heads]
            for h in heads:
                v_new = u_l[h][ra] - r2_l[h][0:LA_CHUNK]
                v_new_l[a].append(v_new)
                qs_l[a].append(r2_l[h][LA_CHUNK:2 * LA_CHUNK])
            upd_l = [_dot(kt_l[h], _chunk_halves(v_new_l[a][h], a).astype(BF16)) for h in heads]
            s = [s[h] * chunk_decay[a][h:h + 1, :] + upd_l[h] for h in heads]

        o_l = [jnp.concatenate([qs_l[0][h], qs_l[1][h]], axis=0)
               + _dot(qk_l[h], jnp.concatenate([v_new_l[0][h], v_new_l[1][h]], axis=0).astype(BF16))
               for h in heads]
        for h in heads:
            cs = slice(h * HEAD_DIM, (h + 1) * HEAD_DIM)
            o = o_l[h]
            o = o * lax.rsqrt(jnp.mean(o * o, axis=-1, keepdims=True) + EPS) * onw * _silu(z_ref[rows, cs].astype(F32))
            o_ref[rows, cs] = o.astype(o_ref.dtype)

    for h in heads:
        s_ref[h] = s[h]
    xpad_ref[:, 0:HALO, :] = xpad_ref[:, tl:tl + HALO, :]


def _gdn(proj, gates, cw, gp, onw, cast_arrays, *, batch, seq, tl=256):
    nblk = seq // tl
    cast_in_specs, cast_out_specs, cast_shapes = _cast_specs(cast_arrays, batch * nblk, lambda b, i: b * nblk + i)

    def seg(s_idx):
        return pl.BlockSpec((tl, QK_WIDTH), lambda b, i: (b * nblk + i, s_idx))

    return pl.pallas_call(
        functools.partial(_gdn_kernel, tl=tl, n_cast=len(cast_arrays)),
        grid=(batch, nblk),
        in_specs=[
            seg(0), seg(1), seg(2), seg(3),
            pl.BlockSpec((tl, LANES), lambda b, i: (b * nblk + i, 0)),
            pl.BlockSpec((3, CONV_WIDTH, QK_WIDTH), lambda b, i: (0, 0, 0)),
            pl.BlockSpec((8, LANES), lambda b, i: (0, 0)),
            pl.BlockSpec((1, HEAD_DIM), lambda b, i: (0, 0)),
        ] + cast_in_specs,
        out_specs=[pl.BlockSpec((tl, QK_WIDTH), lambda b, i: (b * nblk + i, 0))] + cast_out_specs,
        out_shape=[jax.ShapeDtypeStruct((batch * seq, QK_WIDTH), BF16)] + cast_shapes,
        scratch_shapes=[pltpu.VMEM((3, HALO + tl, QK_WIDTH), F32), pltpu.VMEM((N_HEADS, HEAD_DIM, HEAD_DIM), F32)],
        compiler_params=_params(2, 48),
        name="gdn",
    )(proj, proj, proj, proj, gates, cw, gp, onw, *cast_arrays)


def _ret_kernel(*refs, tl, n_cast):
    main_in, rest = refs[:14], refs[14:]
    cast_in, o_ref, cast_out, (s_ref,) = rest[:n_cast], rest[n_cast], rest[n_cast + 1:2 * n_cast + 1], rest[2 * n_cast + 1:]
    _cast_slabs(cast_in, cast_out)
    _ret_body(*main_in, o_ref, s_ref, tl=tl)


def _ret_body(q_ref, k_ref, v_ref, g_ref, cos_a_ref, sin_a_ref, cos_b_ref, sin_b_ref, cos_bs_ref, sin_bs_ref,
              dmat_ref, qs_ref, ks_ref, cd_ref, o_ref, s_ref, *, tl):
    heads = range(N_HEADS)
    blk = pl.ds(pl.program_id(1), 1)
    cos_a = cos_a_ref[blk, :]
    sin_a = sin_a_ref[blk, :]

    @pl.when(pl.program_id(1) == 0)
    def _():
        s_ref[...] = jnp.zeros_like(s_ref)

    s = [s_ref[h] for h in heads]

    for p in range(tl // PAIR):
        rows = slice(p * PAIR, (p + 1) * PAIR)
        cosf = cos_a * cos_b_ref[rows, :] - sin_a * sin_b_ref[rows, :]
        sinf = sin_a * cos_bs_ref[rows, :] + cos_a * sin_bs_ref[rows, :]

        q_l, kb_l, kdt_l, v_l = [], [], [], []
        for h in heads:
            cs = slice(h * HEAD_DIM, (h + 1) * HEAD_DIM)
            q = q_ref[rows, cs].astype(F32)
            k = k_ref[rows, cs].astype(F32)
            q = q * cosf + pltpu.roll(q, HEAD_DIM // 2, axis=1) * sinf
            k = (k * cosf + pltpu.roll(k, HEAD_DIM // 2, axis=1) * sinf) * (HEAD_DIM ** -0.5)
            q_l.append(q)
            kb_l.append(k.astype(BF16))
            kdt_l.append((k * ks_ref[h]).T.astype(BF16))
            v_l.append(v_ref[rows, cs])

        qk_l = [(_dot_nt(q_l[h].astype(BF16), kb_l[h]) * dmat_ref[h]).astype(BF16) for h in heads]
        inner_l = [_dot(qk_l[h], v_l[h].astype(BF16)) for h in heads]
        qd_l = [(q_l[h] * qs_ref[h]).astype(BF16) for h in heads]

        cross_l = [[], []]
        for a in range(2):
            ra = slice(a * LA_CHUNK, (a + 1) * LA_CHUNK)
            cross_l[a] = [_dot(qd_l[h][ra], s[h].astype(BF16)) for h in heads]
            upd_l = [_dot(kdt_l[h], _chunk_halves(v_l[h][ra], a).astype(BF16)) for h in heads]
            s = [s[h] * cd_ref[h] + upd_l[h] for h in heads]

        for h in heads:
            cs = slice(h * HEAD_DIM, (h + 1) * HEAD_DIM)
            o = inner_l[h] + jnp.concatenate([cross_l[0][h], cross_l[1][h]], axis=0)
            o = o * lax.rsqrt(jnp.mean(o * o, axis=-1, keepdims=True) + EPS)
            o_ref[rows, cs] = (_silu(g_ref[rows, cs].astype(F32)) * o).astype(o_ref.dtype)

    for h in heads:
        s_ref[h] = s[h]


def _ret(proj, cast_arrays, *, batch, seq, tl=256):
    nblk = seq // tl
    tables = _retention_tables(seq, tl)
    cast_in_specs, cast_out_specs, cast_shapes = _cast_specs(cast_arrays, batch * nblk, lambda b, i: b * nblk + i)

    def seg(s_idx):
        return pl.BlockSpec((tl, QK_WIDTH), lambda b, i: (b * nblk + i, s_idx))

    def table():
        return pl.BlockSpec((N_HEADS, PAIR, HEAD_DIM), lambda b, i: (0, 0, 0))

    return pl.pallas_call(
        functools.partial(_ret_kernel, tl=tl, n_cast=len(cast_arrays)),
        grid=(batch, nblk),
        in_specs=[
            seg(4), seg(5), seg(6), seg(7),
            pl.BlockSpec((nblk, HEAD_DIM), lambda b, i: (0, 0)),
            pl.BlockSpec((nblk, HEAD_DIM), lambda b, i: (0, 0)),
            pl.BlockSpec((tl, HEAD_DIM), lambda b, i: (0, 0)),
            pl.BlockSpec((tl, HEAD_DIM), lambda b, i: (0, 0)),
            pl.BlockSpec((tl, HEAD_DIM), lambda b, i: (0, 0)),
            pl.BlockSpec((tl, HEAD_DIM), lambda b, i: (0, 0)),
            table(), table(), table(), table(),
        ] + cast_in_specs,
        out_specs=[pl.BlockSpec((tl, QK_WIDTH), lambda b, i: (b * nblk + i, 0))] + cast_out_specs,
        out_shape=[jax.ShapeDtypeStruct((batch * seq, QK_WIDTH), BF16)] + cast_shapes,
        scratch_shapes=[pltpu.VMEM((N_HEADS, HEAD_DIM, HEAD_DIM), F32)],
        compiler_params=_params(2, 32),
        name="retention",
    )(proj, proj, proj, proj, *tables, *cast_arrays)


def _sg_gate_kernel(u_ref, v_ref, lnw_ref, lnb_ref, ws_ref, bs_ref, o_ref, vn_ref, *, tm):
    lnw = lnw_ref[...]
    lnb = lnb_ref[...]
    step = 16

    def body(r, carry):
        sl = pl.ds(pl.multiple_of(r * step, step), step)
        x = v_ref[sl, :].astype(F32)
        mu = jnp.mean(x, axis=-1, keepdims=True)
        xc = x - mu
        var = jnp.mean(xc * xc, axis=-1, keepdims=True)
        vn_ref[sl, :] = (xc * lax.rsqrt(var + EPS) * lnw + lnb).astype(vn_ref.dtype)
        return carry

    lax.fori_loop(0, tm // step, body, 0, unroll=ROW_LOOP_UNROLL)

    row = lax.broadcasted_iota(jnp.int32, (SG_CHUNK, SG_CHUNK), 0)
    col = lax.broadcasted_iota(jnp.int32, (SG_CHUNK, SG_CHUNK), 1)
    for g in range(SG_GROUPS):
        wg = jnp.where(col <= row, ws_ref[g], 0.0).astype(BF16)
        bias = bs_ref[:, g:g + 1]
        cols = slice(g * SG_GROUP_DIM, (g + 1) * SG_GROUP_DIM)
        for c in range(tm // SG_CHUNK):
            rows = slice(c * SG_CHUNK, (c + 1) * SG_CHUNK)
            sgate = _dot(wg, vn_ref[rows, cols]) + bias
            o_ref[rows, cols] = (u_ref[rows, cols].astype(F32) * sgate).astype(o_ref.dtype)


def _sg_gate(proj, lnw, lnb, ws, bs_t, *, tm=512):
    t = proj.shape[0]
    return pl.pallas_call(
        functools.partial(_sg_gate_kernel, tm=tm),
        grid=(t // tm,),
        in_specs=[
            pl.BlockSpec((tm, SG_WIDTH), lambda i: (i, 0)),
            pl.BlockSpec((tm, SG_WIDTH), lambda i: (i, 1)),
            pl.BlockSpec((1, SG_WIDTH), lambda i: (0, 0)),
            pl.BlockSpec((1, SG_WIDTH), lambda i: (0, 0)),
            pl.BlockSpec((SG_GROUPS, SG_CHUNK, SG_CHUNK), lambda i: (0, 0, 0)),
            pl.BlockSpec((SG_CHUNK, SG_GROUPS), lambda i: (0, 0)),
        ],
        out_specs=pl.BlockSpec((tm, SG_WIDTH), lambda i: (i, 0)),
        out_shape=jax.ShapeDtypeStruct((t, SG_WIDTH), BF16),
        scratch_shapes=[pltpu.VMEM((tm, SG_WIDTH), BF16)],
        compiler_params=_params(1, 40),
        name="sg_gate",
    )(proj, proj, lnw, lnb, ws, bs_t)


def _retention_tables(seq, tl):
    half = HEAD_DIM // 2
    inv_freq = 1.0 / (ROPE_BASE ** jnp.linspace(0.0, 1.0, half, dtype=F32))
    base = (jnp.arange(seq // tl, dtype=F32) * tl)[:, None] * inv_freq[None, :]
    off = jnp.arange(tl, dtype=F32)[:, None] * inv_freq[None, :]

    def both_halves(a):
        return jnp.concatenate([a, a], axis=1)

    sign = jnp.concatenate([-jnp.ones((half,), F32), jnp.ones((half,), F32)])[None, :]
    cos_a, sin_a = both_halves(jnp.cos(base)), both_halves(jnp.sin(base))
    cos_b, sin_b = both_halves(jnp.cos(off)), both_halves(jnp.sin(off))

    log_gamma = jnp.log1p(-jnp.power(2.0, -5.0 - jnp.arange(N_HEADS, dtype=F32)))
    t = jnp.arange(PAIR)
    pos = (t % LA_CHUNK).astype(F32)
    mask = ((t[:, None] // LA_CHUNK) == (t[None, :] // LA_CHUNK)) & (t[None, :] <= t[:, None])
    dmat = jnp.exp(jnp.where(mask[None], (pos[:, None] - pos[None, :])[None] * log_gamma[:, None, None], -jnp.inf))
    full = (N_HEADS, PAIR, HEAD_DIM)
    q_scale = jnp.broadcast_to(jnp.exp((pos[None, :] + 1.0) * log_gamma[:, None])[:, :, None], full)
    k_scale = jnp.broadcast_to(jnp.exp((LA_CHUNK - 1.0 - pos[None, :]) * log_gamma[:, None])[:, :, None], full)
    chunk_decay = jnp.broadcast_to(jnp.exp(LA_CHUNK * log_gamma)[:, None, None], full)
    return cos_a, sin_a, cos_b, sin_b, cos_b * sign, sin_b * sign, dmat, q_scale, k_scale, chunk_decay


def kernel(x, norm_w, la_w_in, la_conv_w, la_a_log, la_dt_bias, la_out_norm_w, la_w_out, sg_w_in, sg_ln_w,
           sg_ln_b, sg_w_s, sg_b_s, sg_w_out, ffn_w_up, ffn_w_down):
    batch, seq, d = x.shape
    t = batch * seq
    h = x.reshape(t, d)
    gate0 = 4 * QK_WIDTH

    w_t = jnp.swapaxes(la_w_in[0], 0, 1)
    w_parts = jnp.stack([w_t[:gate0], w_t[gate0 + 2 * N_HEADS:]]).astype(BF16)
    w_gate = jnp.pad(w_t[gate0:gate0 + 2 * N_HEADS], ((0, LANES - 2 * N_HEADS), (0, 0))).astype(BF16)
    proj, gates = _inproj_la(h, norm_w[0, 0][None, :], w_parts, w_gate)

    cw = la_conv_w[0].reshape(3, QK_WIDTH, CONV_WIDTH).transpose(0, 2, 1)
    gp = jnp.zeros((8, LANES), F32)
    gp = gp.at[0, N_HEADS:2 * N_HEADS].set(la_a_log[0]).at[1, N_HEADS:2 * N_HEADS].set(la_dt_bias[0])
    n_layers, _, hidden = ffn_w_up.shape
    o_a, w_up, w_down = _gdn(proj, gates, cw, gp, la_out_norm_w[0][None, :],
                             [ffn_w_up.reshape(n_layers * d, hidden), ffn_w_down.reshape(n_layers * hidden, d)],
                             batch=batch, seq=seq)
    w_up = w_up.reshape(n_layers, d, hidden)
    w_down = w_down.reshape(n_layers, hidden, d)
    o_b, w_out, w_sg_in, w_sg_out = _ret(proj, [la_w_out[0], sg_w_in[0], sg_w_out[0]], batch=batch, seq=seq)

    h = _out_rms_res([o_a, o_b], [w_out, w_out], h, norm_w[0, 1][None, :], tm=512, vmem_mib=48)
    h = _ffn(h, norm_w[0, 2][None, :], w_up, w_down, norm_w[0, 3][None, :], 0)

    proj1 = _inproj_gelu(h, norm_w[1, 0][None, :], w_sg_in)
    gated = _sg_gate(proj1, sg_ln_w[0][None, :], sg_ln_b[0][None, :], sg_w_s[0], sg_b_s[0].T)
    h = _out_rms_res([gated], [w_sg_out], h, norm_w[1, 1][None, :], tm=512, vmem_mib=56)
    h = _ffn(h, norm_w[1, 2][None, :], w_up, w_down, norm_w[1, 3][None, :], 1)
    return h.reshape(batch, seq, d)
```

```python
import functools
import math

import jax
import jax.numpy as jnp
from jax import lax
from jax.experimental import pallas as pl
from jax.experimental.pallas import tpu as pltpu

F32 = jnp.float32
BF16 = jnp.bfloat16

D_MODEL = 2048
N_HEADS = 8
HEAD_DIM = 128
QK_WIDTH = N_HEADS * HEAD_DIM
LA_CHUNK = 64
PAIR = 2 * LA_CHUNK
CONV_WIDTH = 4
HALO = 8
ROPE_BASE = 10000.0
SG_CHUNK = 128
SG_GROUPS = 8
SG_WIDTH = 2 * D_MODEL
SG_GROUP_DIM = SG_WIDTH // SG_GROUPS
FFN_HIDDEN = 4 * D_MODEL
EPS = 1e-6
LANES = 128
ROW_LOOP_UNROLL = 8
MIB = 1024 * 1024

NT_DIMS = (((1,), (1,)), ((), ()))


def _params(n_grid_axes, vmem_mib):
    return pltpu.CompilerParams(dimension_semantics=("arbitrary",) * n_grid_axes,
                                vmem_limit_bytes=vmem_mib * MIB)


def _dot(a, b):
    return jnp.dot(a, b, preferred_element_type=F32)


def _dot_nt(a, b):
    return lax.dot_general(a, b, NT_DIMS, preferred_element_type=F32)


def _silu(x):
    return x * jax.nn.sigmoid(x)


def _softplus(x):
    return jnp.maximum(x, 0.0) + jnp.log1p(jnp.exp(-jnp.abs(x)))


def _gelu_tanh(x):
    c = math.sqrt(2.0 / math.pi)
    return x * (0.5 * (1.0 + jnp.tanh(c * (x + 0.044715 * (x * x * x)))))


def _rmsnorm_rows_to(src_ref, nw_ref, dst_ref, rows):
    nw = nw_ref[...]
    step = 16

    def body(r, carry):
        sl = pl.ds(pl.multiple_of(r * step, step), step)
        x = src_ref[sl, :]
        ms = jnp.mean(x * x, axis=-1, keepdims=True)
        dst_ref[sl, :] = (x * lax.rsqrt(ms + EPS) * nw).astype(dst_ref.dtype)
        return carry

    lax.fori_loop(0, rows // step, body, 0, unroll=ROW_LOOP_UNROLL)


def _rms_residual_rows(acc_ref, h_ref, nw_ref, o_ref, rows):
    nw = nw_ref[...]
    step = 16

    def body(r, carry):
        sl = pl.ds(pl.multiple_of(r * step, step), step)
        y = acc_ref[sl, :]
        ms = jnp.mean(y * y, axis=-1, keepdims=True)
        o_ref[sl, :] = h_ref[sl, :] + y * lax.rsqrt(ms + EPS) * nw
        return carry

    lax.fori_loop(0, rows // step, body, 0, unroll=ROW_LOOP_UNROLL)


def _column_chunks(width, chunk=1024):
    return [slice(c, c + chunk) for c in range(0, width, chunk)]


def _inproj_la_kernel(x_ref, nw_ref, w_ref, wg_ref, o_ref, g_ref, xn_ref, *, tm):
    @pl.when(pl.program_id(1) == 0)
    def _():
        _rmsnorm_rows_to(x_ref, nw_ref, xn_ref, tm)
        g_ref[...] = _dot_nt(xn_ref[...], wg_ref[...])

    for cols in _column_chunks(o_ref.shape[1]):
        o_ref[:, cols] = _dot_nt(xn_ref[...], w_ref[cols, :]).astype(o_ref.dtype)


def _inproj_la(x, nw, w_t, *, gate0, n_gate, tm=1024, tn=2048):
    t, k = x.shape
    n = w_t.shape[0] - n_gate
    per_part = gate0 // tn

    def w_rows(i, j):
        return (pl.multiple_of(j * tn + jnp.where(j >= per_part, n_gate, 0), n_gate), 0)

    return pl.pallas_call(
        functools.partial(_inproj_la_kernel, tm=tm),
        grid=(t // tm, n // tn),
        in_specs=[
            pl.BlockSpec((tm, k), lambda i, j: (i, 0)),
            pl.BlockSpec((1, k), lambda i, j: (0, 0)),
            pl.BlockSpec((pl.Element(tn), pl.Element(k)), w_rows),
            pl.BlockSpec((LANES, k), lambda i, j: (gate0 // LANES, 0)),
        ],
        out_specs=[
            pl.BlockSpec((tm, tn), lambda i, j: (i, j)),
            pl.BlockSpec((tm, LANES), lambda i, j: (i, 0)),
        ],
        out_shape=[jax.ShapeDtypeStruct((t, n), BF16), jax.ShapeDtypeStruct((t, LANES), F32)],
        scratch_shapes=[pltpu.VMEM((tm, k), BF16)],
        compiler_params=_params(2, 56),
        name="inproj_la",
    )(x, nw, w_t, w_t)


def _inproj_gelu_kernel(x_ref, nw_ref, w_ref, o_ref, xn_ref, *, tm):
    @pl.when(pl.program_id(1) == 0)
    def _():
        _rmsnorm_rows_to(x_ref, nw_ref, xn_ref, tm)

    for cols in _column_chunks(o_ref.shape[1]):
        o_ref[:, cols] = _gelu_tanh(_dot(xn_ref[...], w_ref[:, cols])).astype(o_ref.dtype)


def _inproj_gelu(x, nw, w, *, tm=1024, tn=2048):
    t, k = x.shape
    n = w.shape[1]
    return pl.pallas_call(
        functools.partial(_inproj_gelu_kernel, tm=tm),
        grid=(t // tm, n // tn),
        in_specs=[
            pl.BlockSpec((tm, k), lambda i, j: (i, 0)),
            pl.BlockSpec((1, k), lambda i, j: (0, 0)),
            pl.BlockSpec((k, tn), lambda i, j: (0, j)),
        ],
        out_specs=pl.BlockSpec((tm, tn), lambda i, j: (i, j)),
        out_shape=jax.ShapeDtypeStruct((t, n), BF16),
        scratch_shapes=[pltpu.VMEM((tm, k), BF16)],
        compiler_params=_params(2, 56),
        name="inproj_gelu",
    )(x, nw, w)


def _out_rms_res_kernel(*refs, n_pairs, tm):
    a_refs = refs[:n_pairs]
    w_refs = refs[n_pairs:2 * n_pairs]
    h_ref, nw_ref, o_ref, acc_ref = refs[2 * n_pairs:]
    y = _dot(a_refs[0][...], w_refs[0][...])
    for a_ref, w_ref in zip(a_refs[1:], w_refs[1:]):
        y = y + _dot(a_ref[...], w_ref[...])
    acc_ref[...] = y
    _rms_residual_rows(acc_ref, h_ref, nw_ref, o_ref, tm)


def _out_rms_res(a_list, w_list, h, nw, *, tm, vmem_mib):
    t, n = h.shape
    n_pairs = len(a_list)
    in_specs = [pl.BlockSpec((tm, a.shape[1]), lambda i: (i, 0)) for a in a_list]
    k0 = 0
    for a, w in zip(a_list, w_list):
        ka = a.shape[1]
        row_block = k0 // ka if w.shape[0] != ka else 0
        in_specs.append(pl.BlockSpec((ka, n), lambda i, rb=row_block: (rb, 0)))
        k0 += ka
    in_specs += [pl.BlockSpec((tm, n), lambda i: (i, 0)), pl.BlockSpec((1, n), lambda i: (0, 0))]
    return pl.pallas_call(
        functools.partial(_out_rms_res_kernel, n_pairs=n_pairs, tm=tm),
        grid=(t // tm,),
        in_specs=in_specs,
        out_specs=pl.BlockSpec((tm, n), lambda i: (i, 0)),
        out_shape=jax.ShapeDtypeStruct((t, n), F32),
        scratch_shapes=[pltpu.VMEM((tm, n), F32)],
        compiler_params=_params(1, vmem_mib),
        name="out_rms_res",
    )(*a_list, *w_list, h, nw)


def _ffn_kernel(h_ref, h_next_ref, nw_in_ref, wup_ref, wdn_ref, nw_out_ref, o_ref, xn_ref, acc_ref, *, tm, n_j):
    i = pl.program_id(0)
    j = pl.program_id(1)
    slot = lax.rem(i, 2)

    @pl.when((i == 0) & (j == 0))
    def _():
        _rmsnorm_rows_to(h_ref, nw_in_ref, xn_ref.at[0], tm)

    @pl.when(j == 0)
    def _():
        acc_ref[...] = jnp.zeros_like(acc_ref)

    u = _dot(xn_ref[slot], wup_ref[...])
    u = jnp.square(jnp.maximum(u, 0.0)).astype(BF16)
    acc_ref[...] += _dot(u, wdn_ref[...])

    nw_in = nw_in_ref[...]
    share = tm // n_j
    xn_next_ref = xn_ref.at[1 - slot]
    for g in range(share // 16):
        sl = pl.ds(pl.multiple_of(j * share + g * 16, 16), 16)
        x = h_next_ref[sl, :]
        ms = jnp.mean(x * x, axis=-1, keepdims=True)
        xn_next_ref[sl, :] = (x * lax.rsqrt(ms + EPS) * nw_in).astype(BF16)

    @pl.when(j == n_j - 1)
    def _():
        _rms_residual_rows(acc_ref, h_ref, nw_out_ref, o_ref, tm)


def _ffn(h, nw_in, wup, wdn, nw_out, layer, *, tm=512, th=1024):
    t, d = h.shape
    hidden = wup.shape[2]
    n_i, n_j = t // tm, hidden // th
    return pl.pallas_call(
        functools.partial(_ffn_kernel, tm=tm, n_j=n_j),
        grid=(n_i, n_j),
        in_specs=[
            pl.BlockSpec((tm, d), lambda i, j: (i, 0)),
            pl.BlockSpec((tm, d), lambda i, j: (jnp.minimum(i + 1, n_i - 1), 0)),
            pl.BlockSpec((1, d), lambda i, j: (0, 0)),
            pl.BlockSpec((None, d, th), lambda i, j: (layer, 0, j)),
            pl.BlockSpec((None, th, d), lambda i, j: (layer, j, 0)),
            pl.BlockSpec((1, d), lambda i, j: (0, 0)),
        ],
        out_specs=pl.BlockSpec((tm, d), lambda i, j: (i, 0)),
        out_shape=jax.ShapeDtypeStruct((t, d), F32),
        scratch_shapes=[pltpu.VMEM((2, tm, d), BF16), pltpu.VMEM((tm, d), F32)],
        compiler_params=_params(2, 56),
        name="ffn",
    )(h, h, nw_in, wup, wdn, nw_out)


def _cast_specs(arrays, n_steps, step_index):
    in_specs, out_specs, out_shapes = [], [], []
    for a in arrays:
        rows, cols = a.shape[0] // n_steps, a.shape[1]
        in_specs.append(pl.BlockSpec((rows, cols), lambda b, i: (step_index(b, i), 0)))
        out_specs.append(pl.BlockSpec((rows, cols), lambda b, i: (step_index(b, i), 0)))
        out_shapes.append(jax.ShapeDtypeStruct(a.shape, BF16))
    return in_specs, out_specs, out_shapes


def _cast_slabs(in_refs, out_refs):
    for in_ref, out_ref in zip(in_refs, out_refs):
        out_ref[...] = in_ref[...].astype(out_ref.dtype)


def _pair_masks():
    row = lax.broadcasted_iota(jnp.int32, (PAIR, PAIR), 0)
    col = lax.broadcasted_iota(jnp.int32, (PAIR, PAIR), 1)
    same_chunk = (row >> 6) == (col >> 6)
    return row, col, same_chunk


def _chunk_halves(parts, a):
    zeros = jnp.zeros_like(parts)
    return jnp.concatenate([parts, zeros] if a == 0 else [zeros, parts], axis=0)


def _gdn_kernel(*refs, tl, n_cast):
    (q_ref, k_ref, v_ref, z_ref, gate_ref, cw_ref, gp_ref, onw_ref), rest = refs[:8], refs[8:]
    cast_in, o_ref, cast_out, (xpad_ref, s_ref) = rest[:n_cast], rest[n_cast], rest[n_cast + 1:2 * n_cast + 1], rest[2 * n_cast + 1:]
    _cast_slabs(cast_in, cast_out)
    _gdn_body(q_ref, k_ref, v_ref, z_ref, gate_ref, cw_ref, gp_ref, onw_ref, o_ref, xpad_ref, s_ref, tl=tl)


def _gdn_body(q_ref, k_ref, v_ref, z_ref, gate_ref, cw_ref, gp_ref, onw_ref, o_ref,
              xpad_ref, s_ref, *, tl):
    nh = N_HEADS
    heads = range(nh)

    @pl.when(pl.program_id(1) == 0)
    def _():
        xpad_ref[:, 0:HALO, :] = jnp.zeros((3, HALO, QK_WIDTH), F32)
        s_ref[...] = jnp.zeros_like(s_ref)

    xpad_ref[0, HALO:HALO + tl, :] = q_ref[...].astype(F32)
    xpad_ref[1, HALO:HALO + tl, :] = k_ref[...].astype(F32)
    xpad_ref[2, HALO:HALO + tl, :] = v_ref[...].astype(F32)

    row, col, same_chunk = _pair_masks()
    causal = same_chunk & (col <= row)
    strict = same_chunk & (col < row)
    eye = (row == col).astype(F32)
    row_in_chunk = row & (LA_CHUNK - 1)
    lane8 = lax.broadcasted_iota(jnp.int32, (nh, PAIR), 1)

    a_coef = -jnp.exp(gp_ref[0:1, :])
    dt_bias = gp_ref[1:2, :]
    onw = onw_ref[...]
    s = [s_ref[h] for h in heads]

    for p in range(tl // PAIR):
        r0 = p * PAIR
        rows = slice(r0, r0 + PAIR)

        pre = gate_ref[rows, :]
        beta_all = jax.nn.sigmoid(pre)
        g = a_coef * _softplus(pre + dt_bias)
        for sh in (1, 2, 4, 8, 16, 32):
            g = g + jnp.where(row_in_chunk >= sh, pltpu.roll(g, sh, axis=0), 0.0)
        gt8 = g.T[nh:2 * nh, :]
        bt8 = beta_all.T[0:nh, :]
        g_last = [jnp.sum(jnp.where(lane8 == (a + 1) * LA_CHUNK - 1, gt8, 0.0), axis=1, keepdims=True)
                  for a in range(2)]
        k_tail8 = jnp.exp(jnp.where(lane8 < LA_CHUNK, g_last[0], g_last[1]) - gt8)
        be8 = bt8 * jnp.exp(gt8)
        chunk_decay = [jnp.exp(gl) for gl in g_last]

        def conv(ci, cs):
            w = cw_ref[ci, :, cs]
            acc = xpad_ref[ci, HALO + r0:HALO + r0 + PAIR, cs] * w[CONV_WIDTH - 1:CONV_WIDTH, :]
            for j in range(1, CONV_WIDTH):
                acc = acc + xpad_ref[ci, HALO + r0 - j:HALO + r0 - j + PAIR, cs] * w[CONV_WIDTH - 1 - j:CONV_WIDTH - j, :]
            return _silu(acc)

        q_l, k_l, v_l, kbf_l, decay_l, m1_l, lhs_l = [], [], [], [], [], [], []
        for h in heads:
            cs = slice(h * HEAD_DIM, (h + 1) * HEAD_DIM)
            q = conv(0, cs)
            k = conv(1, cs)
            v = conv(2, cs)
            q = q * lax.rsqrt(jnp.sum(q * q, axis=-1, keepdims=True) + 1e-6) * (HEAD_DIM ** -0.5)
            k = k * lax.rsqrt(jnp.sum(k * k, axis=-1, keepdims=True) + 1e-6)
            gc_col = jnp.sum(jnp.where(col == nh + h, g, 0.0), axis=1, keepdims=True)
            beta_col = jnp.sum(jnp.where(col == h, beta_all, 0.0), axis=1, keepdims=True)
            decay = jnp.exp(jnp.where(causal, gc_col - gt8[h:h + 1, :], -jnp.inf))
            q_l.append(q * jnp.exp(gc_col))
            k_l.append(k)
            v_l.append(v.astype(BF16))
            kbf = k.astype(BF16)
            kbf_l.append(kbf)
            decay_l.append(decay)
            m1_l.append(beta_col * decay)
            lhs_l.append(jnp.concatenate([q.astype(BF16), kbf], axis=0))

        r_l = [_dot_nt(lhs_l[h], kbf_l[h]) for h in heads]
        qk_l = [(r_l[h][0:PAIR] * decay_l[h]).astype(BF16) for h in heads]
        a_l = [jnp.where(strict, r_l[h][PAIR:2 * PAIR] * m1_l[h], 0.0) for h in heads]
        x_l = [eye - a_l[h] for h in heads]
        pb_l = [a_l[h].astype(BF16) for h in heads]
        for _ in range(5):
            pb_l = [_dot(pb_l[h], pb_l[h]).astype(BF16) for h in heads]
            x_l = [x_l[h] + _dot(x_l[h].astype(BF16), pb_l[h]) for h in heads]

        u_l = [_dot((x_l[h] * bt8[h:h + 1, :]).astype(BF16), v_l[h]) for h in heads]
        w_l = [_dot((x_l[h] * be8[h:h + 1, :]).astype(BF16), kbf_l[h]) for h in heads]
        kt_l = [(k_l[h].T * k_tail8[h:h + 1, :]).astype(BF16) for h in heads]

        v_new_l = [[], []]
        qs_l = [[], []]
        for a in range(2):
            ra = slice(a * LA_CHUNK, (a + 1) * LA_CHUNK)
            r2_l = [_dot(jnp.concatenate([w_l[h][ra], q_l[h][ra]], axis=0).astype(BF16), s[h].astype(BF16))
                    for h in heads]
            for h in heads:
                v_new = u_l[h][ra] - r2_l[h][0:LA_CHUNK]
                v_new_l[a].append(v_new)
                qs_l[a].append(r2_l[h][LA_CHUNK:2 * LA_CHUNK])
            upd_l = [_dot(kt_l[h], _chunk_halves(v_new_l[a][h], a).astype(BF16)) for h in heads]
            s = [s[h] * chunk_decay[a][h:h + 1, :] + upd_l[h] for h in heads]

        o_l = [jnp.concatenate([qs_l[0][h], qs_l[1][h]], axis=0)
               + _dot(qk_l[h], jnp.concatenate([v_new_l[0][h], v_new_l[1][h]], axis=0).astype(BF16))
               for h in heads]
        for h in heads:
            cs = slice(h * HEAD_DIM, (h + 1) * HEAD_DIM)
            o = o_l[h]
            o = o * lax.rsqrt(jnp.mean(o * o, axis=-1, keepdims=True) + EPS) * onw * _silu(z_ref[rows, cs].astype(F32))
            o_ref[rows, cs] = o.astype(o_ref.dtype)

    for h in heads:
        s_ref[h] = s[h]
    xpad_ref[:, 0:HALO, :] = xpad_ref[:, tl:tl + HALO, :]


def _gdn(proj, gates, cw, gp, onw, cast_arrays, *, batch, seq, tl=256):
    nblk = seq // tl
    cast_in_specs, cast_out_specs, cast_shapes = _cast_specs(cast_arrays, batch * nblk, lambda b, i: b * nblk + i)

    def seg(s_idx):
        return pl.BlockSpec((tl, QK_WIDTH), lambda b, i: (b * nblk + i, s_idx))

    return pl.pallas_call(
        functools.partial(_gdn_kernel, tl=tl, n_cast=len(cast_arrays)),
        grid=(batch, nblk),
        in_specs=[
            seg(0), seg(1), seg(2), seg(3),
            pl.BlockSpec((tl, LANES), lambda b, i: (b * nblk + i, 0)),
            pl.BlockSpec((3, CONV_WIDTH, QK_WIDTH), lambda b, i: (0, 0, 0)),
            pl.BlockSpec((8, LANES), lambda b, i: (0, 0)),
            pl.BlockSpec((1, HEAD_DIM), lambda b, i: (0, 0)),
        ] + cast_in_specs,
        out_specs=[pl.BlockSpec((tl, QK_WIDTH), lambda b, i: (b * nblk + i, 0))] + cast_out_specs,
        out_shape=[jax.ShapeDtypeStruct((batch * seq, QK_WIDTH), BF16)] + cast_shapes,
        scratch_shapes=[pltpu.VMEM((3, HALO + tl, QK_WIDTH), F32), pltpu.VMEM((N_HEADS, HEAD_DIM, HEAD_DIM), F32)],
        compiler_params=_params(2, 48),
        name="gdn",
    )(proj, proj, proj, proj, gates, cw, gp, onw, *cast_arrays)


def _ret_kernel(*refs, tl, n_cast):
    main_in, rest = refs[:14], refs[14:]
    cast_in, o_ref, cast_out, (s_ref,) = rest[:n_cast], rest[n_cast], rest[n_cast + 1:2 * n_cast + 1], rest[2 * n_cast + 1:]
    _cast_slabs(cast_in, cast_out)
    _ret_body(*main_in, o_ref, s_ref, tl=tl)


def _ret_body(q_ref, k_ref, v_ref, g_ref, cos_a_ref, sin_a_ref, cos_b_ref, sin_b_ref, cos_bs_ref, sin_bs_ref,
              dmat_ref, qs_ref, ks_ref, cd_ref, o_ref, s_ref, *, tl):
    heads = range(N_HEADS)
    blk = pl.ds(pl.program_id(1), 1)
    cos_a = cos_a_ref[blk, :]
    sin_a = sin_a_ref[blk, :]

    @pl.when(pl.program_id(1) == 0)
    def _():
        s_ref[...] = jnp.zeros_like(s_ref)

    s = [s_ref[h] for h in heads]

    for p in range(tl // PAIR):
        rows = slice(p * PAIR, (p + 1) * PAIR)
        cosf = cos_a * cos_b_ref[rows, :] - sin_a * sin_b_ref[rows, :]
        sinf = sin_a * cos_bs_ref[rows, :] + cos_a * sin_bs_ref[rows, :]

        q_l, kb_l, kdt_l, v_l = [], [], [], []
        for h in heads:
            cs = slice(h * HEAD_DIM, (h + 1) * HEAD_DIM)
            q = q_ref[rows, cs].astype(F32)
            k = k_ref[rows, cs].astype(F32)
            q = q * cosf + pltpu.roll(q, HEAD_DIM // 2, axis=1) * sinf
            k = (k * cosf + pltpu.roll(k, HEAD_DIM // 2, axis=1) * sinf) * (HEAD_DIM ** -0.5)
            q_l.append(q)
            kb_l.append(k.astype(BF16))
            kdt_l.append((k * ks_ref[h]).T.astype(BF16))
            v_l.append(v_ref[rows, cs])

        qk_l = [(_dot_nt(q_l[h].astype(BF16), kb_l[h]) * dmat_ref[h]).astype(BF16) for h in heads]
        inner_l = [_dot(qk_l[h], v_l[h].astype(BF16)) for h in heads]
        qd_l = [(q_l[h] * qs_ref[h]).astype(BF16) for h in heads]

        cross_l = [[], []]
        for a in range(2):
            ra = slice(a * LA_CHUNK, (a + 1) * LA_CHUNK)
            cross_l[a] = [_dot(qd_l[h][ra], s[h].astype(BF16)) for h in heads]
            upd_l = [_dot(kdt_l[h], _chunk_halves(v_l[h][ra], a).astype(BF16)) for h in heads]
            s = [s[h] * cd_ref[h] + upd_l[h] for h in heads]

        for h in heads:
            cs = slice(h * HEAD_DIM, (h + 1) * HEAD_DIM)
            o = inner_l[h] + jnp.concatenate([cross_l[0][h], cross_l[1][h]], axis=0)
            o = o * lax.rsqrt(jnp.mean(o * o, axis=-1, keepdims=True) + EPS)
            o_ref[rows, cs] = (_silu(g_ref[rows, cs].astype(F32)) * o).astype(o_ref.dtype)

    for h in heads:
        s_ref[h] = s[h]


def _ret(proj, cast_arrays, *, batch, seq, tl=256):
    nblk = seq // tl
    tables = _retention_tables(seq, tl)
    cast_in_specs, cast_out_specs, cast_shapes = _cast_specs(cast_arrays, batch * nblk, lambda b, i: b * nblk + i)

    def seg(s_idx):
        return pl.BlockSpec((tl, QK_WIDTH), lambda b, i: (b * nblk + i, s_idx))

    def table():
        return pl.BlockSpec((N_HEADS, PAIR, HEAD_DIM), lambda b, i: (0, 0, 0))

    return pl.pallas_call(
        functools.partial(_ret_kernel, tl=tl, n_cast=len(cast_arrays)),
        grid=(batch, nblk),
        in_specs=[
            seg(4), seg(5), seg(6), seg(7),
            pl.BlockSpec((nblk, HEAD_DIM), lambda b, i: (0, 0)),
            pl.BlockSpec((nblk, HEAD_DIM), lambda b, i: (0, 0)),
            pl.BlockSpec((tl, HEAD_DIM), lambda b, i: (0, 0)),
            pl.BlockSpec((tl, HEAD_DIM), lambda b, i: (0, 0)),
            pl.BlockSpec((tl, HEAD_DIM), lambda b, i: (0, 0)),
            pl.BlockSpec((tl, HEAD_DIM), lambda b, i: (0, 0)),
            table(), table(), table(), table(),
        ] + cast_in_specs,
        out_specs=[pl.BlockSpec((tl, QK_WIDTH), lambda b, i: (b * nblk + i, 0))] + cast_out_specs,
        out_shape=[jax.ShapeDtypeStruct((batch * seq, QK_WIDTH), BF16)] + cast_shapes,
        scratch_shapes=[pltpu.VMEM((N_HEADS, HEAD_DIM, HEAD_DIM), F32)],
        compiler_params=_params(2, 32),
        name="retention",
    )(proj, proj, proj, proj, *tables, *cast_arrays)


def _sg_gate_kernel(u_ref, v_ref, lnw_ref, lnb_ref, ws_ref, bs_ref, o_ref, vn_ref, *, tm):
    lnw = lnw_ref[...]
    lnb = lnb_ref[...]
    step = 16

    def body(r, carry):
        sl = pl.ds(pl.multiple_of(r * step, step), step)
        x = v_ref[sl, :].astype(F32)
        mu = jnp.mean(x, axis=-1, keepdims=True)
        xc = x - mu
        var = jnp.mean(xc * xc, axis=-1, keepdims=True)
        vn_ref[sl, :] = (xc * lax.rsqrt(var + EPS) * lnw + lnb).astype(vn_ref.dtype)
        return carry

    lax.fori_loop(0, tm // step, body, 0, unroll=ROW_LOOP_UNROLL)

    row = lax.broadcasted_iota(jnp.int32, (SG_CHUNK, SG_CHUNK), 0)
    col = lax.broadcasted_iota(jnp.int32, (SG_CHUNK, SG_CHUNK), 1)
    for g in range(SG_GROUPS):
        wg = jnp.where(col <= row, ws_ref[g], 0.0).astype(BF16)
        bias = bs_ref[:, g:g + 1]
        cols = slice(g * SG_GROUP_DIM, (g + 1) * SG_GROUP_DIM)
        for c in range(tm // SG_CHUNK):
            rows = slice(c * SG_CHUNK, (c + 1) * SG_CHUNK)
            sgate = _dot(wg, vn_ref[rows, cols]) + bias
            o_ref[rows, cols] = (u_ref[rows, cols].astype(F32) * sgate).astype(o_ref.dtype)


def _sg_gate(proj, lnw, lnb, ws, bs_t, *, tm=512):
    t = proj.shape[0]
    return pl.pallas_call(
        functools.partial(_sg_gate_kernel, tm=tm),
        grid=(t // tm,),
        in_specs=[
            pl.BlockSpec((tm, SG_WIDTH), lambda i: (i, 0)),
            pl.BlockSpec((tm, SG_WIDTH), lambda i: (i, 1)),
            pl.BlockSpec((1, SG_WIDTH), lambda i: (0, 0)),
            pl.BlockSpec((1, SG_WIDTH), lambda i: (0, 0)),
            pl.BlockSpec((SG_GROUPS, SG_CHUNK, SG_CHUNK), lambda i: (0, 0, 0)),
            pl.BlockSpec((SG_CHUNK, SG_GROUPS), lambda i: (0, 0)),
        ],
        out_specs=pl.BlockSpec((tm, SG_WIDTH), lambda i: (i, 0)),
        out_shape=jax.ShapeDtypeStruct((t, SG_WIDTH), BF16),
        scratch_shapes=[pltpu.VMEM((tm, SG_WIDTH), BF16)],
        compiler_params=_params(1, 40),
        name="sg_gate",
    )(proj, proj, lnw, lnb, ws, bs_t)


def _retention_tables(seq, tl):
    half = HEAD_DIM // 2
    inv_freq = 1.0 / (ROPE_BASE ** jnp.linspace(0.0, 1.0, half, dtype=F32))
    base = (jnp.arange(seq // tl, dtype=F32) * tl)[:, None] * inv_freq[None, :]
    off = jnp.arange(tl, dtype=F32)[:, None] * inv_freq[None, :]

    def both_halves(a):
        return jnp.concatenate([a, a], axis=1)

    sign = jnp.concatenate([-jnp.ones((half,), F32), jnp.ones((half,), F32)])[None, :]
    cos_a, sin_a = both_halves(jnp.cos(base)), both_halves(jnp.sin(base))
    cos_b, sin_b = both_halves(jnp.cos(off)), both_halves(jnp.sin(off))

    log_gamma = jnp.log1p(-jnp.power(2.0, -5.0 - jnp.arange(N_HEADS, dtype=F32)))
    t = jnp.arange(PAIR)
    pos = (t % LA_CHUNK).astype(F32)
    mask = ((t[:, None] // LA_CHUNK) == (t[None, :] // LA_CHUNK)) & (t[None, :] <= t[:, None])
    dmat = jnp.exp(jnp.where(mask[None], (pos[:, None] - pos[None, :])[None] * log_gamma[:, None, None], -jnp.inf))
    full = (N_HEADS, PAIR, HEAD_DIM)
    q_scale = jnp.broadcast_to(jnp.exp((pos[None, :] + 1.0) * log_gamma[:, None])[:, :, None], full)
    k_scale = jnp.broadcast_to(jnp.exp((LA_CHUNK - 1.0 - pos[None, :]) * log_gamma[:, None])[:, :, None], full)
    chunk_decay = jnp.broadcast_to(jnp.exp(LA_CHUNK * log_gamma)[:, None, None], full)
    return cos_a, sin_a, cos_b, sin_b, cos_b * sign, sin_b * sign, dmat, q_scale, k_scale, chunk_decay


def kernel(x, norm_w, la_w_in, la_conv_w, la_a_log, la_dt_bias, la_out_norm_w, la_w_out, sg_w_in, sg_ln_w,
           sg_ln_b, sg_w_s, sg_b_s, sg_w_out, ffn_w_up, ffn_w_down):
    batch, seq, d = x.shape
    t = batch * seq
    h = x.reshape(t, d)
    gate0 = 4 * QK_WIDTH

    w_t = jnp.swapaxes(la_w_in[0], 0, 1).astype(BF16)
    proj, gates = _inproj_la(h, norm_w[0, 0][None, :], w_t, gate0=gate0, n_gate=2 * N_HEADS)

    cw = la_conv_w[0].reshape(3, QK_WIDTH, CONV_WIDTH).transpose(0, 2, 1)
    gp = jnp.zeros((8, LANES), F32)
    gp = gp.at[0, N_HEADS:2 * N_HEADS].set(la_a_log[0]).at[1, N_HEADS:2 * N_HEADS].set(la_dt_bias[0])
    n_layers, _, hidden = ffn_w_up.shape
    o_a, w_up, w_down = _gdn(proj, gates, cw, gp, la_out_norm_w[0][None, :],
                             [ffn_w_up.reshape(n_layers * d, hidden), ffn_w_down.reshape(n_layers * hidden, d)],
                             batch=batch, seq=seq)
    w_up = w_up.reshape(n_layers, d, hidden)
    w_down = w_down.reshape(n_layers, hidden, d)
    o_b, w_out, w_sg_in, w_sg_out = _ret(proj, [la_w_out[0], sg_w_in[0], sg_w_out[0]], batch=batch, seq=seq)

    h = _out_rms_res([o_a, o_b], [w_out, w_out], h, norm_w[0, 1][None, :], tm=512, vmem_mib=48)
    h = _ffn(h, norm_w[0, 2][None, :], w_up, w_down, norm_w[0, 3][None, :], 0)

    proj1 = _inproj_gelu(h, norm_w[1, 0][None, :], w_sg_in)
    gated = _sg_gate(proj1, sg_ln_w[0][None, :], sg_ln_b[0][None, :], sg_w_s[0], sg_b_s[0].T)
    h = _out_rms_res([gated], [w_sg_out], h, norm_w[1, 1][None, :], tm=512, vmem_mib=56)
    h = _ffn(h, norm_w[1, 2][None, :], w_up, w_down, norm_w[1, 3][None, :], 1)
    return h.reshape(batch, seq, d)
```

```python
import functools
import math

import jax
import jax.numpy as jnp
from jax import lax
from jax.experimental import pallas as pl
from jax.experimental.pallas import tpu as pltpu

F32 = jnp.float32
BF16 = jnp.bfloat16

D_MODEL = 2048
N_HEADS = 8
HEAD_DIM = 128
QK_WIDTH = N_HEADS * HEAD_DIM
LA_CHUNK = 64
PAIR = 2 * LA_CHUNK
CONV_WIDTH = 4
HALO = 8
ROPE_BASE = 10000.0
SG_CHUNK = 128
SG_GROUPS = 8
SG_WIDTH = 2 * D_MODEL
SG_GROUP_DIM = SG_WIDTH // SG_GROUPS
FFN_HIDDEN = 4 * D_MODEL
EPS = 1e-6
LANES = 128
ROW_LOOP_UNROLL = 8
MIB = 1024 * 1024

NT_DIMS = (((1,), (1,)), ((), ()))


def _params(n_grid_axes, vmem_mib):
    return pltpu.CompilerParams(dimension_semantics=("arbitrary",) * n_grid_axes,
                                vmem_limit_bytes=vmem_mib * MIB)


def _dot(a, b):
    return jnp.dot(a, b, preferred_element_type=F32)


def _dot_nt(a, b):
    return lax.dot_general(a, b, NT_DIMS, preferred_element_type=F32)


def _silu(x):
    return x * jax.nn.sigmoid(x)


def _softplus(x):
    return jnp.maximum(x, 0.0) + jnp.log1p(jnp.exp(-jnp.abs(x)))


def _gelu_tanh(x):
    k0 = -2.0 * math.sqrt(2.0 / math.pi) * math.log2(math.e)
    k1 = k0 * 0.044715
    return x / (1.0 + jnp.exp2(x * (k0 + k1 * (x * x))))


def _rmsnorm_rows_to(src_ref, nw_ref, dst_ref, rows):
    nw = nw_ref[...]
    step = 16

    def body(r, carry):
        sl = pl.ds(pl.multiple_of(r * step, step), step)
        x = src_ref[sl, :]
        ms = jnp.mean(x * x, axis=-1, keepdims=True)
        dst_ref[sl, :] = (x * lax.rsqrt(ms + EPS) * nw).astype(dst_ref.dtype)
        return carry

    lax.fori_loop(0, rows // step, body, 0, unroll=ROW_LOOP_UNROLL)


def _rms_residual_rows(acc_ref, h_ref, nw_ref, o_ref, rows):
    nw = nw_ref[...]
    step = 16

    def body(r, carry):
        sl = pl.ds(pl.multiple_of(r * step, step), step)
        y = acc_ref[sl, :]
        ms = jnp.mean(y * y, axis=-1, keepdims=True)
        o_ref[sl, :] = h_ref[sl, :] + y * lax.rsqrt(ms + EPS) * nw
        return carry

    lax.fori_loop(0, rows // step, body, 0, unroll=ROW_LOOP_UNROLL)


def _column_chunks(width, chunk=1024):
    return [slice(c, c + chunk) for c in range(0, width, chunk)]


def _inproj_la_kernel(x_ref, nw_ref, w_ref, wg_ref, o_ref, g_ref, xn_ref, *, tm):
    @pl.when(pl.program_id(1) == 0)
    def _():
        _rmsnorm_rows_to(x_ref, nw_ref, xn_ref, tm)
        g_ref[...] = _dot_nt(xn_ref[...], wg_ref[...])

    for cols in _column_chunks(o_ref.shape[1]):
        o_ref[:, cols] = _dot_nt(xn_ref[...], w_ref[cols, :]).astype(o_ref.dtype)


def _inproj_la(x, nw, w_t, *, gate0, n_gate, tm=1024, tn=2048):
    t, k = x.shape
    n = w_t.shape[0] - n_gate
    per_part = gate0 // tn

    def w_rows(i, j):
        return (pl.multiple_of(j * tn + jnp.where(j >= per_part, n_gate, 0), n_gate), 0)

    return pl.pallas_call(
        functools.partial(_inproj_la_kernel, tm=tm),
        grid=(t // tm, n // tn),
        in_specs=[
            pl.BlockSpec((tm, k), lambda i, j: (i, 0)),
            pl.BlockSpec((1, k), lambda i, j: (0, 0)),
            pl.BlockSpec((pl.Element(tn), pl.Element(k)), w_rows),
            pl.BlockSpec((LANES, k), lambda i, j: (gate0 // LANES, 0)),
        ],
        out_specs=[
            pl.BlockSpec((tm, tn), lambda i, j: (i, j)),
            pl.BlockSpec((tm, LANES), lambda i, j: (i, 0)),
        ],
        out_shape=[jax.ShapeDtypeStruct((t, n), BF16), jax.ShapeDtypeStruct((t, LANES), F32)],
        scratch_shapes=[pltpu.VMEM((tm, k), BF16)],
        compiler_params=_params(2, 56),
        name="inproj_la",
    )(x, nw, w_t, w_t)


def _inproj_gelu_kernel(x_ref, nw_ref, w_ref, o_ref, xn_ref, *, tm):
    @pl.when(pl.program_id(1) == 0)
    def _():
        _rmsnorm_rows_to(x_ref, nw_ref, xn_ref, tm)

    for cols in _column_chunks(o_ref.shape[1]):
        o_ref[:, cols] = _gelu_tanh(_dot(xn_ref[...], w_ref[:, cols])).astype(o_ref.dtype)


def _inproj_gelu(x, nw, w, *, tm=1024, tn=2048):
    t, k = x.shape
    n = w.shape[1]
    return pl.pallas_call(
        functools.partial(_inproj_gelu_kernel, tm=tm),
        grid=(t // tm, n // tn),
        in_specs=[
            pl.BlockSpec((tm, k), lambda i, j: (i, 0)),
            pl.BlockSpec((1, k), lambda i, j: (0, 0)),
            pl.BlockSpec((k, tn), lambda i, j: (0, j)),
        ],
        out_specs=pl.BlockSpec((tm, tn), lambda i, j: (i, j)),
        out_shape=jax.ShapeDtypeStruct((t, n), BF16),
        scratch_shapes=[pltpu.VMEM((tm, k), BF16)],
        compiler_params=_params(2, 56),
        name="inproj_gelu",
    )(x, nw, w)


def _out_rms_res_kernel(*refs, n_pairs, tm):
    a_refs = refs[:n_pairs]
    w_refs = refs[n_pairs:2 * n_pairs]
    h_ref, nw_ref, o_ref, acc_ref = refs[2 * n_pairs:]
    y = _dot(a_refs[0][...], w_refs[0][...])
    for a_ref, w_ref in zip(a_refs[1:], w_refs[1:]):
        y = y + _dot(a_ref[...], w_ref[...])
    acc_ref[...] = y
    _rms_residual_rows(acc_ref, h_ref, nw_ref, o_ref, tm)


def _out_rms_res(a_list, w_list, h, nw, *, tm, vmem_mib):
    t, n = h.shape
    n_pairs = len(a_list)
    in_specs = [pl.BlockSpec((tm, a.shape[1]), lambda i: (i, 0)) for a in a_list]
    k0 = 0
    for a, w in zip(a_list, w_list):
        ka = a.shape[1]
        row_block = k0 // ka if w.shape[0] != ka else 0
        in_specs.append(pl.BlockSpec((ka, n), lambda i, rb=row_block: (rb, 0)))
        k0 += ka
    in_specs += [pl.BlockSpec((tm, n), lambda i: (i, 0)), pl.BlockSpec((1, n), lambda i: (0, 0))]
    return pl.pallas_call(
        functools.partial(_out_rms_res_kernel, n_pairs=n_pairs, tm=tm),
        grid=(t // tm,),
        in_specs=in_specs,
        out_specs=pl.BlockSpec((tm, n), lambda i: (i, 0)),
        out_shape=jax.ShapeDtypeStruct((t, n), F32),
        scratch_shapes=[pltpu.VMEM((tm, n), F32)],
        compiler_params=_params(1, vmem_mib),
        name="out_rms_res",
    )(*a_list, *w_list, h, nw)


def _ffn_kernel(h_ref, h_next_ref, nw_in_ref, wup_ref, wdn_ref, nw_out_ref, o_ref, xn_ref, acc_ref, *, tm, n_j):
    i = pl.program_id(0)
    j = pl.program_id(1)
    slot = lax.rem(i, 2)

    @pl.when((i == 0) & (j == 0))
    def _():
        _rmsnorm_rows_to(h_ref, nw_in_ref, xn_ref.at[0], tm)

    def step(first):
        u = _dot(xn_ref[slot], wup_ref[...])
        u = jnp.square(jnp.maximum(u, 0.0)).astype(BF16)
        down = _dot(u, wdn_ref[...])
        if first:
            acc_ref[...] = down
        else:
            acc_ref[...] += down

        nw_in = nw_in_ref[...]
        share = tm // n_j
        xn_next_ref = xn_ref.at[1 - slot]
        for g in range(share // 16):
            sl = pl.ds(pl.multiple_of(j * share + g * 16, 16), 16)
            x = h_next_ref[sl, :]
            ms = jnp.mean(x * x, axis=-1, keepdims=True)
            xn_next_ref[sl, :] = (x * lax.rsqrt(ms + EPS) * nw_in).astype(BF16)

    pl.when(j == 0)(functools.partial(step, True))
    pl.when(j > 0)(functools.partial(step, False))

    @pl.when(j == n_j - 1)
    def _():
        _rms_residual_rows(acc_ref, h_ref, nw_out_ref, o_ref, tm)


def _ffn(h, nw_in, wup, wdn, nw_out, layer, *, tm=512, th=1024):
    t, d = h.shape
    hidden = wup.shape[2]
    n_i, n_j = t // tm, hidden // th
    return pl.pallas_call(
        functools.partial(_ffn_kernel, tm=tm, n_j=n_j),
        grid=(n_i, n_j),
        in_specs=[
            pl.BlockSpec((tm, d), lambda i, j: (i, 0)),
            pl.BlockSpec((tm, d), lambda i, j: (jnp.minimum(i + 1, n_i - 1), 0)),
            pl.BlockSpec((1, d), lambda i, j: (0, 0)),
            pl.BlockSpec((None, d, th), lambda i, j: (layer, 0, j)),
            pl.BlockSpec((None, th, d), lambda i, j: (layer, j, 0)),
            pl.BlockSpec((1, d), lambda i, j: (0, 0)),
        ],
        out_specs=pl.BlockSpec((tm, d), lambda i, j: (i, 0)),
        out_shape=jax.ShapeDtypeStruct((t, d), F32),
        scratch_shapes=[pltpu.VMEM((2, tm, d), BF16), pltpu.VMEM((tm, d), F32)],
        compiler_params=_params(2, 56),
        name="ffn",
    )(h, h, nw_in, wup, wdn, nw_out)


def _cast_specs(arrays, n_steps, step_index):
    in_specs, out_specs, out_shapes = [], [], []
    for a in arrays:
        rows, cols = a.shape[0] // n_steps, a.shape[1]
        in_specs.append(pl.BlockSpec((rows, cols), lambda b, i: (step_index(b, i), 0)))
        out_specs.append(pl.BlockSpec((rows, cols), lambda b, i: (step_index(b, i), 0)))
        out_shapes.append(jax.ShapeDtypeStruct(a.shape, BF16))
    return in_specs, out_specs, out_shapes


def _cast_slabs(in_refs, out_refs):
    for in_ref, out_ref in zip(in_refs, out_refs):
        out_ref[...] = in_ref[...].astype(out_ref.dtype)


def _pair_masks():
    row = lax.broadcasted_iota(jnp.int32, (PAIR, PAIR), 0)
    col = lax.broadcasted_iota(jnp.int32, (PAIR, PAIR), 1)
    same_chunk = (row >> 6) == (col >> 6)
    return row, col, same_chunk


def _chunk_halves(parts, a):
    zeros = jnp.zeros_like(parts)
    return jnp.concatenate([parts, zeros] if a == 0 else [zeros, parts], axis=0)


def _gdn_kernel(*refs, tl, n_cast):
    (q_ref, k_ref, v_ref, z_ref, gate_ref, cw_ref, gp_ref, onw_ref), rest = refs[:8], refs[8:]
    cast_in, o_ref, cast_out, (xpad_ref, s_ref) = rest[:n_cast], rest[n_cast], rest[n_cast + 1:2 * n_cast + 1], rest[2 * n_cast + 1:]
    _cast_slabs(cast_in, cast_out)
    _gdn_body(q_ref, k_ref, v_ref, z_ref, gate_ref, cw_ref, gp_ref, onw_ref, o_ref, xpad_ref, s_ref, tl=tl)


def _gdn_body(q_ref, k_ref, v_ref, z_ref, gate_ref, cw_ref, gp_ref, onw_ref, o_ref,
              xpad_ref, s_ref, *, tl):
    nh = N_HEADS
    heads = range(nh)

    @pl.when(pl.program_id(1) == 0)
    def _():
        xpad_ref[:, 0:HALO, :] = jnp.zeros((3, HALO, QK_WIDTH), F32)
        s_ref[...] = jnp.zeros_like(s_ref)

    xpad_ref[0, HALO:HALO + tl, :] = q_ref[...].astype(F32)
    xpad_ref[1, HALO:HALO + tl, :] = k_ref[...].astype(F32)
    xpad_ref[2, HALO:HALO + tl, :] = v_ref[...].astype(F32)

    row, col, same_chunk = _pair_masks()
    causal = same_chunk & (col <= row)
    strict = same_chunk & (col < row)
    eye = (row == col).astype(F32)
    row_in_chunk = row & (LA_CHUNK - 1)
    lane8 = lax.broadcasted_iota(jnp.int32, (nh, PAIR), 1)

    a_coef = -jnp.exp(gp_ref[0:1, :])
    dt_bias = gp_ref[1:2, :]
    onw = onw_ref[...]
    s = [s_ref[h] for h in heads]

    for p in range(tl // PAIR):
        r0 = p * PAIR
        rows = slice(r0, r0 + PAIR)

        pre = gate_ref[rows, :]
        beta_all = jax.nn.sigmoid(pre)
        g = a_coef * _softplus(pre + dt_bias)
        for sh in (1, 2, 4, 8, 16, 32):
            g = g + jnp.where(row_in_chunk >= sh, pltpu.roll(g, sh, axis=0), 0.0)
        gt8 = g.T[nh:2 * nh, :]
        g_last = [jnp.sum(jnp.where(lane8 == (a + 1) * LA_CHUNK - 1, gt8, 0.0), axis=1, keepdims=True)
                  for a in range(2)]
        k_tail8 = jnp.exp(jnp.where(lane8 < LA_CHUNK, g_last[0], g_last[1]) - gt8)
        chunk_decay = [jnp.exp(gl) for gl in g_last]

        def conv(ci, cs):
            w = cw_ref[ci, :, cs]
            acc = xpad_ref[ci, HALO + r0:HALO + r0 + PAIR, cs] * w[CONV_WIDTH - 1:CONV_WIDTH, :]
            for j in range(1, CONV_WIDTH):
                acc = acc + xpad_ref[ci, HALO + r0 - j:HALO + r0 - j + PAIR, cs] * w[CONV_WIDTH - 1 - j:CONV_WIDTH - j, :]
            return _silu(acc)

        q_l, k_l, rhs_l, kbf_l, decay_l, m1_l, lhs_l = [], [], [], [], [], [], []
        for h in heads:
            cs = slice(h * HEAD_DIM, (h + 1) * HEAD_DIM)
            q = conv(0, cs)
            k = conv(1, cs)
            v = conv(2, cs)
            q = q * lax.rsqrt(jnp.sum(q * q, axis=-1, keepdims=True) + 1e-6) * (HEAD_DIM ** -0.5)
            k = k * lax.rsqrt(jnp.sum(k * k, axis=-1, keepdims=True) + 1e-6)
            gc_col = jnp.sum(jnp.where(col == nh + h, g, 0.0), axis=1, keepdims=True)
            beta_col = jnp.sum(jnp.where(col == h, beta_all, 0.0), axis=1, keepdims=True)
            decay = jnp.exp(jnp.where(causal, gc_col - gt8[h:h + 1, :], -jnp.inf))
            e_col = jnp.exp(gc_col)
            q_l.append(q * e_col)
            k_l.append(k)
            rhs_l.append(jnp.concatenate([v * beta_col, k * (beta_col * e_col)], axis=1).astype(BF16))
            kbf = k.astype(BF16)
            kbf_l.append(kbf)
            decay_l.append(decay)
            m1_l.append(beta_col * decay)
            lhs_l.append(jnp.concatenate([q.astype(BF16), kbf], axis=0))

        r_l = [_dot_nt(lhs_l[h], kbf_l[h]) for h in heads]
        qk_l = [(r_l[h][0:PAIR] * decay_l[h]).astype(BF16) for h in heads]
        a_l = [jnp.where(strict, r_l[h][PAIR:2 * PAIR] * m1_l[h], 0.0) for h in heads]
        x_l = [eye - a_l[h] for h in heads]
        pb_l = [a_l[h].astype(BF16) for h in heads]
        for _ in range(5):
            pb_l = [_dot(pb_l[h], pb_l[h]).astype(BF16) for h in heads]
            x_l = [x_l[h] + _dot(x_l[h].astype(BF16), pb_l[h]) for h in heads]

        uw_l = [_dot(x_l[h].astype(BF16), rhs_l[h]) for h in heads]
        u_l = [uw_l[h][:, 0:HEAD_DIM] for h in heads]
        w_l = [uw_l[h][:, HEAD_DIM:2 * HEAD_DIM] for h in heads]
        kt_l = [(k_l[h].T * k_tail8[h:h + 1, :]).astype(BF16) for h in heads]

        v_new_l = [[], []]
        qs_l = [[], []]
        for a in range(2):
            ra = slice(a * LA_CHUNK, (a + 1) * LA_CHUNK)
            r2_l = [_dot(jnp.concatenate([w_l[h][ra], q_l[h][ra]], axis=0).astype(BF16), s[h].astype(BF16))
                    for h in heads]
            for h in heads:
                v_new = u_l[h][ra] - r2_l[h][0:LA_CHUNK]
                v_new_l[a].append(v_new)
                qs_l[a].append(r2_l[h][LA_CHUNK:2 * LA_CHUNK])
            upd_l = [_dot(kt_l[h], _chunk_halves(v_new_l[a][h], a).astype(BF16)) for h in heads]
            s = [s[h] * chunk_decay[a][h:h + 1, :] + upd_l[h] for h in heads]

        o_l = [jnp.concatenate([qs_l[0][h], qs_l[1][h]], axis=0)
               + _dot(qk_l[h], jnp.concatenate([v_new_l[0][h], v_new_l[1][h]], axis=0).astype(BF16))
               for h in heads]
        for h in heads:
            cs = slice(h * HEAD_DIM, (h + 1) * HEAD_DIM)
            o = o_l[h]
            o = o * lax.rsqrt(jnp.mean(o * o, axis=-1, keepdims=True) + EPS) * onw * _silu(z_ref[rows, cs].astype(F32))
            o_ref[rows, cs] = o.astype(o_ref.dtype)

    for h in heads:
        s_ref[h] = s[h]
    xpad_ref[:, 0:HALO, :] = xpad_ref[:, tl:tl + HALO, :]


def _gdn(proj, gates, cw, gp, onw, cast_arrays, *, batch, seq, tl=256):
    nblk = seq // tl
    cast_in_specs, cast_out_specs, cast_shapes = _cast_specs(cast_arrays, batch * nblk, lambda b, i: b * nblk + i)

    def seg(s_idx):
        return pl.BlockSpec((tl, QK_WIDTH), lambda b, i: (b * nblk + i, s_idx))

    return pl.pallas_call(
        functools.partial(_gdn_kernel, tl=tl, n_cast=len(cast_arrays)),
        grid=(batch, nblk),
        in_specs=[
            seg(0), seg(1), seg(2), seg(3),
            pl.BlockSpec((tl, LANES), lambda b, i: (b * nblk + i, 0)),
            pl.BlockSpec((3, CONV_WIDTH, QK_WIDTH), lambda b, i: (0, 0, 0)),
            pl.BlockSpec((8, LANES), lambda b, i: (0, 0)),
            pl.BlockSpec((1, HEAD_DIM), lambda b, i: (0, 0)),
        ] + cast_in_specs,
        out_specs=[pl.BlockSpec((tl, QK_WIDTH), lambda b, i: (b * nblk + i, 0))] + cast_out_specs,
        out_shape=[jax.ShapeDtypeStruct((batch * seq, QK_WIDTH), BF16)] + cast_shapes,
        scratch_shapes=[pltpu.VMEM((3, HALO + tl, QK_WIDTH), F32), pltpu.VMEM((N_HEADS, HEAD_DIM, HEAD_DIM), F32)],
        compiler_params=_params(2, 48),
        name="gdn",
    )(proj, proj, proj, proj, gates, cw, gp, onw, *cast_arrays)


def _ret_kernel(*refs, tl, n_cast):
    main_in, rest = refs[:14], refs[14:]
    cast_in, o_ref, cast_out, (s_ref,) = rest[:n_cast], rest[n_cast], rest[n_cast + 1:2 * n_cast + 1], rest[2 * n_cast + 1:]
    _cast_slabs(cast_in, cast_out)
    _ret_body(*main_in, o_ref, s_ref, tl=tl)


def _ret_body(q_ref, k_ref, v_ref, g_ref, cos_a_ref, sin_a_ref, cos_b_ref, sin_b_ref, cos_bs_ref, sin_bs_ref,
              dmat_ref, qs_ref, ks_ref, cd_ref, o_ref, s_ref, *, tl):
    heads = range(N_HEADS)
    blk = pl.ds(pl.program_id(1), 1)
    cos_a = cos_a_ref[blk, :]
    sin_a = sin_a_ref[blk, :]

    @pl.when(pl.program_id(1) == 0)
    def _():
        s_ref[...] = jnp.zeros_like(s_ref)

    s = [s_ref[h] for h in heads]

    for p in range(tl // PAIR):
        rows = slice(p * PAIR, (p + 1) * PAIR)
        cosf = cos_a * cos_b_ref[rows, :] - sin_a * sin_b_ref[rows, :]
        sinf = sin_a * cos_bs_ref[rows, :] + cos_a * sin_bs_ref[rows, :]

        q_l, kb_l, kdt_l, v_l = [], [], [], []
        for h in heads:
            cs = slice(h * HEAD_DIM, (h + 1) * HEAD_DIM)
            q = q_ref[rows, cs].astype(F32)
            k = k_ref[rows, cs].astype(F32)
            q = q * cosf + pltpu.roll(q, HEAD_DIM // 2, axis=1) * sinf
            k = (k * cosf + pltpu.roll(k, HEAD_DIM // 2, axis=1) * sinf) * (HEAD_DIM ** -0.5)
            q_l.append(q)
            kb_l.append(k.astype(BF16))
            kdt_l.append((k * ks_ref[h]).T.astype(BF16))
            v_l.append(v_ref[rows, cs])

        qk_l = [(_dot_nt(q_l[h].astype(BF16), kb_l[h]) * dmat_ref[h]).astype(BF16) for h in heads]
        inner_l = [_dot(qk_l[h], v_l[h].astype(BF16)) for h in heads]
        qd_l = [(q_l[h] * qs_ref[h]).astype(BF16) for h in heads]

        cross_l = [[], []]
        for a in range(2):
            ra = slice(a * LA_CHUNK, (a + 1) * LA_CHUNK)
            cross_l[a] = [_dot(qd_l[h][ra], s[h].astype(BF16)) for h in heads]
            upd_l = [_dot(kdt_l[h], _chunk_halves(v_l[h][ra], a).astype(BF16)) for h in heads]
            s = [s[h] * cd_ref[h] + upd_l[h] for h in heads]

        for h in heads:
            cs = slice(h * HEAD_DIM, (h + 1) * HEAD_DIM)
            o = inner_l[h] + jnp.concatenate([cross_l[0][h], cross_l[1][h]], axis=0)
            o = o * lax.rsqrt(jnp.mean(o * o, axis=-1, keepdims=True) + EPS)
            o_ref[rows, cs] = (_silu(g_ref[rows, cs].astype(F32)) * o).astype(o_ref.dtype)

    for h in heads:
        s_ref[h] = s[h]


def _ret(proj, cast_arrays, *, batch, seq, tl=256):
    nblk = seq // tl
    tables = _retention_tables(seq, tl)
    cast_in_specs, cast_out_specs, cast_shapes = _cast_specs(cast_arrays, batch * nblk, lambda b, i: b * nblk + i)

    def seg(s_idx):
        return pl.BlockSpec((tl, QK_WIDTH), lambda b, i: (b * nblk + i, s_idx))

    def table():
        return pl.BlockSpec((N_HEADS, PAIR, HEAD_DIM), lambda b, i: (0, 0, 0))

    return pl.pallas_call(
        functools.partial(_ret_kernel, tl=tl, n_cast=len(cast_arrays)),
        grid=(batch, nblk),
        in_specs=[
            seg(4), seg(5), seg(6), seg(7),
            pl.BlockSpec((nblk, HEAD_DIM), lambda b, i: (0, 0)),
            pl.BlockSpec((nblk, HEAD_DIM), lambda b, i: (0, 0)),
            pl.BlockSpec((tl, HEAD_DIM), lambda b, i: (0, 0)),
            pl.BlockSpec((tl, HEAD_DIM), lambda b, i: (0, 0)),
            pl.BlockSpec((tl, HEAD_DIM), lambda b, i: (0, 0)),
            pl.BlockSpec((tl, HEAD_DIM), lambda b, i: (0, 0)),
            table(), table(), table(), table(),
        ] + cast_in_specs,
        out_specs=[pl.BlockSpec((tl, QK_WIDTH), lambda b, i: (b * nblk + i, 0))] + cast_out_specs,
        out_shape=[jax.ShapeDtypeStruct((batch * seq, QK_WIDTH), BF16)] + cast_shapes,
        scratch_shapes=[pltpu.VMEM((N_HEADS, HEAD_DIM, HEAD_DIM), F32)],
        compiler_params=_params(2, 32),
        name="retention",
    )(proj, proj, proj, proj, *tables, *cast_arrays)


def _sg_gate_kernel(u_ref, v_ref, lnw_ref, lnb_ref, ws_ref, bs_ref, o_ref, vn_ref, *, tm):
    lnw = lnw_ref[...]
    lnb = lnb_ref[...]
    step = 16

    def body(r, carry):
        sl = pl.ds(pl.multiple_of(r * step, step), step)
        x = v_ref[sl, :].astype(F32)
        mu = jnp.mean(x, axis=-1, keepdims=True)
        xc = x - mu
        var = jnp.mean(xc * xc, axis=-1, keepdims=True)
        vn_ref[sl, :] = (xc * lax.rsqrt(var + EPS) * lnw + lnb).astype(vn_ref.dtype)
        return carry

    lax.fori_loop(0, tm // step, body, 0, unroll=ROW_LOOP_UNROLL)

    row = lax.broadcasted_iota(jnp.int32, (SG_CHUNK, SG_CHUNK), 0)
    col = lax.broadcasted_iota(jnp.int32, (SG_CHUNK, SG_CHUNK), 1)
    for g in range(SG_GROUPS):
        wg = jnp.where(col <= row, ws_ref[g], 0.0).astype(BF16)
        bias = bs_ref[:, g:g + 1]
        cols = slice(g * SG_GROUP_DIM, (g + 1) * SG_GROUP_DIM)
        for c in range(tm // SG_CHUNK):
            rows = slice(c * SG_CHUNK, (c + 1) * SG_CHUNK)
            sgate = _dot(wg, vn_ref[rows, cols]) + bias
            o_ref[rows, cols] = (u_ref[rows, cols].astype(F32) * sgate).astype(o_ref.dtype)


def _sg_gate(proj, lnw, lnb, ws, bs_t, *, tm=512):
    t = proj.shape[0]
    return pl.pallas_call(
        functools.partial(_sg_gate_kernel, tm=tm),
        grid=(t // tm,),
        in_specs=[
            pl.BlockSpec((tm, SG_WIDTH), lambda i: (i, 0)),
            pl.BlockSpec((tm, SG_WIDTH), lambda i: (i, 1)),
            pl.BlockSpec((1, SG_WIDTH), lambda i: (0, 0)),
            pl.BlockSpec((1, SG_WIDTH), lambda i: (0, 0)),
            pl.BlockSpec((SG_GROUPS, SG_CHUNK, SG_CHUNK), lambda i: (0, 0, 0)),
            pl.BlockSpec((SG_CHUNK, SG_GROUPS), lambda i: (0, 0)),
        ],
        out_specs=pl.BlockSpec((tm, SG_WIDTH), lambda i: (i, 0)),
        out_shape=jax.ShapeDtypeStruct((t, SG_WIDTH), BF16),
        scratch_shapes=[pltpu.VMEM((tm, SG_WIDTH), BF16)],
        compiler_params=_params(1, 40),
        name="sg_gate",
    )(proj, proj, lnw, lnb, ws, bs_t)


def _retention_tables(seq, tl):
    half = HEAD_DIM // 2
    inv_freq = 1.0 / (ROPE_BASE ** jnp.linspace(0.0, 1.0, half, dtype=F32))
    base = (jnp.arange(seq // tl, dtype=F32) * tl)[:, None] * inv_freq[None, :]
    off = jnp.arange(tl, dtype=F32)[:, None] * inv_freq[None, :]

    def both_halves(a):
        return jnp.concatenate([a, a], axis=1)

    sign = jnp.concatenate([-jnp.ones((half,), F32), jnp.ones((half,), F32)])[None, :]
    cos_a, sin_a = both_halves(jnp.cos(base)), both_halves(jnp.sin(base))
    cos_b, sin_b = both_halves(jnp.cos(off)), both_halves(jnp.sin(off))

    log_gamma = jnp.log1p(-jnp.power(2.0, -5.0 - jnp.arange(N_HEADS, dtype=F32)))
    t = jnp.arange(PAIR)
    pos = (t % LA_CHUNK).astype(F32)
    mask = ((t[:, None] // LA_CHUNK) == (t[None, :] // LA_CHUNK)) & (t[None, :] <= t[:, None])
    dmat = jnp.exp(jnp.where(mask[None], (pos[:, None] - pos[None, :])[None] * log_gamma[:, None, None], -jnp.inf))
    full = (N_HEADS, PAIR, HEAD_DIM)
    q_scale = jnp.broadcast_to(jnp.exp((pos[None, :] + 1.0) * log_gamma[:, None])[:, :, None], full)
    k_scale = jnp.broadcast_to(jnp.exp((LA_CHUNK - 1.0 - pos[None, :]) * log_gamma[:, None])[:, :, None], full)
    chunk_decay = jnp.broadcast_to(jnp.exp(LA_CHUNK * log_gamma)[:, None, None], full)
    return cos_a, sin_a, cos_b, sin_b, cos_b * sign, sin_b * sign, dmat, q_scale, k_scale, chunk_decay


def kernel(x, norm_w, la_w_in, la_conv_w, la_a_log, la_dt_bias, la_out_norm_w, la_w_out, sg_w_in, sg_ln_w,
           sg_ln_b, sg_w_s, sg_b_s, sg_w_out, ffn_w_up, ffn_w_down):
    batch, seq, d = x.shape
    t = batch * seq
    h = x.reshape(t, d)
    gate0 = 4 * QK_WIDTH

    w_t = jnp.swapaxes(la_w_in[0], 0, 1).astype(BF16)
    proj, gates = _inproj_la(h, norm_w[0, 0][None, :], w_t, gate0=gate0, n_gate=2 * N_HEADS)

    cw = la_conv_w[0].reshape(3, QK_WIDTH, CONV_WIDTH).transpose(0, 2, 1)
    gp = jnp.zeros((8, LANES), F32)
    gp = gp.at[0, N_HEADS:2 * N_HEADS].set(la_a_log[0]).at[1, N_HEADS:2 * N_HEADS].set(la_dt_bias[0])
    n_layers, _, hidden = ffn_w_up.shape
    o_a, w_up, w_down = _gdn(proj, gates, cw, gp, la_out_norm_w[0][None, :],
                             [ffn_w_up.reshape(n_layers * d, hidden), ffn_w_down.reshape(n_layers * hidden, d)],
                             batch=batch, seq=seq)
    w_up = w_up.reshape(n_layers, d, hidden)
    w_down = w_down.reshape(n_layers, hidden, d)
    o_b, w_out, w_sg_in, w_sg_out = _ret(proj, [la_w_out[0], sg_w_in[0], sg_w_out[0]], batch=batch, seq=seq)

    h = _out_rms_res([o_a, o_b], [w_out, w_out], h, norm_w[0, 1][None, :], tm=512, vmem_mib=48)
    h = _ffn(h, norm_w[0, 2][None, :], w_up, w_down, norm_w[0, 3][None, :], 0)

    proj1 = _inproj_gelu(h, norm_w[1, 0][None, :], w_sg_in)
    gated = _sg_gate(proj1, sg_ln_w[0][None, :], sg_ln_b[0][None, :], sg_w_s[0], sg_b_s[0].T)
    h = _out_rms_res([gated], [w_sg_out], h, norm_w[1, 1][None, :], tm=512, vmem_mib=56)
    h = _ffn(h, norm_w[1, 2][None, :], w_up, w_down, norm_w[1, 3][None, :], 1)
    return h.reshape(batch, seq, d)
```

```python
import functools
import math

import jax
import jax.numpy as jnp
from jax import lax
from jax.experimental import pallas as pl
from jax.experimental.pallas import tpu as pltpu

F32 = jnp.float32
BF16 = jnp.bfloat16

D_MODEL = 2048
N_HEADS = 8
HEAD_DIM = 128
QK_WIDTH = N_HEADS * HEAD_DIM
LA_CHUNK = 64
PAIR = 2 * LA_CHUNK
CONV_WIDTH = 4
HALO = 8
ROPE_BASE = 10000.0
SG_CHUNK = 128
SG_GROUPS = 8
SG_WIDTH = 2 * D_MODEL
SG_GROUP_DIM = SG_WIDTH // SG_GROUPS
FFN_HIDDEN = 4 * D_MODEL
EPS = 1e-6
LANES = 128
FFN_TILE = 1024
INPROJ_TILE = 2048
ROW_LOOP_UNROLL = 8
MIB = 1024 * 1024

NT_DIMS = (((1,), (1,)), ((), ()))


def _params(n_grid_axes, vmem_mib):
    return pltpu.CompilerParams(dimension_semantics=("arbitrary",) * n_grid_axes,
                                vmem_limit_bytes=vmem_mib * MIB)


def _dot(a, b):
    return jnp.dot(a, b, preferred_element_type=F32)


def _dot_nt(a, b):
    return lax.dot_general(a, b, NT_DIMS, preferred_element_type=F32)


def _silu(x):
    return x * jax.nn.sigmoid(x)


def _softplus(x):
    return jnp.maximum(x, 0.0) + jnp.log1p(jnp.exp(-jnp.abs(x)))


def _gelu_tanh(x):
    k0 = -2.0 * math.sqrt(2.0 / math.pi) * math.log2(math.e)
    k1 = k0 * 0.044715
    return x / (1.0 + jnp.exp2(x * (k0 + k1 * (x * x))))


def _rmsnorm_rows_to(src_ref, nw_ref, dst_ref, rows):
    nw = nw_ref[...]
    step = 16

    def body(r, carry):
        sl = pl.ds(pl.multiple_of(r * step, step), step)
        x = src_ref[sl, :]
        ms = jnp.mean(x * x, axis=-1, keepdims=True)
        dst_ref[sl, :] = (x * lax.rsqrt(ms + EPS) * nw).astype(dst_ref.dtype)
        return carry

    lax.fori_loop(0, rows // step, body, 0, unroll=ROW_LOOP_UNROLL)


def _rms_residual_rows(acc_ref, h_ref, nw_ref, o_ref, rows):
    nw = nw_ref[...]
    step = 16

    def body(r, carry):
        sl = pl.ds(pl.multiple_of(r * step, step), step)
        y = acc_ref[sl, :]
        ms = jnp.mean(y * y, axis=-1, keepdims=True)
        o_ref[sl, :] = h_ref[sl, :] + y * lax.rsqrt(ms + EPS) * nw
        return carry

    lax.fori_loop(0, rows // step, body, 0, unroll=ROW_LOOP_UNROLL)


def _column_chunks(width, chunk=1024):
    return [slice(c, c + chunk) for c in range(0, width, chunk)]


def _inproj_la_kernel(x_ref, nw_ref, w_ref, wg_ref, o_ref, g_ref, xn_ref, *, tm):
    @pl.when(pl.program_id(1) == 0)
    def _():
        _rmsnorm_rows_to(x_ref, nw_ref, xn_ref, tm)
        g_ref[...] = _dot_nt(xn_ref[...], wg_ref[...])

    for cols in _column_chunks(o_ref.shape[1]):
        o_ref[:, cols] = _dot_nt(xn_ref[...], w_ref[cols, :]).astype(o_ref.dtype)


def _inproj_la(x, nw, w_t, *, gate0, n_gate, tm=1024, tn=INPROJ_TILE):
    t, k = x.shape
    n = w_t.shape[0] - n_gate
    per_part = gate0 // tn

    def w_rows(i, j):
        return (pl.multiple_of(j * tn + jnp.where(j >= per_part, n_gate, 0), n_gate), 0)

    return pl.pallas_call(
        functools.partial(_inproj_la_kernel, tm=tm),
        grid=(t // tm, n // tn),
        in_specs=[
            pl.BlockSpec((tm, k), lambda i, j: (i, 0)),
            pl.BlockSpec((1, k), lambda i, j: (0, 0)),
            pl.BlockSpec((pl.Element(tn), pl.Element(k)), w_rows),
            pl.BlockSpec((LANES, k), lambda i, j: (gate0 // LANES, 0)),
        ],
        out_specs=[
            pl.BlockSpec((tm, tn), lambda i, j: (i, j)),
            pl.BlockSpec((tm, LANES), lambda i, j: (i, 0)),
        ],
        out_shape=[jax.ShapeDtypeStruct((t, n), BF16), jax.ShapeDtypeStruct((t, LANES), F32)],
        scratch_shapes=[pltpu.VMEM((tm, k), BF16)],
        compiler_params=_params(2, 56),
        name="inproj_la",
    )(x, nw, w_t, w_t)


def _inproj_gelu_kernel(x_ref, nw_ref, w_ref, o_ref, xn_ref, *, tm):
    @pl.when(pl.program_id(1) == 0)
    def _():
        _rmsnorm_rows_to(x_ref, nw_ref, xn_ref, tm)

    for cols in _column_chunks(o_ref.shape[1]):
        o_ref[:, cols] = _gelu_tanh(_dot(xn_ref[...], w_ref[:, cols])).astype(o_ref.dtype)


def _inproj_gelu(x, nw, w, *, tm=1024):
    t, k = x.shape
    n_blocks, _, tn = w.shape
    n = n_blocks * tn
    return pl.pallas_call(
        functools.partial(_inproj_gelu_kernel, tm=tm),
        grid=(t // tm, n // tn),
        in_specs=[
            pl.BlockSpec((tm, k), lambda i, j: (i, 0)),
            pl.BlockSpec((1, k), lambda i, j: (0, 0)),
            pl.BlockSpec((None, k, tn), lambda i, j: (j, 0, 0)),
        ],
        out_specs=pl.BlockSpec((tm, tn), lambda i, j: (i, j)),
        out_shape=jax.ShapeDtypeStruct((t, n), BF16),
        scratch_shapes=[pltpu.VMEM((tm, k), BF16)],
        compiler_params=_params(2, 56),
        name="inproj_gelu",
    )(x, nw, w)


def _out_rms_res_kernel(*refs, n_pairs, tm):
    a_refs = refs[:n_pairs]
    w_refs = refs[n_pairs:2 * n_pairs]
    h_ref, nw_ref, o_ref, acc_ref = refs[2 * n_pairs:]
    y = _dot(a_refs[0][...], w_refs[0][...])
    for a_ref, w_ref in zip(a_refs[1:], w_refs[1:]):
        y = y + _dot(a_ref[...], w_ref[...])
    acc_ref[...] = y
    _rms_residual_rows(acc_ref, h_ref, nw_ref, o_ref, tm)


def _out_rms_res(a_list, w_list, h, nw, *, tm, vmem_mib):
    t, n = h.shape
    n_pairs = len(a_list)
    in_specs = [pl.BlockSpec((tm, a.shape[1]), lambda i: (i, 0)) for a in a_list]
    k0 = 0
    for a, w in zip(a_list, w_list):
        ka = a.shape[1]
        row_block = k0 // ka if w.shape[0] != ka else 0
        in_specs.append(pl.BlockSpec((ka, n), lambda i, rb=row_block: (rb, 0)))
        k0 += ka
    in_specs += [pl.BlockSpec((tm, n), lambda i: (i, 0)), pl.BlockSpec((1, n), lambda i: (0, 0))]
    return pl.pallas_call(
        functools.partial(_out_rms_res_kernel, n_pairs=n_pairs, tm=tm),
        grid=(t // tm,),
        in_specs=in_specs,
        out_specs=pl.BlockSpec((tm, n), lambda i: (i, 0)),
        out_shape=jax.ShapeDtypeStruct((t, n), F32),
        scratch_shapes=[pltpu.VMEM((tm, n), F32)],
        compiler_params=_params(1, vmem_mib),
        name="out_rms_res",
    )(*a_list, *w_list, h, nw)


def _ffn_kernel(h_ref, h_next_ref, nw_in_ref, wup_ref, wdn_ref, nw_out_ref, o_ref, xn_ref, acc_ref, *, tm, n_j):
    i = pl.program_id(0)
    j = pl.program_id(1)
    slot = lax.rem(i, 2)

    @pl.when((i == 0) & (j == 0))
    def _():
        _rmsnorm_rows_to(h_ref, nw_in_ref, xn_ref.at[0], tm)

    def step(first):
        u = _dot(xn_ref[slot], wup_ref[...])
        u = jnp.square(jnp.maximum(u, 0.0)).astype(BF16)
        down = _dot(u, wdn_ref[...])
        if first:
            acc_ref[...] = down
        else:
            acc_ref[...] += down

        nw_in = nw_in_ref[...]
        share = tm // n_j
        xn_next_ref = xn_ref.at[1 - slot]
        for g in range(share // 16):
            sl = pl.ds(pl.multiple_of(j * share + g * 16, 16), 16)
            x = h_next_ref[sl, :]
            ms = jnp.mean(x * x, axis=-1, keepdims=True)
            xn_next_ref[sl, :] = (x * lax.rsqrt(ms + EPS) * nw_in).astype(BF16)

    pl.when(j == 0)(functools.partial(step, True))
    pl.when(j > 0)(functools.partial(step, False))

    @pl.when(j == n_j - 1)
    def _():
        _rms_residual_rows(acc_ref, h_ref, nw_out_ref, o_ref, tm)


def _ffn(h, nw_in, wup, wdn, nw_out, layer, *, tm=512):
    t, d = h.shape
    n_j, _, th = wup.shape
    n_i = t // tm
    return pl.pallas_call(
        functools.partial(_ffn_kernel, tm=tm, n_j=n_j),
        grid=(n_i, n_j),
        in_specs=[
            pl.BlockSpec((tm, d), lambda i, j: (i, 0)),
            pl.BlockSpec((tm, d), lambda i, j: (jnp.minimum(i + 1, n_i - 1), 0)),
            pl.BlockSpec((1, d), lambda i, j: (0, 0)),
            pl.BlockSpec((None, d, th), lambda i, j: (j, layer, 0)),
            pl.BlockSpec((None, th, d), lambda i, j: (layer, j, 0)),
            pl.BlockSpec((1, d), lambda i, j: (0, 0)),
        ],
        out_specs=pl.BlockSpec((tm, d), lambda i, j: (i, 0)),
        out_shape=jax.ShapeDtypeStruct((t, d), F32),
        scratch_shapes=[pltpu.VMEM((2, tm, d), BF16), pltpu.VMEM((tm, d), F32)],
        compiler_params=_params(2, 56),
        name="ffn",
    )(h, h, nw_in, wup, wdn, nw_out)


def _cast_specs(arrays, n_steps, step_index):
    in_specs, out_specs, out_shapes = [], [], []
    for a, col_block in arrays:
        rows, cols = a.shape[0] // n_steps, a.shape[1]
        in_specs.append(pl.BlockSpec((rows, cols), lambda b, i: (step_index(b, i), 0)))
        if col_block is None:
            out_specs.append(pl.BlockSpec((rows, cols), lambda b, i: (step_index(b, i), 0)))
            out_shapes.append(jax.ShapeDtypeStruct(a.shape, BF16))
        else:
            out_specs.append(pl.BlockSpec((cols // col_block, rows, col_block), lambda b, i: (0, step_index(b, i), 0)))
            out_shapes.append(jax.ShapeDtypeStruct((cols // col_block, a.shape[0], col_block), BF16))
    return in_specs, out_specs, out_shapes


def _cast_slabs(in_refs, out_refs):
    for in_ref, out_ref in zip(in_refs, out_refs):
        if len(out_ref.shape) == 2:
            out_ref[...] = in_ref[...].astype(out_ref.dtype)
        else:
            n_blocks, _, col_block = out_ref.shape
            for c in range(n_blocks):
                out_ref[c] = in_ref[:, c * col_block:(c + 1) * col_block].astype(out_ref.dtype)


def _pair_masks():
    row = lax.broadcasted_iota(jnp.int32, (PAIR, PAIR), 0)
    col = lax.broadcasted_iota(jnp.int32, (PAIR, PAIR), 1)
    same_chunk = (row >> 6) == (col >> 6)
    return row, col, same_chunk


def _chunk_halves(parts, a):
    zeros = jnp.zeros_like(parts)
    return jnp.concatenate([parts, zeros] if a == 0 else [zeros, parts], axis=0)


def _gdn_kernel(*refs, tl, n_cast):
    (q_ref, k_ref, v_ref, z_ref, gate_ref, cw_ref, gp_ref, onw_ref), rest = refs[:8], refs[8:]
    cast_in, o_ref, cast_out, (xpad_ref, s_ref) = rest[:n_cast], rest[n_cast], rest[n_cast + 1:2 * n_cast + 1], rest[2 * n_cast + 1:]
    _cast_slabs(cast_in, cast_out)
    _gdn_body(q_ref, k_ref, v_ref, z_ref, gate_ref, cw_ref, gp_ref, onw_ref, o_ref, xpad_ref, s_ref, tl=tl)


def _gdn_body(q_ref, k_ref, v_ref, z_ref, gate_ref, cw_ref, gp_ref, onw_ref, o_ref,
              xpad_ref, s_ref, *, tl):
    nh = N_HEADS
    heads = range(nh)

    @pl.when(pl.program_id(1) == 0)
    def _():
        xpad_ref[:, 0:HALO, :] = jnp.zeros((3, HALO, QK_WIDTH), F32)
        s_ref[...] = jnp.zeros_like(s_ref)

    xpad_ref[0, HALO:HALO + tl, :] = q_ref[...].astype(F32)
    xpad_ref[1, HALO:HALO + tl, :] = k_ref[...].astype(F32)
    xpad_ref[2, HALO:HALO + tl, :] = v_ref[...].astype(F32)

    row, col, same_chunk = _pair_masks()
    causal = same_chunk & (col <= row)
    strict = same_chunk & (col < row)
    eye = (row == col).astype(F32)
    row_in_chunk = row & (LA_CHUNK - 1)
    lane8 = lax.broadcasted_iota(jnp.int32, (nh, PAIR), 1)

    a_coef = -jnp.exp(gp_ref[0:1, :])
    dt_bias = gp_ref[1:2, :]
    onw = onw_ref[...]
    s = [s_ref[h] for h in heads]

    for p in range(tl // PAIR):
        r0 = p * PAIR
        rows = slice(r0, r0 + PAIR)

        pre = gate_ref[rows, :]
        beta_all = jax.nn.sigmoid(pre)
        g = a_coef * _softplus(pre + dt_bias)
        for sh in (1, 2, 4, 8, 16, 32):
            g = g + jnp.where(row_in_chunk >= sh, pltpu.roll(g, sh, axis=0), 0.0)
        gt8 = g.T[nh:2 * nh, :]
        g_last = [jnp.sum(jnp.where(lane8 == (a + 1) * LA_CHUNK - 1, gt8, 0.0), axis=1, keepdims=True)
                  for a in range(2)]
        k_tail8 = jnp.exp(jnp.where(lane8 < LA_CHUNK, g_last[0], g_last[1]) - gt8)
        chunk_decay = [jnp.exp(gl) for gl in g_last]

        def conv(ci, cs):
            w = cw_ref[ci, :, cs]
            acc = xpad_ref[ci, HALO + r0:HALO + r0 + PAIR, cs] * w[CONV_WIDTH - 1:CONV_WIDTH, :]
            for j in range(1, CONV_WIDTH):
                acc = acc + xpad_ref[ci, HALO + r0 - j:HALO + r0 - j + PAIR, cs] * w[CONV_WIDTH - 1 - j:CONV_WIDTH - j, :]
            return _silu(acc)

        q_l, k_l, rhs_l, kbf_l, decay_l, m1_l, lhs_l = [], [], [], [], [], [], []
        for h in heads:
            cs = slice(h * HEAD_DIM, (h + 1) * HEAD_DIM)
            q = conv(0, cs)
            k = conv(1, cs)
            v = conv(2, cs)
            q = q * lax.rsqrt(jnp.sum(q * q, axis=-1, keepdims=True) + 1e-6) * (HEAD_DIM ** -0.5)
            k = k * lax.rsqrt(jnp.sum(k * k, axis=-1, keepdims=True) + 1e-6)
            gc_col = jnp.sum(jnp.where(col == nh + h, g, 0.0), axis=1, keepdims=True)
            beta_col = jnp.sum(jnp.where(col == h, beta_all, 0.0), axis=1, keepdims=True)
            decay = jnp.exp(jnp.where(causal, gc_col - gt8[h:h + 1, :], -jnp.inf))
            e_col = jnp.exp(gc_col)
            q_l.append(q * e_col)
            k_l.append(k)
            rhs_l.append(jnp.concatenate([v * beta_col, k * (beta_col * e_col)], axis=1).astype(BF16))
            kbf = k.astype(BF16)
            kbf_l.append(kbf)
            decay_l.append(decay)
            m1_l.append(beta_col * decay)
            lhs_l.append(jnp.concatenate([q.astype(BF16), kbf], axis=0))

        r_l = [_dot_nt(lhs_l[h], kbf_l[h]) for h in heads]
        qk_l = [(r_l[h][0:PAIR] * decay_l[h]).astype(BF16) for h in heads]
        a_l = [jnp.where(strict, r_l[h][PAIR:2 * PAIR] * m1_l[h], 0.0) for h in heads]
        x_l = [eye - a_l[h] for h in heads]
        pb_l = [a_l[h].astype(BF16) for h in heads]
        for _ in range(5):
            pb_l = [_dot(pb_l[h], pb_l[h]).astype(BF16) for h in heads]
            x_l = [x_l[h] + _dot(x_l[h].astype(BF16), pb_l[h]) for h in heads]

        uw_l = [_dot(x_l[h].astype(BF16), rhs_l[h]) for h in heads]
        u_l = [uw_l[h][:, 0:HEAD_DIM] for h in heads]
        w_l = [uw_l[h][:, HEAD_DIM:2 * HEAD_DIM] for h in heads]
        kt_l = [(k_l[h].T * k_tail8[h:h + 1, :]).astype(BF16) for h in heads]

        v_new_l = [[], []]
        qs_l = [[], []]
        for a in range(2):
            ra = slice(a * LA_CHUNK, (a + 1) * LA_CHUNK)
            r2_l = [_dot(jnp.concatenate([w_l[h][ra], q_l[h][ra]], axis=0).astype(BF16), s[h].astype(BF16))
                    for h in heads]
            for h in heads:
                v_new = u_l[h][ra] - r2_l[h][0:LA_CHUNK]
                v_new_l[a].append(v_new)
                qs_l[a].append(r2_l[h][LA_CHUNK:2 * LA_CHUNK])
            upd_l = [_dot(kt_l[h], _chunk_halves(v_new_l[a][h], a).astype(BF16)) for h in heads]
            s = [s[h] * chunk_decay[a][h:h + 1, :] + upd_l[h] for h in heads]

        o_l = [jnp.concatenate([qs_l[0][h], qs_l[1][h]], axis=0)
               + _dot(qk_l[h], jnp.concatenate([v_new_l[0][h], v_new_l[1][h]], axis=0).astype(BF16))
               for h in heads]
        for h in heads:
            cs = slice(h * HEAD_DIM, (h + 1) * HEAD_DIM)
            o = o_l[h]
            o = o * lax.rsqrt(jnp.mean(o * o, axis=-1, keepdims=True) + EPS) * onw * _silu(z_ref[rows, cs].astype(F32))
            o_ref[rows, cs] = o.astype(o_ref.dtype)

    for h in heads:
        s_ref[h] = s[h]
    xpad_ref[:, 0:HALO, :] = xpad_ref[:, tl:tl + HALO, :]


def _gdn(proj, gates, cw, gp, onw, cast_arrays, *, batch, seq, tl=256):
    nblk = seq // tl
    cast_in_specs, cast_out_specs, cast_shapes = _cast_specs(cast_arrays, batch * nblk, lambda b, i: b * nblk + i)

    def seg(s_idx):
        return pl.BlockSpec((tl, QK_WIDTH), lambda b, i: (b * nblk + i, s_idx))

    return pl.pallas_call(
        functools.partial(_gdn_kernel, tl=tl, n_cast=len(cast_arrays)),
        grid=(batch, nblk),
        in_specs=[
            seg(0), seg(1), seg(2), seg(3),
            pl.BlockSpec((tl, LANES), lambda b, i: (b * nblk + i, 0)),
            pl.BlockSpec((3, CONV_WIDTH, QK_WIDTH), lambda b, i: (0, 0, 0)),
            pl.BlockSpec((8, LANES), lambda b, i: (0, 0)),
            pl.BlockSpec((1, HEAD_DIM), lambda b, i: (0, 0)),
        ] + cast_in_specs,
        out_specs=[pl.BlockSpec((tl, QK_WIDTH), lambda b, i: (b * nblk + i, 0))] + cast_out_specs,
        out_shape=[jax.ShapeDtypeStruct((batch * seq, QK_WIDTH), BF16)] + cast_shapes,
        scratch_shapes=[pltpu.VMEM((3, HALO + tl, QK_WIDTH), F32), pltpu.VMEM((N_HEADS, HEAD_DIM, HEAD_DIM), F32)],
        compiler_params=_params(2, 48),
        name="gdn",
    )(proj, proj, proj, proj, gates, cw, gp, onw, *[a for a, _ in cast_arrays])


def _ret_kernel(*refs, tl, n_cast):
    main_in, rest = refs[:14], refs[14:]
    cast_in, o_ref, cast_out, (s_ref,) = rest[:n_cast], rest[n_cast], rest[n_cast + 1:2 * n_cast + 1], rest[2 * n_cast + 1:]
    _cast_slabs(cast_in, cast_out)
    _ret_body(*main_in, o_ref, s_ref, tl=tl)


def _ret_body(q_ref, k_ref, v_ref, g_ref, cos_a_ref, sin_a_ref, cos_b_ref, sin_b_ref, cos_bs_ref, sin_bs_ref,
              dmat_ref, qs_ref, ks_ref, cd_ref, o_ref, s_ref, *, tl):
    heads = range(N_HEADS)
    blk = pl.ds(pl.program_id(1), 1)
    cos_a = cos_a_ref[blk, :]
    sin_a = sin_a_ref[blk, :]

    @pl.when(pl.program_id(1) == 0)
    def _():
        s_ref[...] = jnp.zeros_like(s_ref)

    s = [s_ref[h] for h in heads]

    for p in range(tl // PAIR):
        rows = slice(p * PAIR, (p + 1) * PAIR)
        cosf = cos_a * cos_b_ref[rows, :] - sin_a * sin_b_ref[rows, :]
        sinf = sin_a * cos_bs_ref[rows, :] + cos_a * sin_bs_ref[rows, :]

        q_l, kb_l, kdt_l, v_l = [], [], [], []
        for h in heads:
            cs = slice(h * HEAD_DIM, (h + 1) * HEAD_DIM)
            q = q_ref[rows, cs].astype(F32)
            k = k_ref[rows, cs].astype(F32)
            q = q * cosf + pltpu.roll(q, HEAD_DIM // 2, axis=1) * sinf
            k = (k * cosf + pltpu.roll(k, HEAD_DIM // 2, axis=1) * sinf) * (HEAD_DIM ** -0.5)
            q_l.append(q)
            kb_l.append(k.astype(BF16))
            kdt_l.append((k * ks_ref[h]).T.astype(BF16))
            v_l.append(v_ref[rows, cs])

        qk_l = [(_dot_nt(q_l[h].astype(BF16), kb_l[h]) * dmat_ref[h]).astype(BF16) for h in heads]
        inner_l = [_dot(qk_l[h], v_l[h].astype(BF16)) for h in heads]
        qd_l = [(q_l[h] * qs_ref[h]).astype(BF16) for h in heads]

        cross_l = [[], []]
        for a in range(2):
            ra = slice(a * LA_CHUNK, (a + 1) * LA_CHUNK)
            cross_l[a] = [_dot(qd_l[h][ra], s[h].astype(BF16)) for h in heads]
            upd_l = [_dot(kdt_l[h], _chunk_halves(v_l[h][ra], a).astype(BF16)) for h in heads]
            s = [s[h] * cd_ref[h] + upd_l[h] for h in heads]

        for h in heads:
            cs = slice(h * HEAD_DIM, (h + 1) * HEAD_DIM)
            o = inner_l[h] + jnp.concatenate([cross_l[0][h], cross_l[1][h]], axis=0)
            o = o * lax.rsqrt(jnp.mean(o * o, axis=-1, keepdims=True) + EPS)
            o_ref[rows, cs] = (_silu(g_ref[rows, cs].astype(F32)) * o).astype(o_ref.dtype)

    for h in heads:
        s_ref[h] = s[h]


def _ret(proj, cast_arrays, *, batch, seq, tl=256):
    nblk = seq // tl
    tables = _retention_tables(seq, tl)
    cast_in_specs, cast_out_specs, cast_shapes = _cast_specs(cast_arrays, batch * nblk, lambda b, i: b * nblk + i)

    def seg(s_idx):
        return pl.BlockSpec((tl, QK_WIDTH), lambda b, i: (b * nblk + i, s_idx))

    def table():
        return pl.BlockSpec((N_HEADS, PAIR, HEAD_DIM), lambda b, i: (0, 0, 0))

    return pl.pallas_call(
        functools.partial(_ret_kernel, tl=tl, n_cast=len(cast_arrays)),
        grid=(batch, nblk),
        in_specs=[
            seg(4), seg(5), seg(6), seg(7),
            pl.BlockSpec((nblk, HEAD_DIM), lambda b, i: (0, 0)),
            pl.BlockSpec((nblk, HEAD_DIM), lambda b, i: (0, 0)),
            pl.BlockSpec((tl, HEAD_DIM), lambda b, i: (0, 0)),
            pl.BlockSpec((tl, HEAD_DIM), lambda b, i: (0, 0)),
            pl.BlockSpec((tl, HEAD_DIM), lambda b, i: (0, 0)),
            pl.BlockSpec((tl, HEAD_DIM), lambda b, i: (0, 0)),
            table(), table(), table(), table(),
        ] + cast_in_specs,
        out_specs=[pl.BlockSpec((tl, QK_WIDTH), lambda b, i: (b * nblk + i, 0))] + cast_out_specs,
        out_shape=[jax.ShapeDtypeStruct((batch * seq, QK_WIDTH), BF16)] + cast_shapes,
        scratch_shapes=[pltpu.VMEM((N_HEADS, HEAD_DIM, HEAD_DIM), F32)],
        compiler_params=_params(2, 32),
        name="retention",
    )(proj, proj, proj, proj, *tables, *[a for a, _ in cast_arrays])


def _sg_gate_kernel(u_ref, v_ref, lnw_ref, lnb_ref, ws_ref, bs_ref, o_ref, vn_ref, *, tm):
    lnw = lnw_ref[...]
    lnb = lnb_ref[...]
    step = 16

    def body(r, carry):
        sl = pl.ds(pl.multiple_of(r * step, step), step)
        x = v_ref[sl, :].astype(F32)
        mu = jnp.mean(x, axis=-1, keepdims=True)
        xc = x - mu
        var = jnp.mean(xc * xc, axis=-1, keepdims=True)
        vn_ref[sl, :] = (xc * lax.rsqrt(var + EPS) * lnw + lnb).astype(vn_ref.dtype)
        return carry

    lax.fori_loop(0, tm // step, body, 0, unroll=ROW_LOOP_UNROLL)

    row = lax.broadcasted_iota(jnp.int32, (SG_CHUNK, SG_CHUNK), 0)
    col = lax.broadcasted_iota(jnp.int32, (SG_CHUNK, SG_CHUNK), 1)
    for g in range(SG_GROUPS):
        wg = jnp.where(col <= row, ws_ref[g], 0.0).astype(BF16)
        bias = bs_ref[:, g:g + 1]
        cols = slice(g * SG_GROUP_DIM, (g + 1) * SG_GROUP_DIM)
        for c in range(tm // SG_CHUNK):
            rows = slice(c * SG_CHUNK, (c + 1) * SG_CHUNK)
            sgate = _dot(wg, vn_ref[rows, cols]) + bias
            o_ref[rows, cols] = (u_ref[rows, cols].astype(F32) * sgate).astype(o_ref.dtype)


def _sg_gate(proj, lnw, lnb, ws, bs_t, *, tm=512):
    t = proj.shape[0]
    return pl.pallas_call(
        functools.partial(_sg_gate_kernel, tm=tm),
        grid=(t // tm,),
        in_specs=[
            pl.BlockSpec((tm, SG_WIDTH), lambda i: (i, 0)),
            pl.BlockSpec((tm, SG_WIDTH), lambda i: (i, 1)),
            pl.BlockSpec((1, SG_WIDTH), lambda i: (0, 0)),
            pl.BlockSpec((1, SG_WIDTH), lambda i: (0, 0)),
            pl.BlockSpec((SG_GROUPS, SG_CHUNK, SG_CHUNK), lambda i: (0, 0, 0)),
            pl.BlockSpec((SG_CHUNK, SG_GROUPS), lambda i: (0, 0)),
        ],
        out_specs=pl.BlockSpec((tm, SG_WIDTH), lambda i: (i, 0)),
        out_shape=jax.ShapeDtypeStruct((t, SG_WIDTH), BF16),
        scratch_shapes=[pltpu.VMEM((tm, SG_WIDTH), BF16)],
        compiler_params=_params(1, 40),
        name="sg_gate",
    )(proj, proj, lnw, lnb, ws, bs_t)


def _retention_tables(seq, tl):
    half = HEAD_DIM // 2
    inv_freq = 1.0 / (ROPE_BASE ** jnp.linspace(0.0, 1.0, half, dtype=F32))
    base = (jnp.arange(seq // tl, dtype=F32) * tl)[:, None] * inv_freq[None, :]
    off = jnp.arange(tl, dtype=F32)[:, None] * inv_freq[None, :]

    def both_halves(a):
        return jnp.concatenate([a, a], axis=1)

    sign = jnp.concatenate([-jnp.ones((half,), F32), jnp.ones((half,), F32)])[None, :]
    cos_a, sin_a = both_halves(jnp.cos(base)), both_halves(jnp.sin(base))
    cos_b, sin_b = both_halves(jnp.cos(off)), both_halves(jnp.sin(off))

    log_gamma = jnp.log1p(-jnp.power(2.0, -5.0 - jnp.arange(N_HEADS, dtype=F32)))
    t = jnp.arange(PAIR)
    pos = (t % LA_CHUNK).astype(F32)
    mask = ((t[:, None] // LA_CHUNK) == (t[None, :] // LA_CHUNK)) & (t[None, :] <= t[:, None])
    dmat = jnp.exp(jnp.where(mask[None], (pos[:, None] - pos[None, :])[None] * log_gamma[:, None, None], -jnp.inf))
    full = (N_HEADS, PAIR, HEAD_DIM)
    q_scale = jnp.broadcast_to(jnp.exp((pos[None, :] + 1.0) * log_gamma[:, None])[:, :, None], full)
    k_scale = jnp.broadcast_to(jnp.exp((LA_CHUNK - 1.0 - pos[None, :]) * log_gamma[:, None])[:, :, None], full)
    chunk_decay = jnp.broadcast_to(jnp.exp(LA_CHUNK * log_gamma)[:, None, None], full)
    return cos_a, sin_a, cos_b, sin_b, cos_b * sign, sin_b * sign, dmat, q_scale, k_scale, chunk_decay


def kernel(x, norm_w, la_w_in, la_conv_w, la_a_log, la_dt_bias, la_out_norm_w, la_w_out, sg_w_in, sg_ln_w,
           sg_ln_b, sg_w_s, sg_b_s, sg_w_out, ffn_w_up, ffn_w_down):
    batch, seq, d = x.shape
    t = batch * seq
    h = x.reshape(t, d)
    gate0 = 4 * QK_WIDTH

    w_t = jnp.swapaxes(la_w_in[0], 0, 1).astype(BF16)
    proj, gates = _inproj_la(h, norm_w[0, 0][None, :], w_t, gate0=gate0, n_gate=2 * N_HEADS)

    cw = la_conv_w[0].reshape(3, QK_WIDTH, CONV_WIDTH).transpose(0, 2, 1)
    gp = jnp.zeros((8, LANES), F32)
    gp = gp.at[0, N_HEADS:2 * N_HEADS].set(la_a_log[0]).at[1, N_HEADS:2 * N_HEADS].set(la_dt_bias[0])
    n_layers, _, hidden = ffn_w_up.shape
    o_a, w_up, w_down = _gdn(proj, gates, cw, gp, la_out_norm_w[0][None, :],
                             [(ffn_w_up.reshape(n_layers * d, hidden), FFN_TILE),
                              (ffn_w_down.reshape(n_layers * hidden, d), None)],
                             batch=batch, seq=seq)
    w_down = w_down.reshape(n_layers, hidden, d)
    o_b, w_out, w_sg_in, w_sg_out = _ret(proj, [(la_w_out[0], None), (sg_w_in[0], INPROJ_TILE), (sg_w_out[0], None)],
                                         batch=batch, seq=seq)

    h = _out_rms_res([o_a, o_b], [w_out, w_out], h, norm_w[0, 1][None, :], tm=512, vmem_mib=48)
    h = _ffn(h, norm_w[0, 2][None, :], w_up, w_down, norm_w[0, 3][None, :], 0)

    proj1 = _inproj_gelu(h, norm_w[1, 0][None, :], w_sg_in)
    gated = _sg_gate(proj1, sg_ln_w[0][None, :], sg_ln_b[0][None, :], sg_w_s[0], sg_b_s[0].T)
    h = _out_rms_res([gated], [w_sg_out], h, norm_w[1, 1][None, :], tm=512, vmem_mib=56)
    h = _ffn(h, norm_w[1, 2][None, :], w_up, w_down, norm_w[1, 3][None, :], 1)
    return h.reshape(batch, seq, d)
```

```python
import functools
import math

import jax
import jax.numpy as jnp
from jax import lax
from jax.experimental import pallas as pl
from jax.experimental.pallas import tpu as pltpu

F32 = jnp.float32
BF16 = jnp.bfloat16

D_MODEL = 2048
N_HEADS = 8
HEAD_DIM = 128
QK_WIDTH = N_HEADS * HEAD_DIM
LA_CHUNK = 64
PAIR = 2 * LA_CHUNK
CONV_WIDTH = 4
HALO = 8
ROPE_BASE = 10000.0
SG_CHUNK = 128
SG_GROUPS = 8
SG_WIDTH = 2 * D_MODEL
SG_GROUP_DIM = SG_WIDTH // SG_GROUPS
FFN_HIDDEN = 4 * D_MODEL
EPS = 1e-6
LANES = 128
ROW_LOOP_UNROLL = 16
LN_LOOP_UNROLL = 8
MIB = 1024 * 1024

NT_DIMS = (((1,), (1,)), ((), ()))


def _params(n_grid_axes, vmem_mib):
    return pltpu.CompilerParams(dimension_semantics=("arbitrary",) * n_grid_axes,
                                vmem_limit_bytes=vmem_mib * MIB)


def _dot(a, b):
    return jnp.dot(a, b, preferred_element_type=F32)


def _dot_nt(a, b):
    return lax.dot_general(a, b, NT_DIMS, preferred_element_type=F32)


def _silu(x):
    return x * jax.nn.sigmoid(x)


def _softplus(x):
    return jnp.maximum(x, 0.0) + jnp.log1p(jnp.exp(-jnp.abs(x)))


def _gelu_tanh(x):
    k0 = -2.0 * math.sqrt(2.0 / math.pi) * math.log2(math.e)
    k1 = k0 * 0.044715
    return x / (1.0 + jnp.exp2(x * (k0 + k1 * (x * x))))


def _rmsnorm_rows_to(src_ref, nw_ref, dst_ref, rows):
    nw = nw_ref[...]
    step = 16

    def body(r, carry):
        sl = pl.ds(pl.multiple_of(r * step, step), step)
        x = src_ref[sl, :]
        ms = jnp.mean(x * x, axis=-1, keepdims=True)
        dst_ref[sl, :] = (x * lax.rsqrt(ms + EPS) * nw).astype(dst_ref.dtype)
        return carry

    lax.fori_loop(0, rows // step, body, 0, unroll=ROW_LOOP_UNROLL)


def _rms_residual_rows(acc_ref, h_ref, nw_ref, o_ref, rows):
    nw = nw_ref[...]
    step = 16

    def body(r, carry):
        sl = pl.ds(pl.multiple_of(r * step, step), step)
        y = acc_ref[sl, :]
        ms = jnp.mean(y * y, axis=-1, keepdims=True)
        o_ref[sl, :] = h_ref[sl, :] + y * lax.rsqrt(ms + EPS) * nw
        return carry

    lax.fori_loop(0, rows // step, body, 0, unroll=ROW_LOOP_UNROLL)


def _column_chunks(width, chunk=1024):
    return [slice(c, c + chunk) for c in range(0, width, chunk)]


def _inproj_la_kernel(x_ref, nw_ref, w_ref, wg_ref, o_ref, g_ref, xn_ref, *, tm):
    @pl.when(pl.program_id(1) == 0)
    def _():
        _rmsnorm_rows_to(x_ref, nw_ref, xn_ref, tm)
        g_ref[...] = _dot_nt(xn_ref[...], wg_ref[...])

    for cols in _column_chunks(o_ref.shape[1]):
        o_ref[:, cols] = _dot_nt(xn_ref[...], w_ref[cols, :]).astype(o_ref.dtype)


def _inproj_la(x, nw, w_t, *, gate0, n_gate, tm=1024, tn=2048):
    t, k = x.shape
    n = w_t.shape[0] - n_gate
    per_part = gate0 // tn

    def w_rows(i, j):
        return (pl.multiple_of(j * tn + jnp.where(j >= per_part, n_gate, 0), n_gate), 0)

    return pl.pallas_call(
        functools.partial(_inproj_la_kernel, tm=tm),
        grid=(t // tm, n // tn),
        in_specs=[
            pl.BlockSpec((tm, k), lambda i, j: (i, 0)),
            pl.BlockSpec((1, k), lambda i, j: (0, 0)),
            pl.BlockSpec((pl.Element(tn), pl.Element(k)), w_rows),
            pl.BlockSpec((LANES, k), lambda i, j: (gate0 // LANES, 0)),
        ],
        out_specs=[
            pl.BlockSpec((tm, tn), lambda i, j: (i, j)),
            pl.BlockSpec((tm, LANES), lambda i, j: (i, 0)),
        ],
        out_shape=[jax.ShapeDtypeStruct((t, n), BF16), jax.ShapeDtypeStruct((t, LANES), F32)],
        scratch_shapes=[pltpu.VMEM((tm, k), BF16)],
        compiler_params=_params(2, 56),
        name="inproj_la",
    )(x, nw, w_t, w_t)


def _inproj_gelu_kernel(x_ref, nw_ref, w_ref, o_ref, xn_ref, *, tm):
    @pl.when(pl.program_id(1) == 0)
    def _():
        _rmsnorm_rows_to(x_ref, nw_ref, xn_ref, tm)

    for cols in _column_chunks(o_ref.shape[1]):
        o_ref[:, cols] = _gelu_tanh(_dot(xn_ref[...], w_ref[:, cols])).astype(o_ref.dtype)


def _inproj_gelu(x, nw, w, *, tm=1024, tn=2048):
    t, k = x.shape
    n = w.shape[1]
    return pl.pallas_call(
        functools.partial(_inproj_gelu_kernel, tm=tm),
        grid=(t // tm, n // tn),
        in_specs=[
            pl.BlockSpec((tm, k), lambda i, j: (i, 0)),
            pl.BlockSpec((1, k), lambda i, j: (0, 0)),
            pl.BlockSpec((k, tn), lambda i, j: (0, j)),
        ],
        out_specs=pl.BlockSpec((tm, tn), lambda i, j: (i, j)),
        out_shape=jax.ShapeDtypeStruct((t, n), BF16),
        scratch_shapes=[pltpu.VMEM((tm, k), BF16)],
        compiler_params=_params(2, 56),
        name="inproj_gelu",
    )(x, nw, w)


def _out_rms_res_kernel(*refs, n_pairs, tm):
    a_refs = refs[:n_pairs]
    w_refs = refs[n_pairs:2 * n_pairs]
    h_ref, nw_ref, o_ref, acc_ref = refs[2 * n_pairs:]
    y = _dot(a_refs[0][...], w_refs[0][...])
    for a_ref, w_ref in zip(a_refs[1:], w_refs[1:]):
        y = y + _dot(a_ref[...], w_ref[...])
    acc_ref[...] = y
    _rms_residual_rows(acc_ref, h_ref, nw_ref, o_ref, tm)


def _out_rms_res(a_list, w_list, h, nw, *, tm, vmem_mib):
    t, n = h.shape
    n_pairs = len(a_list)
    in_specs = [pl.BlockSpec((tm, a.shape[1]), lambda i: (i, 0)) for a in a_list]
    k0 = 0
    for a, w in zip(a_list, w_list):
        ka = a.shape[1]
        row_block = k0 // ka if w.shape[0] != ka else 0
        in_specs.append(pl.BlockSpec((ka, n), lambda i, rb=row_block: (rb, 0)))
        k0 += ka
    in_specs += [pl.BlockSpec((tm, n), lambda i: (i, 0)), pl.BlockSpec((1, n), lambda i: (0, 0))]
    return pl.pallas_call(
        functools.partial(_out_rms_res_kernel, n_pairs=n_pairs, tm=tm),
        grid=(t // tm,),
        in_specs=in_specs,
        out_specs=pl.BlockSpec((tm, n), lambda i: (i, 0)),
        out_shape=jax.ShapeDtypeStruct((t, n), F32),
        scratch_shapes=[pltpu.VMEM((tm, n), F32)],
        compiler_params=_params(1, vmem_mib),
        name="out_rms_res",
    )(*a_list, *w_list, h, nw)


def _ffn_kernel(h_ref, h_next_ref, nw_in_ref, wup_ref, wdn_ref, nw_out_ref, o_ref, xn_ref, acc_ref, *, tm, n_j):
    i = pl.program_id(0)
    j = pl.program_id(1)
    slot = lax.rem(i, 2)

    @pl.when((i == 0) & (j == 0))
    def _():
        _rmsnorm_rows_to(h_ref, nw_in_ref, xn_ref.at[0], tm)

    def step(first):
        u = _dot(xn_ref[slot], wup_ref[...])
        u = jnp.square(jnp.maximum(u, 0.0)).astype(BF16)
        down = _dot(u, wdn_ref[...])
        if first:
            acc_ref[...] = down
        else:
            acc_ref[...] += down

        nw_in = nw_in_ref[...]
        share = tm // n_j
        xn_next_ref = xn_ref.at[1 - slot]
        for g in range(share // 16):
            sl = pl.ds(pl.multiple_of(j * share + g * 16, 16), 16)
            x = h_next_ref[sl, :]
            ms = jnp.mean(x * x, axis=-1, keepdims=True)
            xn_next_ref[sl, :] = (x * lax.rsqrt(ms + EPS) * nw_in).astype(BF16)

    pl.when(j == 0)(functools.partial(step, True))
    pl.when(j > 0)(functools.partial(step, False))

    @pl.when(j == n_j - 1)
    def _():
        _rms_residual_rows(acc_ref, h_ref, nw_out_ref, o_ref, tm)


def _ffn(h, nw_in, wup, wdn, nw_out, layer, *, tm=512, th=1024):
    t, d = h.shape
    hidden = wup.shape[2]
    n_i, n_j = t // tm, hidden // th
    return pl.pallas_call(
        functools.partial(_ffn_kernel, tm=tm, n_j=n_j),
        grid=(n_i, n_j),
        in_specs=[
            pl.BlockSpec((tm, d), lambda i, j: (i, 0)),
            pl.BlockSpec((tm, d), lambda i, j: (jnp.minimum(i + 1, n_i - 1), 0)),
            pl.BlockSpec((1, d), lambda i, j: (0, 0)),
            pl.BlockSpec((None, d, th), lambda i, j: (layer, 0, j)),
            pl.BlockSpec((None, th, d), lambda i, j: (layer, j, 0)),
            pl.BlockSpec((1, d), lambda i, j: (0, 0)),
        ],
        out_specs=pl.BlockSpec((tm, d), lambda i, j: (i, 0)),
        out_shape=jax.ShapeDtypeStruct((t, d), F32),
        scratch_shapes=[pltpu.VMEM((2, tm, d), BF16), pltpu.VMEM((tm, d), F32)],
        compiler_params=_params(2, 56),
        name="ffn",
    )(h, h, nw_in, wup, wdn, nw_out)


def _cast_specs(arrays, n_steps, step_index):
    in_specs, out_specs, out_shapes = [], [], []
    for a in arrays:
        rows, cols = a.shape[0] // n_steps, a.shape[1]
        in_specs.append(pl.BlockSpec((rows, cols), lambda b, i: (step_index(b, i), 0)))
        out_specs.append(pl.BlockSpec((rows, cols), lambda b, i: (step_index(b, i), 0)))
        out_shapes.append(jax.ShapeDtypeStruct(a.shape, BF16))
    return in_specs, out_specs, out_shapes


def _cast_slabs(in_refs, out_refs):
    for in_ref, out_ref in zip(in_refs, out_refs):
        out_ref[...] = in_ref[...].astype(out_ref.dtype)


def _pair_masks():
    row = lax.broadcasted_iota(jnp.int32, (PAIR, PAIR), 0)
    col = lax.broadcasted_iota(jnp.int32, (PAIR, PAIR), 1)
    chunk_bits = LA_CHUNK.bit_length() - 1
    same_chunk = (row >> chunk_bits) == (col >> chunk_bits)
    return row, col, same_chunk


def _chunk_halves(parts, a):
    zeros = jnp.zeros_like(parts)
    return jnp.concatenate([parts, zeros] if a == 0 else [zeros, parts], axis=0)


N_GDN_IN = 8
N_RET_IN = 14


def _mixer_kernel(*refs, tl, n_cast):
    gdn_in, ret_in, rest = refs[:N_GDN_IN], refs[N_GDN_IN:N_GDN_IN + N_RET_IN], refs[N_GDN_IN + N_RET_IN:]
    cast_in, (oa_ref, ob_ref), cast_out = rest[:n_cast], rest[n_cast:n_cast + 2], rest[n_cast + 2:2 * n_cast + 2]
    xpad_ref, s_gdn_ref, s_ret_ref = rest[2 * n_cast + 2:]

    @pl.when(pl.program_id(1) == 0)
    def _():
        _gdn_reset(xpad_ref, s_gdn_ref)
        s_ret_ref[...] = jnp.zeros_like(s_ret_ref)

    _cast_slabs(cast_in, cast_out)
    _gdn_body(*gdn_in, oa_ref, xpad_ref, s_gdn_ref, tl=tl)
    _ret_body(*ret_in, ob_ref, s_ret_ref, tl=tl)


def _gdn_reset(xpad_ref, s_ref):
    xpad_ref[:, 0:HALO, :] = jnp.zeros((3, HALO, QK_WIDTH), F32)
    s_ref[...] = jnp.zeros_like(s_ref)


def _gdn_body(q_ref, k_ref, v_ref, z_ref, gate_ref, cw_ref, gp_ref, onw_ref, o_ref,
              xpad_ref, s_ref, *, tl):
    nh = N_HEADS
    heads = range(nh)

    xpad_ref[0, HALO:HALO + tl, :] = q_ref[...].astype(F32)
    xpad_ref[1, HALO:HALO + tl, :] = k_ref[...].astype(F32)
    xpad_ref[2, HALO:HALO + tl, :] = v_ref[...].astype(F32)

    row, col, same_chunk = _pair_masks()
    causal = same_chunk & (col <= row)
    strict = same_chunk & (col < row)
    eye = (row == col).astype(F32)
    row_in_chunk = row & (LA_CHUNK - 1)
    lane8 = lax.broadcasted_iota(jnp.int32, (nh, PAIR), 1)

    a_coef = -jnp.exp(gp_ref[0:1, :])
    dt_bias = gp_ref[1:2, :]
    onw = onw_ref[...]
    s = [s_ref[h] for h in heads]

    for p in range(tl // PAIR):
        r0 = p * PAIR
        rows = slice(r0, r0 + PAIR)

        pre = gate_ref[rows, :]
        beta_all = jax.nn.sigmoid(pre)
        g = a_coef * _softplus(pre + dt_bias)
        for sh in (1, 2, 4, 8, 16, 32):
            g = g + jnp.where(row_in_chunk >= sh, pltpu.roll(g, sh, axis=0), 0.0)
        gt8 = g.T[nh:2 * nh, :]
        g_last = [jnp.sum(jnp.where(lane8 == (a + 1) * LA_CHUNK - 1, gt8, 0.0), axis=1, keepdims=True)
                  for a in range(2)]
        k_tail8 = jnp.exp(jnp.where(lane8 < LA_CHUNK, g_last[0], g_last[1]) - gt8)
        chunk_decay = [jnp.exp(gl) for gl in g_last]

        def conv(ci, cs):
            w = cw_ref[ci, :, cs]
            acc = xpad_ref[ci, HALO + r0:HALO + r0 + PAIR, cs] * w[CONV_WIDTH - 1:CONV_WIDTH, :]
            for j in range(1, CONV_WIDTH):
                acc = acc + xpad_ref[ci, HALO + r0 - j:HALO + r0 - j + PAIR, cs] * w[CONV_WIDTH - 1 - j:CONV_WIDTH - j, :]
            return _silu(acc)

        q_l, k_l, rhs_l, kbf_l, decay_l, m1_l, lhs_l = [], [], [], [], [], [], []
        for h in heads:
            cs = slice(h * HEAD_DIM, (h + 1) * HEAD_DIM)
            q = conv(0, cs)
            k = conv(1, cs)
            v = conv(2, cs)
            q = q * lax.rsqrt(jnp.sum(q * q, axis=-1, keepdims=True) + 1e-6) * (HEAD_DIM ** -0.5)
            k = k * lax.rsqrt(jnp.sum(k * k, axis=-1, keepdims=True) + 1e-6)
            gc_col = jnp.sum(jnp.where(col == nh + h, g, 0.0), axis=1, keepdims=True)
            beta_col = jnp.sum(jnp.where(col == h, beta_all, 0.0), axis=1, keepdims=True)
            decay = jnp.exp(jnp.where(causal, gc_col - gt8[h:h + 1, :], -jnp.inf))
            e_col = jnp.exp(gc_col)
            q_l.append(q * e_col)
            k_l.append(k)
            rhs_l.append(jnp.concatenate([v * beta_col, k * (beta_col * e_col)], axis=1).astype(BF16))
            kbf = k.astype(BF16)
            kbf_l.append(kbf)
            decay_l.append(decay)
            m1_l.append(beta_col * decay)
            lhs_l.append(jnp.concatenate([q.astype(BF16), kbf], axis=0))

        r_l = [_dot_nt(lhs_l[h], kbf_l[h]) for h in heads]
        qk_l = [(r_l[h][0:PAIR] * decay_l[h]).astype(BF16) for h in heads]
        a_l = [jnp.where(strict, r_l[h][PAIR:2 * PAIR] * m1_l[h], 0.0) for h in heads]
        x_l = [eye - a_l[h] for h in heads]
        pb_l = [a_l[h].astype(BF16) for h in heads]
        for _ in range(5):
            pb_l = [_dot(pb_l[h], pb_l[h]).astype(BF16) for h in heads]
            x_l = [x_l[h] + _dot(x_l[h].astype(BF16), pb_l[h]) for h in heads]

        uw_l = [_dot(x_l[h].astype(BF16), rhs_l[h]) for h in heads]
        u_l = [uw_l[h][:, 0:HEAD_DIM] for h in heads]
        w_l = [uw_l[h][:, HEAD_DIM:2 * HEAD_DIM] for h in heads]
        kt_l = [(k_l[h].T * k_tail8[h:h + 1, :]).astype(BF16) for h in heads]

        v_new_l = [[], []]
        qs_l = [[], []]
        for a in range(2):
            ra = slice(a * LA_CHUNK, (a + 1) * LA_CHUNK)
            r2_l = [_dot(jnp.concatenate([w_l[h][ra], q_l[h][ra]], axis=0).astype(BF16), s[h].astype(BF16))
                    for h in heads]
            for h in heads:
                v_new = u_l[h][ra] - r2_l[h][0:LA_CHUNK]
                v_new_l[a].append(v_new)
                qs_l[a].append(r2_l[h][LA_CHUNK:2 * LA_CHUNK])
            upd_l = [_dot(kt_l[h], _chunk_halves(v_new_l[a][h], a).astype(BF16)) for h in heads]
            s = [s[h] * chunk_decay[a][h:h + 1, :] + upd_l[h] for h in heads]

        o_l = [jnp.concatenate([qs_l[0][h], qs_l[1][h]], axis=0)
               + _dot(qk_l[h], jnp.concatenate([v_new_l[0][h], v_new_l[1][h]], axis=0).astype(BF16))
               for h in heads]
        for h in heads:
            cs = slice(h * HEAD_DIM, (h + 1) * HEAD_DIM)
            o = o_l[h]
            o = o * lax.rsqrt(jnp.mean(o * o, axis=-1, keepdims=True) + EPS) * onw * _silu(z_ref[rows, cs].astype(F32))
            o_ref[rows, cs] = o.astype(o_ref.dtype)

    for h in heads:
        s_ref[h] = s[h]
    xpad_ref[:, 0:HALO, :] = xpad_ref[:, tl:tl + HALO, :]


def _ret_body(q_ref, k_ref, v_ref, g_ref, cos_a_ref, sin_a_ref, cos_b_ref, sin_b_ref, cos_bs_ref, sin_bs_ref,
              dmat_ref, qs_ref, ks_ref, cd_ref, o_ref, s_ref, *, tl):
    heads = range(N_HEADS)
    blk = pl.ds(pl.program_id(1), 1)
    cos_a = cos_a_ref[blk, :]
    sin_a = sin_a_ref[blk, :]

    s = [s_ref[h] for h in heads]

    for p in range(tl // PAIR):
        rows = slice(p * PAIR, (p + 1) * PAIR)
        cosf = cos_a * cos_b_ref[rows, :] - sin_a * sin_b_ref[rows, :]
        sinf = sin_a * cos_bs_ref[rows, :] + cos_a * sin_bs_ref[rows, :]

        q_l, kb_l, kdt_l, v_l = [], [], [], []
        for h in heads:
            cs = slice(h * HEAD_DIM, (h + 1) * HEAD_DIM)
            q = q_ref[rows, cs].astype(F32)
            k = k_ref[rows, cs].astype(F32)
            q = q * cosf + pltpu.roll(q, HEAD_DIM // 2, axis=1) * sinf
            k = (k * cosf + pltpu.roll(k, HEAD_DIM // 2, axis=1) * sinf) * (HEAD_DIM ** -0.5)
            q_l.append(q)
            kb_l.append(k.astype(BF16))
            kdt_l.append((k * ks_ref[h]).T.astype(BF16))
            v_l.append(v_ref[rows, cs])

        qk_l = [(_dot_nt(q_l[h].astype(BF16), kb_l[h]) * dmat_ref[h]).astype(BF16) for h in heads]
        inner_l = [_dot(qk_l[h], v_l[h].astype(BF16)) for h in heads]
        qd_l = [(q_l[h] * qs_ref[h]).astype(BF16) for h in heads]

        cross_l = [[], []]
        for a in range(2):
            ra = slice(a * LA_CHUNK, (a + 1) * LA_CHUNK)
            cross_l[a] = [_dot(qd_l[h][ra], s[h].astype(BF16)) for h in heads]
            upd_l = [_dot(kdt_l[h], _chunk_halves(v_l[h][ra], a).astype(BF16)) for h in heads]
            s = [s[h] * cd_ref[h] + upd_l[h] for h in heads]

        for h in heads:
            cs = slice(h * HEAD_DIM, (h + 1) * HEAD_DIM)
            o = inner_l[h] + jnp.concatenate([cross_l[0][h], cross_l[1][h]], axis=0)
            o = o * lax.rsqrt(jnp.mean(o * o, axis=-1, keepdims=True) + EPS)
            o_ref[rows, cs] = (_silu(g_ref[rows, cs].astype(F32)) * o).astype(o_ref.dtype)

    for h in heads:
        s_ref[h] = s[h]


def _mixer(proj, gates, cw, gp, onw, cast_arrays, *, batch, seq, tl=256):
    nblk = seq // tl
    tables = _retention_tables(seq, tl)
    cast_in_specs, cast_out_specs, cast_shapes = _cast_specs(cast_arrays, batch * nblk, lambda b, i: b * nblk + i)

    def seg(s_idx):
        return pl.BlockSpec((tl, QK_WIDTH), lambda b, i: (b * nblk + i, s_idx))

    def const(shape):
        return pl.BlockSpec(shape, lambda b, i: (0,) * len(shape))

    head_table = (N_HEADS, PAIR, HEAD_DIM)
    out_spec = pl.BlockSpec((tl, QK_WIDTH), lambda b, i: (b * nblk + i, 0))
    out_shape = jax.ShapeDtypeStruct((batch * seq, QK_WIDTH), BF16)
    return pl.pallas_call(
        functools.partial(_mixer_kernel, tl=tl, n_cast=len(cast_arrays)),
        grid=(batch, nblk),
        in_specs=[
            seg(0), seg(1), seg(2), seg(3),
            pl.BlockSpec((tl, LANES), lambda b, i: (b * nblk + i, 0)),
            const((3, CONV_WIDTH, QK_WIDTH)), const((8, LANES)), const((1, HEAD_DIM)),
            seg(4), seg(5), seg(6), seg(7),
            const((nblk, HEAD_DIM)), const((nblk, HEAD_DIM)),
            const((tl, HEAD_DIM)), const((tl, HEAD_DIM)), const((tl, HEAD_DIM)), const((tl, HEAD_DIM)),
            const(head_table), const(head_table), const(head_table), const(head_table),
        ] + cast_in_specs,
        out_specs=[out_spec, out_spec] + cast_out_specs,
        out_shape=[out_shape, out_shape] + cast_shapes,
        scratch_shapes=[pltpu.VMEM((3, HALO + tl, QK_WIDTH), F32),
                        pltpu.VMEM((N_HEADS, HEAD_DIM, HEAD_DIM), F32),
                        pltpu.VMEM((N_HEADS, HEAD_DIM, HEAD_DIM), F32)],
        compiler_params=_params(2, 56),
        name="mixer",
    )(proj, proj, proj, proj, gates, cw, gp, onw, proj, proj, proj, proj, *tables, *cast_arrays)


def _sg_gate_kernel(u_ref, v_ref, lnw_ref, lnb_ref, ws_ref, bs_ref, o_ref, vn_ref, *, tm):
    lnw = lnw_ref[...]
    lnb = lnb_ref[...]
    step = 16

    def body(r, carry):
        sl = pl.ds(pl.multiple_of(r * step, step), step)
        x = v_ref[sl, :].astype(F32)
        mu = jnp.mean(x, axis=-1, keepdims=True)
        xc = x - mu
        var = jnp.mean(xc * xc, axis=-1, keepdims=True)
        vn_ref[sl, :] = (xc * lax.rsqrt(var + EPS) * lnw + lnb).astype(vn_ref.dtype)
        return carry

    lax.fori_loop(0, tm // step, body, 0, unroll=LN_LOOP_UNROLL)

    row = lax.broadcasted_iota(jnp.int32, (SG_CHUNK, SG_CHUNK), 0)
    col = lax.broadcasted_iota(jnp.int32, (SG_CHUNK, SG_CHUNK), 1)
    for g in range(SG_GROUPS):
        wg = jnp.where(col <= row, ws_ref[g], 0.0).astype(BF16)
        bias = bs_ref[:, g:g + 1]
        cols = slice(g * SG_GROUP_DIM, (g + 1) * SG_GROUP_DIM)
        for c in range(tm // SG_CHUNK):
            rows = slice(c * SG_CHUNK, (c + 1) * SG_CHUNK)
            sgate = _dot(wg, vn_ref[rows, cols]) + bias
            o_ref[rows, cols] = (u_ref[rows, cols].astype(F32) * sgate).astype(o_ref.dtype)


def _sg_gate(proj, lnw, lnb, ws, bs_t, *, tm=512):
    t = proj.shape[0]
    return pl.pallas_call(
        functools.partial(_sg_gate_kernel, tm=tm),
        grid=(t // tm,),
        in_specs=[
            pl.BlockSpec((tm, SG_WIDTH), lambda i: (i, 0)),
            pl.BlockSpec((tm, SG_WIDTH), lambda i: (i, 1)),
            pl.BlockSpec((1, SG_WIDTH), lambda i: (0, 0)),
            pl.BlockSpec((1, SG_WIDTH), lambda i: (0, 0)),
            pl.BlockSpec((SG_GROUPS, SG_CHUNK, SG_CHUNK), lambda i: (0, 0, 0)),
            pl.BlockSpec((SG_CHUNK, SG_GROUPS), lambda i: (0, 0)),
        ],
        out_specs=pl.BlockSpec((tm, SG_WIDTH), lambda i: (i, 0)),
        out_shape=jax.ShapeDtypeStruct((t, SG_WIDTH), BF16),
        scratch_shapes=[pltpu.VMEM((tm, SG_WIDTH), BF16)],
        compiler_params=_params(1, 40),
        name="sg_gate",
    )(proj, proj, lnw, lnb, ws, bs_t)


def _retention_tables(seq, tl):
    half = HEAD_DIM // 2
    inv_freq = 1.0 / (ROPE_BASE ** jnp.linspace(0.0, 1.0, half, dtype=F32))
    base = (jnp.arange(seq // tl, dtype=F32) * tl)[:, None] * inv_freq[None, :]
    off = jnp.arange(tl, dtype=F32)[:, None] * inv_freq[None, :]

    def both_halves(a):
        return jnp.concatenate([a, a], axis=1)

    sign = jnp.concatenate([-jnp.ones((half,), F32), jnp.ones((half,), F32)])[None, :]
    cos_a, sin_a = both_halves(jnp.cos(base)), both_halves(jnp.sin(base))
    cos_b, sin_b = both_halves(jnp.cos(off)), both_halves(jnp.sin(off))

    log_gamma = jnp.log1p(-jnp.power(2.0, -5.0 - jnp.arange(N_HEADS, dtype=F32)))
    t = jnp.arange(PAIR)
    pos = (t % LA_CHUNK).astype(F32)
    mask = ((t[:, None] // LA_CHUNK) == (t[None, :] // LA_CHUNK)) & (t[None, :] <= t[:, None])
    dmat = jnp.exp(jnp.where(mask[None], (pos[:, None] - pos[None, :])[None] * log_gamma[:, None, None], -jnp.inf))
    full = (N_HEADS, PAIR, HEAD_DIM)
    q_scale = jnp.broadcast_to(jnp.exp((pos[None, :] + 1.0) * log_gamma[:, None])[:, :, None], full)
    k_scale = jnp.broadcast_to(jnp.exp((LA_CHUNK - 1.0 - pos[None, :]) * log_gamma[:, None])[:, :, None], full)
    chunk_decay = jnp.broadcast_to(jnp.exp(LA_CHUNK * log_gamma)[:, None, None], full)
    return cos_a, sin_a, cos_b, sin_b, cos_b * sign, sin_b * sign, dmat, q_scale, k_scale, chunk_decay


def kernel(x, norm_w, la_w_in, la_conv_w, la_a_log, la_dt_bias, la_out_norm_w, la_w_out, sg_w_in, sg_ln_w,
           sg_ln_b, sg_w_s, sg_b_s, sg_w_out, ffn_w_up, ffn_w_down):
    batch, seq, d = x.shape
    t = batch * seq
    h = x.reshape(t, d)
    gate0 = 4 * QK_WIDTH

    w_t = jnp.swapaxes(la_w_in[0], 0, 1).astype(BF16)
    proj, gates = _inproj_la(h, norm_w[0, 0][None, :], w_t, gate0=gate0, n_gate=2 * N_HEADS)

    cw = la_conv_w[0].reshape(3, QK_WIDTH, CONV_WIDTH).transpose(0, 2, 1)
    gp = jnp.zeros((8, LANES), F32)
    gp = gp.at[0, N_HEADS:2 * N_HEADS].set(la_a_log[0]).at[1, N_HEADS:2 * N_HEADS].set(la_dt_bias[0])
    n_layers, _, hidden = ffn_w_up.shape
    o_a, o_b, w_up, w_down, w_out, w_sg_in, w_sg_out = _mixer(
        proj, gates, cw, gp, la_out_norm_w[0][None, :],
        [ffn_w_up.reshape(n_layers * d, hidden), ffn_w_down.reshape(n_layers * hidden, d),
         la_w_out[0], sg_w_in[0], sg_w_out[0]], batch=batch, seq=seq)
    w_up = w_up.reshape(n_layers, d, hidden)
    w_down = w_down.reshape(n_layers, hidden, d)

    h = _out_rms_res([o_a, o_b], [w_out, w_out], h, norm_w[0, 1][None, :], tm=512, vmem_mib=48)
    h = _ffn(h, norm_w[0, 2][None, :], w_up, w_down, norm_w[0, 3][None, :], 0)

    proj1 = _inproj_gelu(h, norm_w[1, 0][None, :], w_sg_in)
    gated = _sg_gate(proj1, sg_ln_w[0][None, :], sg_ln_b[0][None, :], sg_w_s[0], sg_b_s[0].T)
    h = _out_rms_res([gated], [w_sg_out], h, norm_w[1, 1][None, :], tm=512, vmem_mib=56)
    h = _ffn(h, norm_w[1, 2][None, :], w_up, w_down, norm_w[1, 3][None, :], 1)
    return h.reshape(batch, seq, d)
```

```python
import functools
import math

import jax
import jax.numpy as jnp
from jax import lax
from jax.experimental import pallas as pl
from jax.experimental.pallas import tpu as pltpu

F32 = jnp.float32
BF16 = jnp.bfloat16

D_MODEL = 2048
N_HEADS = 8
HEAD_DIM = 128
QK_WIDTH = N_HEADS * HEAD_DIM
LA_CHUNK = 64
PAIR = 2 * LA_CHUNK
CONV_WIDTH = 4
HALO = 8
ROPE_BASE = 10000.0
SG_CHUNK = 128
SG_GROUPS = 8
SG_WIDTH = 2 * D_MODEL
SG_GROUP_DIM = SG_WIDTH // SG_GROUPS
FFN_HIDDEN = 4 * D_MODEL
EPS = 1e-6
LANES = 128
ROW_LOOP_UNROLL = 8
MIB = 1024 * 1024

NT_DIMS = (((1,), (1,)), ((), ()))


def _params(n_grid_axes, vmem_mib):
    return pltpu.CompilerParams(dimension_semantics=("arbitrary",) * n_grid_axes,
                                vmem_limit_bytes=vmem_mib * MIB)


def _dot(a, b):
    return jnp.dot(a, b, preferred_element_type=F32)


def _dot_nt(a, b):
    return lax.dot_general(a, b, NT_DIMS, preferred_element_type=F32)


def _silu(x):
    return x * jax.nn.sigmoid(x)


def _softplus(x):
    return jnp.maximum(x, 0.0) + jnp.log1p(jnp.exp(-jnp.abs(x)))


def _gelu_tanh(x):
    k0 = -2.0 * math.sqrt(2.0 / math.pi) * math.log2(math.e)
    k1 = k0 * 0.044715
    return x / (1.0 + jnp.exp2(x * (k0 + k1 * (x * x))))


def _rmsnorm_rows_to(src_ref, nw_ref, dst_ref, rows):
    nw = nw_ref[...]
    step = 16

    def body(r, carry):
        sl = pl.ds(pl.multiple_of(r * step, step), step)
        x = src_ref[sl, :]
        ms = jnp.mean(x * x, axis=-1, keepdims=True)
        dst_ref[sl, :] = (x * lax.rsqrt(ms + EPS) * nw).astype(dst_ref.dtype)
        return carry

    lax.fori_loop(0, rows // step, body, 0, unroll=ROW_LOOP_UNROLL)


def _rms_residual_rows(acc_ref, h_ref, nw_ref, o_ref, rows):
    nw = nw_ref[...]
    step = 16

    def body(r, carry):
        sl = pl.ds(pl.multiple_of(r * step, step), step)
        y = acc_ref[sl, :]
        ms = jnp.mean(y * y, axis=-1, keepdims=True)
        o_ref[sl, :] = h_ref[sl, :] + y * lax.rsqrt(ms + EPS) * nw
        return carry

    lax.fori_loop(0, rows // step, body, 0, unroll=ROW_LOOP_UNROLL)


def _column_chunks(width, chunk=1024):
    return [slice(c, c + chunk) for c in range(0, width, chunk)]


def _inproj_la_kernel(x_ref, nw_ref, w_ref, wg_ref, o_ref, g_ref, xn_ref, *, tm):
    @pl.when(pl.program_id(1) == 0)
    def _():
        _rmsnorm_rows_to(x_ref, nw_ref, xn_ref, tm)
        g_ref[...] = _dot_nt(xn_ref[...], wg_ref[...])

    for cols in _column_chunks(o_ref.shape[1]):
        o_ref[:, cols] = _dot_nt(xn_ref[...], w_ref[cols, :]).astype(o_ref.dtype)


def _inproj_la(x, nw, w_t, *, gate0, n_gate, tm=1024, tn=2048):
    t, k = x.shape
    n = w_t.shape[0] - n_gate
    per_part = gate0 // tn

    def w_rows(i, j):
        return (pl.multiple_of(j * tn + jnp.where(j >= per_part, n_gate, 0), n_gate), 0)

    return pl.pallas_call(
        functools.partial(_inproj_la_kernel, tm=tm),
        grid=(t // tm, n // tn),
        in_specs=[
            pl.BlockSpec((tm, k), lambda i, j: (i, 0)),
            pl.BlockSpec((1, k), lambda i, j: (0, 0)),
            pl.BlockSpec((pl.Element(tn), pl.Element(k)), w_rows),
            pl.BlockSpec((LANES, k), lambda i, j: (gate0 // LANES, 0)),
        ],
        out_specs=[
            pl.BlockSpec((tm, tn), lambda i, j: (i, j)),
            pl.BlockSpec((tm, LANES), lambda i, j: (i, 0)),
        ],
        out_shape=[jax.ShapeDtypeStruct((t, n), BF16), jax.ShapeDtypeStruct((t, LANES), F32)],
        scratch_shapes=[pltpu.VMEM((tm, k), BF16)],
        compiler_params=_params(2, 56),
        name="inproj_la",
    )(x, nw, w_t, w_t)


def _inproj_gelu_kernel(x_ref, nw_ref, w_ref, o_ref, xn_ref, *, tm):
    @pl.when(pl.program_id(1) == 0)
    def _():
        _rmsnorm_rows_to(x_ref, nw_ref, xn_ref, tm)

    for cols in _column_chunks(o_ref.shape[1]):
        o_ref[:, cols] = _gelu_tanh(_dot(xn_ref[...], w_ref[:, cols])).astype(o_ref.dtype)


def _inproj_gelu(x, nw, w, *, tm=1024, tn=2048):
    t, k = x.shape
    n = w.shape[1]
    return pl.pallas_call(
        functools.partial(_inproj_gelu_kernel, tm=tm),
        grid=(t // tm, n // tn),
        in_specs=[
            pl.BlockSpec((tm, k), lambda i, j: (i, 0)),
            pl.BlockSpec((1, k), lambda i, j: (0, 0)),
            pl.BlockSpec((k, tn), lambda i, j: (0, j)),
        ],
        out_specs=pl.BlockSpec((tm, tn), lambda i, j: (i, j)),
        out_shape=jax.ShapeDtypeStruct((t, n), BF16),
        scratch_shapes=[pltpu.VMEM((tm, k), BF16)],
        compiler_params=_params(2, 56),
        name="inproj_gelu",
    )(x, nw, w)


def _out_rms_res_kernel(*refs, n_pairs, tm):
    a_refs = refs[:n_pairs]
    w_refs = refs[n_pairs:2 * n_pairs]
    h_ref, nw_ref, o_ref, acc_ref = refs[2 * n_pairs:]
    y = _dot(a_refs[0][...], w_refs[0][...])
    for a_ref, w_ref in zip(a_refs[1:], w_refs[1:]):
        y = y + _dot(a_ref[...], w_ref[...])
    acc_ref[...] = y
    _rms_residual_rows(acc_ref, h_ref, nw_ref, o_ref, tm)


def _out_rms_res(a_list, w_list, h, nw, *, tm, vmem_mib):
    t, n = h.shape
    n_pairs = len(a_list)
    in_specs = [pl.BlockSpec((tm, a.shape[1]), lambda i: (i, 0)) for a in a_list]
    k0 = 0
    for a, w in zip(a_list, w_list):
        ka = a.shape[1]
        row_block = k0 // ka if w.shape[0] != ka else 0
        in_specs.append(pl.BlockSpec((ka, n), lambda i, rb=row_block: (rb, 0)))
        k0 += ka
    in_specs += [pl.BlockSpec((tm, n), lambda i: (i, 0)), pl.BlockSpec((1, n), lambda i: (0, 0))]
    return pl.pallas_call(
        functools.partial(_out_rms_res_kernel, n_pairs=n_pairs, tm=tm),
        grid=(t // tm,),
        in_specs=in_specs,
        out_specs=pl.BlockSpec((tm, n), lambda i: (i, 0)),
        out_shape=jax.ShapeDtypeStruct((t, n), F32),
        scratch_shapes=[pltpu.VMEM((tm, n), F32)],
        compiler_params=_params(1, vmem_mib),
        name="out_rms_res",
    )(*a_list, *w_list, h, nw)


def _ffn_kernel(h_ref, h_next_ref, nw_in_ref, wup_ref, wdn_ref, nw_out_ref, o_ref, xn_ref, acc_ref, *, tm, n_j):
    i = pl.program_id(0)
    j = pl.program_id(1)
    slot = lax.rem(i, 2)

    @pl.when((i == 0) & (j == 0))
    def _():
        _rmsnorm_rows_to(h_ref, nw_in_ref, xn_ref.at[0], tm)

    def step(first):
        u = _dot(xn_ref[slot], wup_ref[...])
        u = jnp.square(jnp.maximum(u, 0.0)).astype(BF16)
        down = _dot(u, wdn_ref[...])
        if first:
            acc_ref[...] = down
        else:
            acc_ref[...] += down

        nw_in = nw_in_ref[...]
        share = tm // n_j
        xn_next_ref = xn_ref.at[1 - slot]
        for g in range(share // 16):
            sl = pl.ds(pl.multiple_of(j * share + g * 16, 16), 16)
            x = h_next_ref[sl, :]
            ms = jnp.mean(x * x, axis=-1, keepdims=True)
            xn_next_ref[sl, :] = (x * lax.rsqrt(ms + EPS) * nw_in).astype(BF16)

    pl.when(j == 0)(functools.partial(step, True))
    pl.when(j > 0)(functools.partial(step, False))

    @pl.when(j == n_j - 1)
    def _():
        _rms_residual_rows(acc_ref, h_ref, nw_out_ref, o_ref, tm)


def _ffn(h, nw_in, wup, wdn, nw_out, layer, *, tm=512, th=1024):
    t, d = h.shape
    hidden = wup.shape[2]
    n_i, n_j = t // tm, hidden // th
    return pl.pallas_call(
        functools.partial(_ffn_kernel, tm=tm, n_j=n_j),
        grid=(n_i, n_j),
        in_specs=[
            pl.BlockSpec((tm, d), lambda i, j: (i, 0)),
            pl.BlockSpec((tm, d), lambda i, j: (jnp.minimum(i + 1, n_i - 1), 0)),
            pl.BlockSpec((1, d), lambda i, j: (0, 0)),
            pl.BlockSpec((None, d, th), lambda i, j: (layer, 0, j)),
            pl.BlockSpec((None, th, d), lambda i, j: (layer, j, 0)),
            pl.BlockSpec((1, d), lambda i, j: (0, 0)),
        ],
        out_specs=pl.BlockSpec((tm, d), lambda i, j: (i, 0)),
        out_shape=jax.ShapeDtypeStruct((t, d), F32),
        scratch_shapes=[pltpu.VMEM((2, tm, d), BF16), pltpu.VMEM((tm, d), F32)],
        compiler_params=_params(2, 56),
        name="ffn",
    )(h, h, nw_in, wup, wdn, nw_out)


def _cast_specs(arrays, n_steps, step_index):
    in_specs, out_specs, out_shapes = [], [], []
    for a in arrays:
        rows, cols = a.shape[0] // n_steps, a.shape[1]
        in_specs.append(pl.BlockSpec((rows, cols), lambda b, i: (step_index(b, i), 0)))
        out_specs.append(pl.BlockSpec((rows, cols), lambda b, i: (step_index(b, i), 0)))
        out_shapes.append(jax.ShapeDtypeStruct(a.shape, BF16))
    return in_specs, out_specs, out_shapes


def _cast_slabs(in_refs, out_refs):
    for in_ref, out_ref in zip(in_refs, out_refs):
        out_ref[...] = in_ref[...].astype(out_ref.dtype)


def _pair_masks():
    row = lax.broadcasted_iota(jnp.int32, (PAIR, PAIR), 0)
    col = lax.broadcasted_iota(jnp.int32, (PAIR, PAIR), 1)
    chunk_bits = LA_CHUNK.bit_length() - 1
    same_chunk = (row >> chunk_bits) == (col >> chunk_bits)
    return row, col, same_chunk


def _chunk_halves(parts, a):
    zeros = jnp.zeros_like(parts)
    return jnp.concatenate([parts, zeros] if a == 0 else [zeros, parts], axis=0)


N_GDN_IN = 8
N_RET_IN = 14


def _mixer_kernel(*refs, tl, n_cast):
    gdn_in, ret_in, rest = refs[:N_GDN_IN], refs[N_GDN_IN:N_GDN_IN + N_RET_IN], refs[N_GDN_IN + N_RET_IN:]
    cast_in, (oa_ref, ob_ref), cast_out = rest[:n_cast], rest[n_cast:n_cast + 2], rest[n_cast + 2:2 * n_cast + 2]
    xpad_ref, s_gdn_ref, s_ret_ref = rest[2 * n_cast + 2:]

    @pl.when(pl.program_id(1) == 0)
    def _():
        _gdn_reset(xpad_ref, s_gdn_ref)
        s_ret_ref[...] = jnp.zeros_like(s_ret_ref)

    _cast_slabs(cast_in, cast_out)
    _gdn_body(*gdn_in, oa_ref, xpad_ref, s_gdn_ref, tl=tl)
    _ret_body(*ret_in, ob_ref, s_ret_ref, tl=tl)


def _gdn_reset(xpad_ref, s_ref):
    xpad_ref[:, 0:HALO, :] = jnp.zeros((3, HALO, QK_WIDTH), F32)
    s_ref[...] = jnp.zeros_like(s_ref)


def _gdn_body(q_ref, k_ref, v_ref, z_ref, gate_ref, cw_ref, gp_ref, onw_ref, o_ref,
              xpad_ref, s_ref, *, tl):
    nh = N_HEADS
    heads = range(nh)

    xpad_ref[0, HALO:HALO + tl, :] = q_ref[...].astype(F32)
    xpad_ref[1, HALO:HALO + tl, :] = k_ref[...].astype(F32)
    xpad_ref[2, HALO:HALO + tl, :] = v_ref[...].astype(F32)

    row, col, same_chunk = _pair_masks()
    causal = same_chunk & (col <= row)
    strict = same_chunk & (col < row)
    eye = (row == col).astype(F32)
    row_in_chunk = row & (LA_CHUNK - 1)
    lane8 = lax.broadcasted_iota(jnp.int32, (nh, PAIR), 1)

    a_coef = -jnp.exp(gp_ref[0:1, :])
    dt_bias = gp_ref[1:2, :]
    onw = onw_ref[...]
    s = [s_ref[h] for h in heads]

    for p in range(tl // PAIR):
        r0 = p * PAIR
        rows = slice(r0, r0 + PAIR)

        pre = gate_ref[rows, :]
        beta_all = jax.nn.sigmoid(pre)
        g = a_coef * _softplus(pre + dt_bias)
        for sh in (1, 2, 4, 8, 16, 32):
            g = g + jnp.where(row_in_chunk >= sh, pltpu.roll(g, sh, axis=0), 0.0)
        gt8 = g.T[nh:2 * nh, :]
        g_last = [jnp.sum(jnp.where(lane8 == (a + 1) * LA_CHUNK - 1, gt8, 0.0), axis=1, keepdims=True)
                  for a in range(2)]
        k_tail8 = jnp.exp(jnp.where(lane8 < LA_CHUNK, g_last[0], g_last[1]) - gt8)
        chunk_decay = [jnp.exp(gl) for gl in g_last]

        def conv(ci, cs):
            w = cw_ref[ci, :, cs]
            acc = xpad_ref[ci, HALO + r0:HALO + r0 + PAIR, cs] * w[CONV_WIDTH - 1:CONV_WIDTH, :]
            for j in range(1, CONV_WIDTH):
                acc = acc + xpad_ref[ci, HALO + r0 - j:HALO + r0 - j + PAIR, cs] * w[CONV_WIDTH - 1 - j:CONV_WIDTH - j, :]
            return _silu(acc)

        q_l, k_l, rhs_l, kbf_l, decay_l, m1_l, lhs_l = [], [], [], [], [], [], []
        for h in heads:
            cs = slice(h * HEAD_DIM, (h + 1) * HEAD_DIM)
            q = conv(0, cs)
            k = conv(1, cs)
            v = conv(2, cs)
            q = q * lax.rsqrt(jnp.sum(q * q, axis=-1, keepdims=True) + 1e-6) * (HEAD_DIM ** -0.5)
            k = k * lax.rsqrt(jnp.sum(k * k, axis=-1, keepdims=True) + 1e-6)
            gc_col = jnp.sum(jnp.where(col == nh + h, g, 0.0), axis=1, keepdims=True)
            beta_col = jnp.sum(jnp.where(col == h, beta_all, 0.0), axis=1, keepdims=True)
            decay = jnp.exp(jnp.where(causal, gc_col - gt8[h:h + 1, :], -jnp.inf))
            e_col = jnp.exp(gc_col)
            q_l.append(q * e_col)
            k_l.append(k)
            rhs_l.append(jnp.concatenate([v * beta_col, k * (beta_col * e_col)], axis=1).astype(BF16))
            kbf = k.astype(BF16)
            kbf_l.append(kbf)
            decay_l.append(decay)
            m1_l.append(beta_col * decay)
            lhs_l.append(jnp.concatenate([q.astype(BF16), kbf], axis=0))

        r_l = [_dot_nt(lhs_l[h], kbf_l[h]) for h in heads]
        qk_l = [(r_l[h][0:PAIR] * decay_l[h]).astype(BF16) for h in heads]
        a_l = [jnp.where(strict, r_l[h][PAIR:2 * PAIR] * m1_l[h], 0.0) for h in heads]
        x_l = [eye - a_l[h] for h in heads]
        pb_l = [a_l[h].astype(BF16) for h in heads]
        for _ in range(5):
            pb_l = [_dot(pb_l[h], pb_l[h]).astype(BF16) for h in heads]
            x_l = [x_l[h] + _dot(x_l[h].astype(BF16), pb_l[h]) for h in heads]

        uw_l = [_dot(x_l[h].astype(BF16), rhs_l[h]) for h in heads]
        u_l = [uw_l[h][:, 0:HEAD_DIM] for h in heads]
        w_l = [uw_l[h][:, HEAD_DIM:2 * HEAD_DIM] for h in heads]
        kt_l = [(k_l[h].T * k_tail8[h:h + 1, :]).astype(BF16) for h in heads]

        v_new_l = [[], []]
        qs_l = [[], []]
        for a in range(2):
            ra = slice(a * LA_CHUNK, (a + 1) * LA_CHUNK)
            r2_l = [_dot(jnp.concatenate([w_l[h][ra], q_l[h][ra]], axis=0).astype(BF16), s[h].astype(BF16))
                    for h in heads]
            for h in heads:
                v_new = u_l[h][ra] - r2_l[h][0:LA_CHUNK]
                v_new_l[a].append(v_new)
                qs_l[a].append(r2_l[h][LA_CHUNK:2 * LA_CHUNK])
            upd_l = [_dot(kt_l[h], _chunk_halves(v_new_l[a][h], a).astype(BF16)) for h in heads]
            s = [s[h] * chunk_decay[a][h:h + 1, :] + upd_l[h] for h in heads]

        o_l = [jnp.concatenate([qs_l[0][h], qs_l[1][h]], axis=0)
               + _dot(qk_l[h], jnp.concatenate([v_new_l[0][h], v_new_l[1][h]], axis=0).astype(BF16))
               for h in heads]
        for h in heads:
            cs = slice(h * HEAD_DIM, (h + 1) * HEAD_DIM)
            o = o_l[h]
            o = o * lax.rsqrt(jnp.mean(o * o, axis=-1, keepdims=True) + EPS) * onw * _silu(z_ref[rows, cs].astype(F32))
            o_ref[rows, cs] = o.astype(o_ref.dtype)

    for h in heads:
        s_ref[h] = s[h]
    xpad_ref[:, 0:HALO, :] = xpad_ref[:, tl:tl + HALO, :]


def _ret_body(q_ref, k_ref, v_ref, g_ref, cos_a_ref, sin_a_ref, cos_b_ref, sin_b_ref, cos_bs_ref, sin_bs_ref,
              dmat_ref, qs_ref, ks_ref, cd_ref, o_ref, s_ref, *, tl):
    heads = range(N_HEADS)
    blk = pl.ds(pl.program_id(1), 1)
    cos_a = cos_a_ref[blk, :]
    sin_a = sin_a_ref[blk, :]

    s = [s_ref[h] for h in heads]

    for p in range(tl // PAIR):
        rows = slice(p * PAIR, (p + 1) * PAIR)
        cosf = cos_a * cos_b_ref[rows, :] - sin_a * sin_b_ref[rows, :]
        sinf = sin_a * cos_bs_ref[rows, :] + cos_a * sin_bs_ref[rows, :]

        q_l, kb_l, kdt_l, v_l = [], [], [], []
        for h in heads:
            cs = slice(h * HEAD_DIM, (h + 1) * HEAD_DIM)
            q = q_ref[rows, cs].astype(F32)
            k = k_ref[rows, cs].astype(F32)
            q = q * cosf + pltpu.roll(q, HEAD_DIM // 2, axis=1) * sinf
            k = (k * cosf + pltpu.roll(k, HEAD_DIM // 2, axis=1) * sinf) * (HEAD_DIM ** -0.5)
            q_l.append(q)
            kb_l.append(k.astype(BF16))
            kdt_l.append((k * ks_ref[h]).T.astype(BF16))
            v_l.append(v_ref[rows, cs])

        qk_l = [(_dot_nt(q_l[h].astype(BF16), kb_l[h]) * dmat_ref[h]).astype(BF16) for h in heads]
        inner_l = [_dot(qk_l[h], v_l[h].astype(BF16)) for h in heads]
        qd_l = [(q_l[h] * qs_ref[h]).astype(BF16) for h in heads]

        cross_l = [[], []]
        for a in range(2):
            ra = slice(a * LA_CHUNK, (a + 1) * LA_CHUNK)
            cross_l[a] = [_dot(qd_l[h][ra], s[h].astype(BF16)) for h in heads]
            upd_l = [_dot(kdt_l[h], _chunk_halves(v_l[h][ra], a).astype(BF16)) for h in heads]
            s = [s[h] * cd_ref[h] + upd_l[h] for h in heads]

        for h in heads:
            cs = slice(h * HEAD_DIM, (h + 1) * HEAD_DIM)
            o = inner_l[h] + jnp.concatenate([cross_l[0][h], cross_l[1][h]], axis=0)
            o = o * lax.rsqrt(jnp.mean(o * o, axis=-1, keepdims=True) + EPS)
            o_ref[rows, cs] = (_silu(g_ref[rows, cs].astype(F32)) * o).astype(o_ref.dtype)

    for h in heads:
        s_ref[h] = s[h]


def _mixer(proj, gates, cw, gp, onw, cast_arrays, *, batch, seq, tl=256):
    nblk = seq // tl
    tables = _retention_tables(seq, tl)
    cast_in_specs, cast_out_specs, cast_shapes = _cast_specs(cast_arrays, batch * nblk, lambda b, i: b * nblk + i)

    def seg(s_idx):
        return pl.BlockSpec((tl, QK_WIDTH), lambda b, i: (b * nblk + i, s_idx))

    def const(shape):
        return pl.BlockSpec(shape, lambda b, i: (0,) * len(shape))

    head_table = (N_HEADS, PAIR, HEAD_DIM)
    out_spec = pl.BlockSpec((tl, QK_WIDTH), lambda b, i: (b * nblk + i, 0))
    out_shape = jax.ShapeDtypeStruct((batch * seq, QK_WIDTH), BF16)
    return pl.pallas_call(
        functools.partial(_mixer_kernel, tl=tl, n_cast=len(cast_arrays)),
        grid=(batch, nblk),
        in_specs=[
            seg(0), seg(1), seg(2), seg(3),
            pl.BlockSpec((tl, LANES), lambda b, i: (b * nblk + i, 0)),
            const((3, CONV_WIDTH, QK_WIDTH)), const((8, LANES)), const((1, HEAD_DIM)),
            seg(4), seg(5), seg(6), seg(7),
            const((nblk, HEAD_DIM)), const((nblk, HEAD_DIM)),
            const((tl, HEAD_DIM)), const((tl, HEAD_DIM)), const((tl, HEAD_DIM)), const((tl, HEAD_DIM)),
            const(head_table), const(head_table), const(head_table), const(head_table),
        ] + cast_in_specs,
        out_specs=[out_spec, out_spec] + cast_out_specs,
        out_shape=[out_shape, out_shape] + cast_shapes,
        scratch_shapes=[pltpu.VMEM((3, HALO + tl, QK_WIDTH), F32),
                        pltpu.VMEM((N_HEADS, HEAD_DIM, HEAD_DIM), F32),
                        pltpu.VMEM((N_HEADS, HEAD_DIM, HEAD_DIM), F32)],
        compiler_params=_params(2, 56),
        name="mixer",
    )(proj, proj, proj, proj, gates, cw, gp, onw, proj, proj, proj, proj, *tables, *cast_arrays)


def _sg_gate_kernel(u_ref, v_ref, lnw_ref, lnb_ref, ws_ref, bs_ref, o_ref, vn_ref, *, tm):
    lnw = lnw_ref[...]
    lnb = lnb_ref[...]
    step = 16

    def body(r, carry):
        sl = pl.ds(pl.multiple_of(r * step, step), step)
        x = v_ref[sl, :].astype(F32)
        mu = jnp.mean(x, axis=-1, keepdims=True)
        xc = x - mu
        var = jnp.mean(xc * xc, axis=-1, keepdims=True)
        vn_ref[sl, :] = (xc * lax.rsqrt(var + EPS) * lnw + lnb).astype(vn_ref.dtype)
        return carry

    lax.fori_loop(0, tm // step, body, 0, unroll=ROW_LOOP_UNROLL)

    row = lax.broadcasted_iota(jnp.int32, (SG_CHUNK, SG_CHUNK), 0)
    col = lax.broadcasted_iota(jnp.int32, (SG_CHUNK, SG_CHUNK), 1)
    for g in range(SG_GROUPS):
        wg = jnp.where(col <= row, ws_ref[g], 0.0).astype(BF16)
        bias = bs_ref[:, g:g + 1]
        cols = slice(g * SG_GROUP_DIM, (g + 1) * SG_GROUP_DIM)
        for c in range(tm // SG_CHUNK):
            rows = slice(c * SG_CHUNK, (c + 1) * SG_CHUNK)
            sgate = _dot(wg, vn_ref[rows, cols]) + bias
            o_ref[rows, cols] = (u_ref[rows, cols].astype(F32) * sgate).astype(o_ref.dtype)


def _sg_gate(proj, lnw, lnb, ws, bs_t, *, tm=512):
    t = proj.shape[0]
    return pl.pallas_call(
        functools.partial(_sg_gate_kernel, tm=tm),
        grid=(t // tm,),
        in_specs=[
            pl.BlockSpec((tm, SG_WIDTH), lambda i: (i, 0)),
            pl.BlockSpec((tm, SG_WIDTH), lambda i: (i, 1)),
            pl.BlockSpec((1, SG_WIDTH), lambda i: (0, 0)),
            pl.BlockSpec((1, SG_WIDTH), lambda i: (0, 0)),
            pl.BlockSpec((SG_GROUPS, SG_CHUNK, SG_CHUNK), lambda i: (0, 0, 0)),
            pl.BlockSpec((SG_CHUNK, SG_GROUPS), lambda i: (0, 0)),
        ],
        out_specs=pl.BlockSpec((tm, SG_WIDTH), lambda i: (i, 0)),
        out_shape=jax.ShapeDtypeStruct((t, SG_WIDTH), BF16),
        scratch_shapes=[pltpu.VMEM((tm, SG_WIDTH), BF16)],
        compiler_params=_params(1, 40),
        name="sg_gate",
    )(proj, proj, lnw, lnb, ws, bs_t)


def _retention_tables(seq, tl):
    half = HEAD_DIM // 2
    inv_freq = 1.0 / (ROPE_BASE ** jnp.linspace(0.0, 1.0, half, dtype=F32))
    base = (jnp.arange(seq // tl, dtype=F32) * tl)[:, None] * inv_freq[None, :]
    off = jnp.arange(tl, dtype=F32)[:, None] * inv_freq[None, :]

    def both_halves(a):
        return jnp.concatenate([a, a], axis=1)

    sign = jnp.concatenate([-jnp.ones((half,), F32), jnp.ones((half,), F32)])[None, :]
    cos_a, sin_a = both_halves(jnp.cos(base)), both_halves(jnp.sin(base))
    cos_b, sin_b = both_halves(jnp.cos(off)), both_halves(jnp.sin(off))

    log_gamma = jnp.log1p(-jnp.power(2.0, -5.0 - jnp.arange(N_HEADS, dtype=F32)))
    t = jnp.arange(PAIR)
    pos = (t % LA_CHUNK).astype(F32)
    mask = ((t[:, None] // LA_CHUNK) == (t[None, :] // LA_CHUNK)) & (t[None, :] <= t[:, None])
    dmat = jnp.exp(jnp.where(mask[None], (pos[:, None] - pos[None, :])[None] * log_gamma[:, None, None], -jnp.inf))
    full = (N_HEADS, PAIR, HEAD_DIM)
    q_scale = jnp.broadcast_to(jnp.exp((pos[None, :] + 1.0) * log_gamma[:, None])[:, :, None], full)
    k_scale = jnp.broadcast_to(jnp.exp((LA_CHUNK - 1.0 - pos[None, :]) * log_gamma[:, None])[:, :, None], full)
    chunk_decay = jnp.broadcast_to(jnp.exp(LA_CHUNK * log_gamma)[:, None, None], full)
    return cos_a, sin_a, cos_b, sin_b, cos_b * sign, sin_b * sign, dmat, q_scale, k_scale, chunk_decay


def kernel(x, norm_w, la_w_in, la_conv_w, la_a_log, la_dt_bias, la_out_norm_w, la_w_out, sg_w_in, sg_ln_w,
           sg_ln_b, sg_w_s, sg_b_s, sg_w_out, ffn_w_up, ffn_w_down):
    batch, seq, d = x.shape
    t = batch * seq
    h = x.reshape(t, d)
    gate0 = 4 * QK_WIDTH

    w_t = jnp.swapaxes(la_w_in[0], 0, 1).astype(BF16)
    proj, gates = _inproj_la(h, norm_w[0, 0][None, :], w_t, gate0=gate0, n_gate=2 * N_HEADS)

    cw = la_conv_w[0].reshape(3, QK_WIDTH, CONV_WIDTH).transpose(0, 2, 1)
    gp = jnp.zeros((8, LANES), F32)
    gp = gp.at[0, N_HEADS:2 * N_HEADS].set(la_a_log[0]).at[1, N_HEADS:2 * N_HEADS].set(la_dt_bias[0])
    n_layers, _, hidden = ffn_w_up.shape
    o_a, o_b, w_up, w_down, w_out, w_sg_in, w_sg_out = _mixer(
        proj, gates, cw, gp, la_out_norm_w[0][None, :],
        [ffn_w_up.reshape(n_layers * d, hidden), ffn_w_down.reshape(n_layers * hidden, d),
         la_w_out[0], sg_w_in[0], sg_w_out[0]], batch=batch, seq=seq)
    w_up = w_up.reshape(n_layers, d, hidden)
    w_down = w_down.reshape(n_layers, hidden, d)

    h = _out_rms_res([o_a, o_b], [w_out, w_out], h, norm_w[0, 1][None, :], tm=512, vmem_mib=48)
    h = _ffn(h, norm_w[0, 2][None, :], w_up, w_down, norm_w[0, 3][None, :], 0)

    proj1 = _inproj_gelu(h, norm_w[1, 0][None, :], w_sg_in)
    gated = _sg_gate(proj1, sg_ln_w[0][None, :], sg_ln_b[0][None, :], sg_w_s[0], sg_b_s[0].T)
    h = _out_rms_res([gated], [w_sg_out], h, norm_w[1, 1][None, :], tm=512, vmem_mib=56)
    h = _ffn(h, norm_w[1, 2][None, :], w_up, w_down, norm_w[1, 3][None, :], 1)
    return h.reshape(batch, seq, d)
```

```python
import functools
import math

import jax
import jax.numpy as jnp
from jax import lax
from jax.experimental import pallas as pl
from jax.experimental.pallas import tpu as pltpu

F32 = jnp.float32
BF16 = jnp.bfloat16

D_MODEL = 2048
N_HEADS = 8
HEAD_DIM = 128
QK_WIDTH = N_HEADS * HEAD_DIM
LA_CHUNK = 64
PAIR = 2 * LA_CHUNK
CONV_WIDTH = 4
HALO = 8
ROPE_BASE = 10000.0
SG_CHUNK = 128
SG_GROUPS = 8
SG_WIDTH = 2 * D_MODEL
SG_GROUP_DIM = SG_WIDTH // SG_GROUPS
FFN_HIDDEN = 4 * D_MODEL
EPS = 1e-6
LANES = 128
ROW_LOOP_UNROLL = 8
DEEP_UNROLL = 16
MIB = 1024 * 1024

NT_DIMS = (((1,), (1,)), ((), ()))


def _params(n_grid_axes, vmem_mib):
    return pltpu.CompilerParams(dimension_semantics=("arbitrary",) * n_grid_axes,
                                vmem_limit_bytes=vmem_mib * MIB)


def _dot(a, b):
    return jnp.dot(a, b, preferred_element_type=F32)


def _dot_nt(a, b):
    return lax.dot_general(a, b, NT_DIMS, preferred_element_type=F32)


def _silu(x):
    return x * jax.nn.sigmoid(x)


def _softplus(x):
    return jnp.maximum(x, 0.0) + jnp.log1p(jnp.exp(-jnp.abs(x)))


def _gelu_tanh(x):
    k0 = -2.0 * math.sqrt(2.0 / math.pi) * math.log2(math.e)
    k1 = k0 * 0.044715
    return x / (1.0 + jnp.exp2(x * (k0 + k1 * (x * x))))


def _rmsnorm_rows_to(src_ref, nw_ref, dst_ref, rows, unroll=ROW_LOOP_UNROLL):
    nw = nw_ref[...]
    step = 16

    def body(r, carry):
        sl = pl.ds(pl.multiple_of(r * step, step), step)
        x = src_ref[sl, :]
        ms = jnp.mean(x * x, axis=-1, keepdims=True)
        dst_ref[sl, :] = (x * lax.rsqrt(ms + EPS) * nw).astype(dst_ref.dtype)
        return carry

    lax.fori_loop(0, rows // step, body, 0, unroll=unroll)


def _rms_residual_rows(acc_ref, h_ref, nw_ref, o_ref, rows, unroll=ROW_LOOP_UNROLL):
    nw = nw_ref[...]
    step = 16

    def body(r, carry):
        sl = pl.ds(pl.multiple_of(r * step, step), step)
        y = acc_ref[sl, :]
        ms = jnp.mean(y * y, axis=-1, keepdims=True)
        o_ref[sl, :] = h_ref[sl, :] + y * lax.rsqrt(ms + EPS) * nw
        return carry

    lax.fori_loop(0, rows // step, body, 0, unroll=unroll)


def _column_chunks(width, chunk=1024):
    return [slice(c, c + chunk) for c in range(0, width, chunk)]


def _inproj_la_kernel(x_ref, nw_ref, w_ref, wg_ref, o_ref, g_ref, xn_ref, *, tm):
    @pl.when(pl.program_id(1) == 0)
    def _():
        _rmsnorm_rows_to(x_ref, nw_ref, xn_ref, tm, DEEP_UNROLL)
        g_ref[...] = _dot_nt(xn_ref[...], wg_ref[...])

    for cols in _column_chunks(o_ref.shape[1]):
        o_ref[:, cols] = _dot_nt(xn_ref[...], w_ref[cols, :]).astype(o_ref.dtype)


def _inproj_la(x, nw, w_t, *, gate0, n_gate, tm=1024, tn=2048):
    t, k = x.shape
    n = w_t.shape[0] - n_gate
    per_part = gate0 // tn

    def w_rows(i, j):
        return (pl.multiple_of(j * tn + jnp.where(j >= per_part, n_gate, 0), n_gate), 0)

    return pl.pallas_call(
        functools.partial(_inproj_la_kernel, tm=tm),
        grid=(t // tm, n // tn),
        in_specs=[
            pl.BlockSpec((tm, k), lambda i, j: (i, 0)),
            pl.BlockSpec((1, k), lambda i, j: (0, 0)),
            pl.BlockSpec((pl.Element(tn), pl.Element(k)), w_rows),
            pl.BlockSpec((LANES, k), lambda i, j: (gate0 // LANES, 0)),
        ],
        out_specs=[
            pl.BlockSpec((tm, tn), lambda i, j: (i, j)),
            pl.BlockSpec((tm, LANES), lambda i, j: (i, 0)),
        ],
        out_shape=[jax.ShapeDtypeStruct((t, n), BF16), jax.ShapeDtypeStruct((t, LANES), F32)],
        scratch_shapes=[pltpu.VMEM((tm, k), BF16)],
        compiler_params=_params(2, 56),
        name="inproj_la",
    )(x, nw, w_t, w_t)


def _inproj_gelu_kernel(x_ref, nw_ref, w_ref, o_ref, xn_ref, *, tm):
    @pl.when(pl.program_id(1) == 0)
    def _():
        _rmsnorm_rows_to(x_ref, nw_ref, xn_ref, tm, DEEP_UNROLL)

    for cols in _column_chunks(o_ref.shape[1]):
        o_ref[:, cols] = _gelu_tanh(_dot(xn_ref[...], w_ref[:, cols])).astype(o_ref.dtype)


def _inproj_gelu(x, nw, w, *, tm=1024, tn=2048):
    t, k = x.shape
    n = w.shape[1]
    return pl.pallas_call(
        functools.partial(_inproj_gelu_kernel, tm=tm),
        grid=(t // tm, n // tn),
        in_specs=[
            pl.BlockSpec((tm, k), lambda i, j: (i, 0)),
            pl.BlockSpec((1, k), lambda i, j: (0, 0)),
            pl.BlockSpec((k, tn), lambda i, j: (0, j)),
        ],
        out_specs=pl.BlockSpec((tm, tn), lambda i, j: (i, j)),
        out_shape=jax.ShapeDtypeStruct((t, n), BF16),
        scratch_shapes=[pltpu.VMEM((tm, k), BF16)],
        compiler_params=_params(2, 56),
        name="inproj_gelu",
    )(x, nw, w)


def _out_rms_res_kernel(*refs, n_pairs, tm):
    a_refs = refs[:n_pairs]
    w_refs = refs[n_pairs:2 * n_pairs]
    h_ref, nw_ref, o_ref, acc_ref = refs[2 * n_pairs:]
    y = _dot(a_refs[0][...], w_refs[0][...])
    for a_ref, w_ref in zip(a_refs[1:], w_refs[1:]):
        y = y + _dot(a_ref[...], w_ref[...])
    acc_ref[...] = y
    _rms_residual_rows(acc_ref, h_ref, nw_ref, o_ref, tm, DEEP_UNROLL)


def _out_rms_res(a_list, w_list, h, nw, *, tm, vmem_mib):
    t, n = h.shape
    n_pairs = len(a_list)
    in_specs = [pl.BlockSpec((tm, a.shape[1]), lambda i: (i, 0)) for a in a_list]
    k0 = 0
    for a, w in zip(a_list, w_list):
        ka = a.shape[1]
        row_block = k0 // ka if w.shape[0] != ka else 0
        in_specs.append(pl.BlockSpec((ka, n), lambda i, rb=row_block: (rb, 0)))
        k0 += ka
    in_specs += [pl.BlockSpec((tm, n), lambda i: (i, 0)), pl.BlockSpec((1, n), lambda i: (0, 0))]
    return pl.pallas_call(
        functools.partial(_out_rms_res_kernel, n_pairs=n_pairs, tm=tm),
        grid=(t // tm,),
        in_specs=in_specs,
        out_specs=pl.BlockSpec((tm, n), lambda i: (i, 0)),
        out_shape=jax.ShapeDtypeStruct((t, n), F32),
        scratch_shapes=[pltpu.VMEM((tm, n), F32)],
        compiler_params=_params(1, vmem_mib),
        name="out_rms_res",
    )(*a_list, *w_list, h, nw)


def _ffn_kernel(h_ref, h_next_ref, nw_in_ref, wup_ref, wdn_ref, nw_out_ref, o_ref, xn_ref, acc_ref, *, tm, n_j):
    i = pl.program_id(0)
    j = pl.program_id(1)
    slot = lax.rem(i, 2)

    @pl.when((i == 0) & (j == 0))
    def _():
        _rmsnorm_rows_to(h_ref, nw_in_ref, xn_ref.at[0], tm)

    def step(first):
        u = _dot(xn_ref[slot], wup_ref[...])
        u = jnp.square(jnp.maximum(u, 0.0)).astype(BF16)
        down = _dot(u, wdn_ref[...])
        if first:
            acc_ref[...] = down
        else:
            acc_ref[...] += down

        nw_in = nw_in_ref[...]
        share = tm // n_j
        xn_next_ref = xn_ref.at[1 - slot]
        for g in range(share // 16):
            sl = pl.ds(pl.multiple_of(j * share + g * 16, 16), 16)
            x = h_next_ref[sl, :]
            ms = jnp.mean(x * x, axis=-1, keepdims=True)
            xn_next_ref[sl, :] = (x * lax.rsqrt(ms + EPS) * nw_in).astype(BF16)

    pl.when(j == 0)(functools.partial(step, True))
    pl.when(j > 0)(functools.partial(step, False))

    @pl.when(j == n_j - 1)
    def _():
        _rms_residual_rows(acc_ref, h_ref, nw_out_ref, o_ref, tm)


def _ffn(h, nw_in, wup, wdn, nw_out, layer, *, tm=512, th=1024):
    t, d = h.shape
    hidden = wup.shape[2]
    n_i, n_j = t // tm, hidden // th
    return pl.pallas_call(
        functools.partial(_ffn_kernel, tm=tm, n_j=n_j),
        grid=(n_i, n_j),
        in_specs=[
            pl.BlockSpec((tm, d), lambda i, j: (i, 0)),
            pl.BlockSpec((tm, d), lambda i, j: (jnp.minimum(i + 1, n_i - 1), 0)),
            pl.BlockSpec((1, d), lambda i, j: (0, 0)),
            pl.BlockSpec((None, d, th), lambda i, j: (layer, 0, j)),
            pl.BlockSpec((None, th, d), lambda i, j: (layer, j, 0)),
            pl.BlockSpec((1, d), lambda i, j: (0, 0)),
        ],
        out_specs=pl.BlockSpec((tm, d), lambda i, j: (i, 0)),
        out_shape=jax.ShapeDtypeStruct((t, d), F32),
        scratch_shapes=[pltpu.VMEM((2, tm, d), BF16), pltpu.VMEM((tm, d), F32)],
        compiler_params=_params(2, 56),
        name="ffn",
    )(h, h, nw_in, wup, wdn, nw_out)


def _cast_specs(arrays, n_steps, step_index):
    in_specs, out_specs, out_shapes = [], [], []
    for a in arrays:
        rows, cols = a.shape[0] // n_steps, a.shape[1]
        in_specs.append(pl.BlockSpec((rows, cols), lambda b, i: (step_index(b, i), 0)))
        out_specs.append(pl.BlockSpec((rows, cols), lambda b, i: (step_index(b, i), 0)))
        out_shapes.append(jax.ShapeDtypeStruct(a.shape, BF16))
    return in_specs, out_specs, out_shapes


def _cast_slabs(in_refs, out_refs):
    for in_ref, out_ref in zip(in_refs, out_refs):
        out_ref[...] = in_ref[...].astype(out_ref.dtype)


def _pair_masks():
    row = lax.broadcasted_iota(jnp.int32, (PAIR, PAIR), 0)
    col = lax.broadcasted_iota(jnp.int32, (PAIR, PAIR), 1)
    chunk_bits = LA_CHUNK.bit_length() - 1
    same_chunk = (row >> chunk_bits) == (col >> chunk_bits)
    return row, col, same_chunk


def _chunk_halves(parts, a):
    zeros = jnp.zeros_like(parts)
    return jnp.concatenate([parts, zeros] if a == 0 else [zeros, parts], axis=0)


N_GDN_IN = 8
N_RET_IN = 14


def _mixer_kernel(*refs, tl, n_cast):
    gdn_in, ret_in, rest = refs[:N_GDN_IN], refs[N_GDN_IN:N_GDN_IN + N_RET_IN], refs[N_GDN_IN + N_RET_IN:]
    cast_in, (oa_ref, ob_ref), cast_out = rest[:n_cast], rest[n_cast:n_cast + 2], rest[n_cast + 2:2 * n_cast + 2]
    xpad_ref, s_gdn_ref, s_ret_ref = rest[2 * n_cast + 2:]

    @pl.when(pl.program_id(1) == 0)
    def _():
        _gdn_reset(xpad_ref, s_gdn_ref)
        s_ret_ref[...] = jnp.zeros_like(s_ret_ref)

    _cast_slabs(cast_in, cast_out)
    _gdn_body(*gdn_in, oa_ref, xpad_ref, s_gdn_ref, tl=tl)
    _ret_body(*ret_in, ob_ref, s_ret_ref, tl=tl)


def _gdn_reset(xpad_ref, s_ref):
    xpad_ref[:, 0:HALO, :] = jnp.zeros((3, HALO, QK_WIDTH), F32)
    s_ref[...] = jnp.zeros_like(s_ref)


def _gdn_body(q_ref, k_ref, v_ref, z_ref, gate_ref, cw_ref, gp_ref, onw_ref, o_ref,
              xpad_ref, s_ref, *, tl):
    nh = N_HEADS
    heads = range(nh)

    xpad_ref[0, HALO:HALO + tl, :] = q_ref[...].astype(F32)
    xpad_ref[1, HALO:HALO + tl, :] = k_ref[...].astype(F32)
    xpad_ref[2, HALO:HALO + tl, :] = v_ref[...].astype(F32)

    row, col, same_chunk = _pair_masks()
    causal = same_chunk & (col <= row)
    strict = same_chunk & (col < row)
    eye = (row == col).astype(F32)
    row_in_chunk = row & (LA_CHUNK - 1)
    lane8 = lax.broadcasted_iota(jnp.int32, (nh, PAIR), 1)

    a_coef = -jnp.exp(gp_ref[0:1, :])
    dt_bias = gp_ref[1:2, :]
    onw = onw_ref[...]
    s = [s_ref[h] for h in heads]

    for p in range(tl // PAIR):
        r0 = p * PAIR
        rows = slice(r0, r0 + PAIR)

        pre = gate_ref[rows, :]
        beta_all = jax.nn.sigmoid(pre)
        g = a_coef * _softplus(pre + dt_bias)
        for sh in (1, 2, 4, 8, 16, 32):
            g = g + jnp.where(row_in_chunk >= sh, pltpu.roll(g, sh, axis=0), 0.0)
        gt8 = g.T[nh:2 * nh, :]
        g_last = [jnp.sum(jnp.where(lane8 == (a + 1) * LA_CHUNK - 1, gt8, 0.0), axis=1, keepdims=True)
                  for a in range(2)]
        k_tail8 = jnp.exp(jnp.where(lane8 < LA_CHUNK, g_last[0], g_last[1]) - gt8)
        chunk_decay = [jnp.exp(gl) for gl in g_last]

        def conv(ci, cs):
            w = cw_ref[ci, :, cs]
            acc = xpad_ref[ci, HALO + r0:HALO + r0 + PAIR, cs] * w[CONV_WIDTH - 1:CONV_WIDTH, :]
            for j in range(1, CONV_WIDTH):
                acc = acc + xpad_ref[ci, HALO + r0 - j:HALO + r0 - j + PAIR, cs] * w[CONV_WIDTH - 1 - j:CONV_WIDTH - j, :]
            return _silu(acc)

        q_l, k_l, rhs_l, kbf_l, decay_l, m1_l, lhs_l = [], [], [], [], [], [], []
        for h in heads:
            cs = slice(h * HEAD_DIM, (h + 1) * HEAD_DIM)
            q = conv(0, cs)
            k = conv(1, cs)
            v = conv(2, cs)
            q = q * lax.rsqrt(jnp.sum(q * q, axis=-1, keepdims=True) + 1e-6) * (HEAD_DIM ** -0.5)
            k = k * lax.rsqrt(jnp.sum(k * k, axis=-1, keepdims=True) + 1e-6)
            gc_col = jnp.sum(jnp.where(col == nh + h, g, 0.0), axis=1, keepdims=True)
            beta_col = jnp.sum(jnp.where(col == h, beta_all, 0.0), axis=1, keepdims=True)
            decay = jnp.exp(jnp.where(causal, gc_col - gt8[h:h + 1, :], -jnp.inf))
            e_col = jnp.exp(gc_col)
            q_l.append(q * e_col)
            k_l.append(k)
            rhs_l.append(jnp.concatenate([v * beta_col, k * (beta_col * e_col)], axis=1).astype(BF16))
            kbf = k.astype(BF16)
            kbf_l.append(kbf)
            decay_l.append(decay)
            m1_l.append(beta_col * decay)
            lhs_l.append(jnp.concatenate([q.astype(BF16), kbf], axis=0))

        r_l = [_dot_nt(lhs_l[h], kbf_l[h]) for h in heads]
        qk_l = [(r_l[h][0:PAIR] * decay_l[h]).astype(BF16) for h in heads]
        a_l = [jnp.where(strict, r_l[h][PAIR:2 * PAIR] * m1_l[h], 0.0) for h in heads]
        x_l = [eye - a_l[h] for h in heads]
        pb_l = [a_l[h].astype(BF16) for h in heads]
        for _ in range(5):
            pb_l = [_dot(pb_l[h], pb_l[h]).astype(BF16) for h in heads]
            x_l = [x_l[h] + _dot(x_l[h].astype(BF16), pb_l[h]) for h in heads]

        uw_l = [_dot(x_l[h].astype(BF16), rhs_l[h]) for h in heads]
        u_l = [uw_l[h][:, 0:HEAD_DIM] for h in heads]
        w_l = [uw_l[h][:, HEAD_DIM:2 * HEAD_DIM] for h in heads]
        kt_l = [(k_l[h].T * k_tail8[h:h + 1, :]).astype(BF16) for h in heads]

        v_new_l = [[], []]
        qs_l = [[], []]
        for a in range(2):
            ra = slice(a * LA_CHUNK, (a + 1) * LA_CHUNK)
            r2_l = [_dot(jnp.concatenate([w_l[h][ra], q_l[h][ra]], axis=0).astype(BF16), s[h].astype(BF16))
                    for h in heads]
            for h in heads:
                v_new = u_l[h][ra] - r2_l[h][0:LA_CHUNK]
                v_new_l[a].append(v_new)
                qs_l[a].append(r2_l[h][LA_CHUNK:2 * LA_CHUNK])
            upd_l = [_dot(kt_l[h], _chunk_halves(v_new_l[a][h], a).astype(BF16)) for h in heads]
            s = [s[h] * chunk_decay[a][h:h + 1, :] + upd_l[h] for h in heads]

        o_l = [jnp.concatenate([qs_l[0][h], qs_l[1][h]], axis=0)
               + _dot(qk_l[h], jnp.concatenate([v_new_l[0][h], v_new_l[1][h]], axis=0).astype(BF16))
               for h in heads]
        for h in heads:
            cs = slice(h * HEAD_DIM, (h + 1) * HEAD_DIM)
            o = o_l[h]
            o = o * lax.rsqrt(jnp.mean(o * o, axis=-1, keepdims=True) + EPS) * onw * _silu(z_ref[rows, cs].astype(F32))
            o_ref[rows, cs] = o.astype(o_ref.dtype)

    for h in heads:
        s_ref[h] = s[h]
    xpad_ref[:, 0:HALO, :] = xpad_ref[:, tl:tl + HALO, :]


def _ret_body(q_ref, k_ref, v_ref, g_ref, cos_a_ref, sin_a_ref, cos_b_ref, sin_b_ref, cos_bs_ref, sin_bs_ref,
              dmat_ref, qs_ref, ks_ref, cd_ref, o_ref, s_ref, *, tl):
    heads = range(N_HEADS)
    blk = pl.ds(pl.program_id(1), 1)
    cos_a = cos_a_ref[blk, :]
    sin_a = sin_a_ref[blk, :]

    s = [s_ref[h] for h in heads]

    for p in range(tl // PAIR):
        rows = slice(p * PAIR, (p + 1) * PAIR)
        cosf = cos_a * cos_b_ref[rows, :] - sin_a * sin_b_ref[rows, :]
        sinf = sin_a * cos_bs_ref[rows, :] + cos_a * sin_bs_ref[rows, :]

        q_l, kb_l, kdt_l, v_l = [], [], [], []
        for h in heads:
            cs = slice(h * HEAD_DIM, (h + 1) * HEAD_DIM)
            q = q_ref[rows, cs].astype(F32)
            k = k_ref[rows, cs].astype(F32)
            q = q * cosf + pltpu.roll(q, HEAD_DIM // 2, axis=1) * sinf
            k = (k * cosf + pltpu.roll(k, HEAD_DIM // 2, axis=1) * sinf) * (HEAD_DIM ** -0.5)
            q_l.append(q)
            kb_l.append(k.astype(BF16))
            kdt_l.append((k * ks_ref[h]).T.astype(BF16))
            v_l.append(v_ref[rows, cs])

        qk_l = [(_dot_nt(q_l[h].astype(BF16), kb_l[h]) * dmat_ref[h]).astype(BF16) for h in heads]
        inner_l = [_dot(qk_l[h], v_l[h].astype(BF16)) for h in heads]
        qd_l = [(q_l[h] * qs_ref[h]).astype(BF16) for h in heads]

        cross_l = [[], []]
        for a in range(2):
            ra = slice(a * LA_CHUNK, (a + 1) * LA_CHUNK)
            cross_l[a] = [_dot(qd_l[h][ra], s[h].astype(BF16)) for h in heads]
            upd_l = [_dot(kdt_l[h], _chunk_halves(v_l[h][ra], a).astype(BF16)) for h in heads]
            s = [s[h] * cd_ref[h] + upd_l[h] for h in heads]

        for h in heads:
            cs = slice(h * HEAD_DIM, (h + 1) * HEAD_DIM)
            o = inner_l[h] + jnp.concatenate([cross_l[0][h], cross_l[1][h]], axis=0)
            o = o * lax.rsqrt(jnp.mean(o * o, axis=-1, keepdims=True) + EPS)
            o_ref[rows, cs] = (_silu(g_ref[rows, cs].astype(F32)) * o).astype(o_ref.dtype)

    for h in heads:
        s_ref[h] = s[h]


def _mixer(proj, gates, cw, gp, onw, cast_arrays, *, batch, seq, tl=256):
    nblk = seq // tl
    tables = _retention_tables(seq, tl)
    cast_in_specs, cast_out_specs, cast_shapes = _cast_specs(cast_arrays, batch * nblk, lambda b, i: b * nblk + i)

    def seg(s_idx):
        return pl.BlockSpec((tl, QK_WIDTH), lambda b, i: (b * nblk + i, s_idx))

    def const(shape):
        return pl.BlockSpec(shape, lambda b, i: (0,) * len(shape))

    head_table = (N_HEADS, PAIR, HEAD_DIM)
    out_spec = pl.BlockSpec((tl, QK_WIDTH), lambda b, i: (b * nblk + i, 0))
    out_shape = jax.ShapeDtypeStruct((batch * seq, QK_WIDTH), BF16)
    return pl.pallas_call(
        functools.partial(_mixer_kernel, tl=tl, n_cast=len(cast_arrays)),
        grid=(batch, nblk),
        in_specs=[
            seg(0), seg(1), seg(2), seg(3),
            pl.BlockSpec((tl, LANES), lambda b, i: (b * nblk + i, 0)),
            const((3, CONV_WIDTH, QK_WIDTH)), const((8, LANES)), const((1, HEAD_DIM)),
            seg(4), seg(5), seg(6), seg(7),
            const((nblk, HEAD_DIM)), const((nblk, HEAD_DIM)),
            const((tl, HEAD_DIM)), const((tl, HEAD_DIM)), const((tl, HEAD_DIM)), const((tl, HEAD_DIM)),
            const(head_table), const(head_table), const(head_table), const(head_table),
        ] + cast_in_specs,
        out_specs=[out_spec, out_spec] + cast_out_specs,
        out_shape=[out_shape, out_shape] + cast_shapes,
        scratch_shapes=[pltpu.VMEM((3, HALO + tl, QK_WIDTH), F32),
                        pltpu.VMEM((N_HEADS, HEAD_DIM, HEAD_DIM), F32),
                        pltpu.VMEM((N_HEADS, HEAD_DIM, HEAD_DIM), F32)],
        compiler_params=_params(2, 56),
        name="mixer",
    )(proj, proj, proj, proj, gates, cw, gp, onw, proj, proj, proj, proj, *tables, *cast_arrays)


def _sg_gate_kernel(u_ref, v_ref, lnw_ref, lnb_ref, ws_ref, bs_ref, o_ref, vn_ref, *, tm):
    lnw = lnw_ref[...]
    lnb = lnb_ref[...]
    step = 16

    def body(r, carry):
        sl = pl.ds(pl.multiple_of(r * step, step), step)
        x = v_ref[sl, :].astype(F32)
        mu = jnp.mean(x, axis=-1, keepdims=True)
        xc = x - mu
        var = jnp.mean(xc * xc, axis=-1, keepdims=True)
        vn_ref[sl, :] = (xc * lax.rsqrt(var + EPS) * lnw + lnb).astype(vn_ref.dtype)
        return carry

    lax.fori_loop(0, tm // step, body, 0, unroll=ROW_LOOP_UNROLL)

    row = lax.broadcasted_iota(jnp.int32, (SG_CHUNK, SG_CHUNK), 0)
    col = lax.broadcasted_iota(jnp.int32, (SG_CHUNK, SG_CHUNK), 1)
    for g in range(SG_GROUPS):
        wg = jnp.where(col <= row, ws_ref[g], 0.0).astype(BF16)
        bias = bs_ref[:, g:g + 1]
        cols = slice(g * SG_GROUP_DIM, (g + 1) * SG_GROUP_DIM)
        for c in range(tm // SG_CHUNK):
            rows = slice(c * SG_CHUNK, (c + 1) * SG_CHUNK)
            sgate = _dot(wg, vn_ref[rows, cols]) + bias
            o_ref[rows, cols] = (u_ref[rows, cols].astype(F32) * sgate).astype(o_ref.dtype)


def _sg_gate(proj, lnw, lnb, ws, bs_t, *, tm=512):
    t = proj.shape[0]
    return pl.pallas_call(
        functools.partial(_sg_gate_kernel, tm=tm),
        grid=(t // tm,),
        in_specs=[
            pl.BlockSpec((tm, SG_WIDTH), lambda i: (i, 0)),
            pl.BlockSpec((tm, SG_WIDTH), lambda i: (i, 1)),
            pl.BlockSpec((1, SG_WIDTH), lambda i: (0, 0)),
            pl.BlockSpec((1, SG_WIDTH), lambda i: (0, 0)),
            pl.BlockSpec((SG_GROUPS, SG_CHUNK, SG_CHUNK), lambda i: (0, 0, 0)),
            pl.BlockSpec((SG_CHUNK, SG_GROUPS), lambda i: (0, 0)),
        ],
        out_specs=pl.BlockSpec((tm, SG_WIDTH), lambda i: (i, 0)),
        out_shape=jax.ShapeDtypeStruct((t, SG_WIDTH), BF16),
        scratch_shapes=[pltpu.VMEM((tm, SG_WIDTH), BF16)],
        compiler_params=_params(1, 40),
        name="sg_gate",
    )(proj, proj, lnw, lnb, ws, bs_t)


def _retention_tables(seq, tl):
    half = HEAD_DIM // 2
    inv_freq = 1.0 / (ROPE_BASE ** jnp.linspace(0.0, 1.0, half, dtype=F32))
    base = (jnp.arange(seq // tl, dtype=F32) * tl)[:, None] * inv_freq[None, :]
    off = jnp.arange(tl, dtype=F32)[:, None] * inv_freq[None, :]

    def both_halves(a):
        return jnp.concatenate([a, a], axis=1)

    sign = jnp.concatenate([-jnp.ones((half,), F32), jnp.ones((half,), F32)])[None, :]
    cos_a, sin_a = both_halves(jnp.cos(base)), both_halves(jnp.sin(base))
    cos_b, sin_b = both_halves(jnp.cos(off)), both_halves(jnp.sin(off))

    log_gamma = jnp.log1p(-jnp.power(2.0, -5.0 - jnp.arange(N_HEADS, dtype=F32)))
    t = jnp.arange(PAIR)
    pos = (t % LA_CHUNK).astype(F32)
    mask = ((t[:, None] // LA_CHUNK) == (t[None, :] // LA_CHUNK)) & (t[None, :] <= t[:, None])
    dmat = jnp.exp(jnp.where(mask[None], (pos[:, None] - pos[None, :])[None] * log_gamma[:, None, None], -jnp.inf))
    full = (N_HEADS, PAIR, HEAD_DIM)
    q_scale = jnp.broadcast_to(jnp.exp((pos[None, :] + 1.0) * log_gamma[:, None])[:, :, None], full)
    k_scale = jnp.broadcast_to(jnp.exp((LA_CHUNK - 1.0 - pos[None, :]) * log_gamma[:, None])[:, :, None], full)
    chunk_decay = jnp.broadcast_to(jnp.exp(LA_CHUNK * log_gamma)[:, None, None], full)
    return cos_a, sin_a, cos_b, sin_b, cos_b * sign, sin_b * sign, dmat, q_scale, k_scale, chunk_decay


def kernel(x, norm_w, la_w_in, la_conv_w, la_a_log, la_dt_bias, la_out_norm_w, la_w_out, sg_w_in, sg_ln_w,
           sg_ln_b, sg_w_s, sg_b_s, sg_w_out, ffn_w_up, ffn_w_down):
    batch, seq, d = x.shape
    t = batch * seq
    h = x.reshape(t, d)
    gate0 = 4 * QK_WIDTH

    w_t = jnp.swapaxes(la_w_in[0], 0, 1).astype(BF16)
    proj, gates = _inproj_la(h, norm_w[0, 0][None, :], w_t, gate0=gate0, n_gate=2 * N_HEADS)

    cw = la_conv_w[0].reshape(3, QK_WIDTH, CONV_WIDTH).transpose(0, 2, 1)
    gp = jnp.zeros((8, LANES), F32)
    gp = gp.at[0, N_HEADS:2 * N_HEADS].set(la_a_log[0]).at[1, N_HEADS:2 * N_HEADS].set(la_dt_bias[0])
    n_layers, _, hidden = ffn_w_up.shape
    o_a, o_b, w_up, w_down, w_out, w_sg_in, w_sg_out = _mixer(
        proj, gates, cw, gp, la_out_norm_w[0][None, :],
        [ffn_w_up.reshape(n_layers * d, hidden), ffn_w_down.reshape(n_layers * hidden, d),
         la_w_out[0], sg_w_in[0], sg_w_out[0]], batch=batch, seq=seq)
    w_up = w_up.reshape(n_layers, d, hidden)
    w_down = w_down.reshape(n_layers, hidden, d)

    h = _out_rms_res([o_a, o_b], [w_out, w_out], h, norm_w[0, 1][None, :], tm=512, vmem_mib=48)
    h = _ffn(h, norm_w[0, 2][None, :], w_up, w_down, norm_w[0, 3][None, :], 0)

    proj1 = _inproj_gelu(h, norm_w[1, 0][None, :], w_sg_in)
    gated = _sg_gate(proj1, sg_ln_w[0][None, :], sg_ln_b[0][None, :], sg_w_s[0], sg_b_s[0].T)
    h = _out_rms_res([gated], [w_sg_out], h, norm_w[1, 1][None, :], tm=512, vmem_mib=56)
    h = _ffn(h, norm_w[1, 2][None, :], w_up, w_down, norm_w[1, 3][None, :], 1)
    return h.reshape(batch, seq, d)
```

```python
import functools
import math

import jax
import jax.numpy as jnp
from jax import lax
from jax.experimental import pallas as pl
from jax.experimental.pallas import tpu as pltpu

F32 = jnp.float32
BF16 = jnp.bfloat16

D_MODEL = 2048
N_HEADS = 8
HEAD_DIM = 128
QK_WIDTH = N_HEADS * HEAD_DIM
LA_CHUNK = 64
PAIR = 2 * LA_CHUNK
CONV_WIDTH = 4
HALO = 8
ROPE_BASE = 10000.0
SG_CHUNK = 128
SG_GROUPS = 8
SG_WIDTH = 2 * D_MODEL
SG_GROUP_DIM = SG_WIDTH // SG_GROUPS
FFN_HIDDEN = 4 * D_MODEL
EPS = 1e-6
LANES = 128
ROW_LOOP_UNROLL = 8
DEEP_UNROLL = 16
MIB = 1024 * 1024

NT_DIMS = (((1,), (1,)), ((), ()))


def _params(n_grid_axes, vmem_mib):
    return pltpu.CompilerParams(dimension_semantics=("arbitrary",) * n_grid_axes,
                                vmem_limit_bytes=vmem_mib * MIB)


def _dot(a, b):
    return jnp.dot(a, b, preferred_element_type=F32)


def _dot_nt(a, b):
    return lax.dot_general(a, b, NT_DIMS, preferred_element_type=F32)


def _silu(x):
    return x * jax.nn.sigmoid(x)


def _softplus(x):
    return jnp.maximum(x, 0.0) + jnp.log1p(jnp.exp(-jnp.abs(x)))


def _gelu_tanh(x):
    k0 = -2.0 * math.sqrt(2.0 / math.pi) * math.log2(math.e)
    k1 = k0 * 0.044715
    return x / (1.0 + jnp.exp2(x * (k0 + k1 * (x * x))))


def _rmsnorm_rows_to(src_ref, nw_ref, dst_ref, rows, unroll=ROW_LOOP_UNROLL):
    nw = nw_ref[...]
    step = 16

    def body(r, carry):
        sl = pl.ds(pl.multiple_of(r * step, step), step)
        x = src_ref[sl, :]
        ms = jnp.mean(x * x, axis=-1, keepdims=True)
        dst_ref[sl, :] = (x * lax.rsqrt(ms + EPS) * nw).astype(dst_ref.dtype)
        return carry

    lax.fori_loop(0, rows // step, body, 0, unroll=unroll)


def _rms_residual_rows(acc_ref, h_ref, nw_ref, o_ref, rows, unroll=ROW_LOOP_UNROLL):
    nw = nw_ref[...]
    step = 16

    def body(r, carry):
        sl = pl.ds(pl.multiple_of(r * step, step), step)
        y = acc_ref[sl, :]
        ms = jnp.mean(y * y, axis=-1, keepdims=True)
        o_ref[sl, :] = h_ref[sl, :] + y * lax.rsqrt(ms + EPS) * nw
        return carry

    lax.fori_loop(0, rows // step, body, 0, unroll=unroll)


def _column_chunks(width, chunk=1024):
    return [slice(c, c + chunk) for c in range(0, width, chunk)]


def _inproj_la_kernel(x_ref, nw_ref, w_ref, wg_ref, o_ref, g_ref, xn_ref, *, tm):
    @pl.when(pl.program_id(1) == 0)
    def _():
        _rmsnorm_rows_to(x_ref, nw_ref, xn_ref, tm, DEEP_UNROLL)
        g_ref[...] = _dot_nt(xn_ref[...], wg_ref[...])

    for cols in _column_chunks(o_ref.shape[1]):
        o_ref[:, cols] = _dot_nt(xn_ref[...], w_ref[cols, :]).astype(o_ref.dtype)


def _inproj_la(x, nw, w_t, *, gate0, n_gate, tm=1024, tn=2048):
    t, k = x.shape
    n = w_t.shape[0] - n_gate
    per_part = gate0 // tn

    def w_rows(i, j):
        return (pl.multiple_of(j * tn + jnp.where(j >= per_part, n_gate, 0), n_gate), 0)

    return pl.pallas_call(
        functools.partial(_inproj_la_kernel, tm=tm),
        grid=(t // tm, n // tn),
        in_specs=[
            pl.BlockSpec((tm, k), lambda i, j: (i, 0)),
            pl.BlockSpec((1, k), lambda i, j: (0, 0)),
            pl.BlockSpec((pl.Element(tn), pl.Element(k)), w_rows),
            pl.BlockSpec((LANES, k), lambda i, j: (gate0 // LANES, 0)),
        ],
        out_specs=[
            pl.BlockSpec((tm, tn), lambda i, j: (i, j)),
            pl.BlockSpec((tm, LANES), lambda i, j: (i, 0)),
        ],
        out_shape=[jax.ShapeDtypeStruct((t, n), BF16), jax.ShapeDtypeStruct((t, LANES), F32)],
        scratch_shapes=[pltpu.VMEM((tm, k), BF16)],
        compiler_params=_params(2, 56),
        name="inproj_la",
    )(x, nw, w_t, w_t)


def _inproj_gelu_kernel(x_ref, nw_ref, w_ref, o_ref, xn_ref, *, tm):
    @pl.when(pl.program_id(1) == 0)
    def _():
        _rmsnorm_rows_to(x_ref, nw_ref, xn_ref, tm, DEEP_UNROLL)

    for cols in _column_chunks(o_ref.shape[1]):
        o_ref[:, cols] = _gelu_tanh(_dot(xn_ref[...], w_ref[:, cols])).astype(o_ref.dtype)


def _inproj_gelu(x, nw, w, *, tm=1024, tn=2048):
    t, k = x.shape
    n = w.shape[1]
    return pl.pallas_call(
        functools.partial(_inproj_gelu_kernel, tm=tm),
        grid=(t // tm, n // tn),
        in_specs=[
            pl.BlockSpec((tm, k), lambda i, j: (i, 0)),
            pl.BlockSpec((1, k), lambda i, j: (0, 0)),
            pl.BlockSpec((k, tn), lambda i, j: (0, j)),
        ],
        out_specs=pl.BlockSpec((tm, tn), lambda i, j: (i, j)),
        out_shape=jax.ShapeDtypeStruct((t, n), BF16),
        scratch_shapes=[pltpu.VMEM((tm, k), BF16)],
        compiler_params=_params(2, 56),
        name="inproj_gelu",
    )(x, nw, w)


def _out_rms_res_kernel(*refs, n_pairs, tm):
    a_refs = refs[:n_pairs]
    w_refs = refs[n_pairs:2 * n_pairs]
    h_ref, nw_ref, o_ref, acc_ref = refs[2 * n_pairs:]
    y = _dot(a_refs[0][...], w_refs[0][...])
    for a_ref, w_ref in zip(a_refs[1:], w_refs[1:]):
        y = y + _dot(a_ref[...], w_ref[...])
    acc_ref[...] = y
    _rms_residual_rows(acc_ref, h_ref, nw_ref, o_ref, tm, DEEP_UNROLL)


def _out_rms_res(a_list, w_list, h, nw, *, tm, vmem_mib):
    t, n = h.shape
    n_pairs = len(a_list)
    in_specs = [pl.BlockSpec((tm, a.shape[1]), lambda i: (i, 0)) for a in a_list]
    k0 = 0
    for a, w in zip(a_list, w_list):
        ka = a.shape[1]
        row_block = k0 // ka if w.shape[0] != ka else 0
        in_specs.append(pl.BlockSpec((ka, n), lambda i, rb=row_block: (rb, 0)))
        k0 += ka
    in_specs += [pl.BlockSpec((tm, n), lambda i: (i, 0)), pl.BlockSpec((1, n), lambda i: (0, 0))]
    return pl.pallas_call(
        functools.partial(_out_rms_res_kernel, n_pairs=n_pairs, tm=tm),
        grid=(t // tm,),
        in_specs=in_specs,
        out_specs=pl.BlockSpec((tm, n), lambda i: (i, 0)),
        out_shape=jax.ShapeDtypeStruct((t, n), F32),
        scratch_shapes=[pltpu.VMEM((tm, n), F32)],
        compiler_params=_params(1, vmem_mib),
        name="out_rms_res",
    )(*a_list, *w_list, h, nw)


def _ffn_kernel(h_ref, h_next_ref, nw_in_ref, wup_ref, wdn_ref, nw_out_ref, o_ref, xn_ref, acc_ref, *, tm, n_j):
    i = pl.program_id(0)
    j = pl.program_id(1)
    slot = lax.rem(i, 2)

    @pl.when((i == 0) & (j == 0))
    def _():
        _rmsnorm_rows_to(h_ref, nw_in_ref, xn_ref.at[0], tm)

    def step(first):
        u = _dot(xn_ref[slot], wup_ref[...])
        u = jnp.square(jnp.maximum(u, 0.0)).astype(BF16)
        down = _dot(u, wdn_ref[...])
        if first:
            acc_ref[...] = down
        else:
            acc_ref[...] += down

        nw_in = nw_in_ref[...]
        share = tm // n_j
        xn_next_ref = xn_ref.at[1 - slot]
        for g in range(share // 16):
            sl = pl.ds(pl.multiple_of(j * share + g * 16, 16), 16)
            x = h_next_ref[sl, :]
            ms = jnp.mean(x * x, axis=-1, keepdims=True)
            xn_next_ref[sl, :] = (x * lax.rsqrt(ms + EPS) * nw_in).astype(BF16)

    pl.when(j == 0)(functools.partial(step, True))
    pl.when(j > 0)(functools.partial(step, False))

    @pl.when(j == n_j - 1)
    def _():
        _rms_residual_rows(acc_ref, h_ref, nw_out_ref, o_ref, tm)


def _ffn(h, nw_in, wup, wdn, nw_out, layer, *, tm=512, th=1024):
    t, d = h.shape
    hidden = wup.shape[2]
    n_i, n_j = t // tm, hidden // th
    return pl.pallas_call(
        functools.partial(_ffn_kernel, tm=tm, n_j=n_j),
        grid=(n_i, n_j),
        in_specs=[
            pl.BlockSpec((tm, d), lambda i, j: (i, 0)),
            pl.BlockSpec((tm, d), lambda i, j: (jnp.minimum(i + 1, n_i - 1), 0)),
            pl.BlockSpec((1, d), lambda i, j: (0, 0)),
            pl.BlockSpec((None, d, th), lambda i, j: (layer, 0, j)),
            pl.BlockSpec((None, th, d), lambda i, j: (layer, j, 0)),
            pl.BlockSpec((1, d), lambda i, j: (0, 0)),
        ],
        out_specs=pl.BlockSpec((tm, d), lambda i, j: (i, 0)),
        out_shape=jax.ShapeDtypeStruct((t, d), F32),
        scratch_shapes=[pltpu.VMEM((2, tm, d), BF16), pltpu.VMEM((tm, d), F32)],
        compiler_params=_params(2, 56),
        name="ffn",
    )(h, h, nw_in, wup, wdn, nw_out)


def _cast_specs(arrays, n_steps, step_index):
    in_specs, out_specs, out_shapes = [], [], []
    for a in arrays:
        rows, cols = a.shape[0] // n_steps, a.shape[1]
        in_specs.append(pl.BlockSpec((rows, cols), lambda b, i: (step_index(b, i), 0)))
        out_specs.append(pl.BlockSpec((rows, cols), lambda b, i: (step_index(b, i), 0)))
        out_shapes.append(jax.ShapeDtypeStruct(a.shape, BF16))
    return in_specs, out_specs, out_shapes


def _cast_slabs(in_refs, out_refs):
    for in_ref, out_ref in zip(in_refs, out_refs):
        out_ref[...] = in_ref[...].astype(out_ref.dtype)


def _pair_masks():
    row = lax.broadcasted_iota(jnp.int32, (PAIR, PAIR), 0)
    col = lax.broadcasted_iota(jnp.int32, (PAIR, PAIR), 1)
    chunk_bits = LA_CHUNK.bit_length() - 1
    same_chunk = (row >> chunk_bits) == (col >> chunk_bits)
    return row, col, same_chunk


def _chunk_halves(parts, a):
    zeros = jnp.zeros_like(parts)
    return jnp.concatenate([parts, zeros] if a == 0 else [zeros, parts], axis=0)


N_GDN_IN = 8
N_RET_IN = 14


def _mixer_kernel(*refs, tl, n_cast):
    gdn_in, ret_in, rest = refs[:N_GDN_IN], refs[N_GDN_IN:N_GDN_IN + N_RET_IN], refs[N_GDN_IN + N_RET_IN:]
    cast_in, (oa_ref, ob_ref), cast_out = rest[:n_cast], rest[n_cast:n_cast + 2], rest[n_cast + 2:2 * n_cast + 2]
    xpad_ref, s_gdn_ref, s_ret_ref = rest[2 * n_cast + 2:]

    @pl.when(pl.program_id(1) == 0)
    def _():
        _gdn_reset(xpad_ref, s_gdn_ref)
        s_ret_ref[...] = jnp.zeros_like(s_ret_ref)

    _cast_slabs(cast_in, cast_out)
    _gdn_body(*gdn_in, oa_ref, xpad_ref, s_gdn_ref, tl=tl)
    _ret_body(*ret_in, ob_ref, s_ret_ref, tl=tl)


def _gdn_reset(xpad_ref, s_ref):
    xpad_ref[:, 0:HALO, :] = jnp.zeros((3, HALO, QK_WIDTH), F32)
    s_ref[...] = jnp.zeros_like(s_ref)


def _gdn_body(q_ref, k_ref, v_ref, z_ref, gate_ref, cw_ref, gp_ref, onw_ref, o_ref,
              xpad_ref, s_ref, *, tl):
    nh = N_HEADS
    heads = range(nh)

    xpad_ref[0, HALO:HALO + tl, :] = q_ref[...].astype(F32)
    xpad_ref[1, HALO:HALO + tl, :] = k_ref[...].astype(F32)
    xpad_ref[2, HALO:HALO + tl, :] = v_ref[...].astype(F32)

    row, col, same_chunk = _pair_masks()
    causal = same_chunk & (col <= row)
    strict = same_chunk & (col < row)
    eye = (row == col).astype(F32)
    row_in_chunk = row & (LA_CHUNK - 1)
    lane8 = lax.broadcasted_iota(jnp.int32, (nh, PAIR), 1)

    a_coef = -jnp.exp(gp_ref[0:1, :])
    dt_bias = gp_ref[1:2, :]
    onw = onw_ref[...]
    s = [s_ref[h] for h in heads]

    for p in range(tl // PAIR):
        r0 = p * PAIR
        rows = slice(r0, r0 + PAIR)

        pre = gate_ref[rows, :]
        beta_all = jax.nn.sigmoid(pre)
        g = a_coef * _softplus(pre + dt_bias)
        for sh in (1, 2, 4, 8, 16, 32):
            g = g + jnp.where(row_in_chunk >= sh, pltpu.roll(g, sh, axis=0), 0.0)
        gt8 = g.T[nh:2 * nh, :]
        g_last = [jnp.sum(jnp.where(lane8 == (a + 1) * LA_CHUNK - 1, gt8, 0.0), axis=1, keepdims=True)
                  for a in range(2)]
        k_tail8 = jnp.exp(jnp.where(lane8 < LA_CHUNK, g_last[0], g_last[1]) - gt8)
        chunk_decay = [jnp.exp(gl) for gl in g_last]

        def conv(ci, cs):
            w = cw_ref[ci, :, cs]
            acc = xpad_ref[ci, HALO + r0:HALO + r0 + PAIR, cs] * w[CONV_WIDTH - 1:CONV_WIDTH, :]
            for j in range(1, CONV_WIDTH):
                acc = acc + xpad_ref[ci, HALO + r0 - j:HALO + r0 - j + PAIR, cs] * w[CONV_WIDTH - 1 - j:CONV_WIDTH - j, :]
            return _silu(acc)

        q_l, k_l, rhs_l, kbf_l, decay_l, m1_l, lhs_l = [], [], [], [], [], [], []
        for h in heads:
            cs = slice(h * HEAD_DIM, (h + 1) * HEAD_DIM)
            q = conv(0, cs)
            k = conv(1, cs)
            v = conv(2, cs)
            q = q * lax.rsqrt(jnp.sum(q * q, axis=-1, keepdims=True) + 1e-6) * (HEAD_DIM ** -0.5)
            k = k * lax.rsqrt(jnp.sum(k * k, axis=-1, keepdims=True) + 1e-6)
            gc_col = jnp.sum(jnp.where(col == nh + h, g, 0.0), axis=1, keepdims=True)
            beta_col = jnp.sum(jnp.where(col == h, beta_all, 0.0), axis=1, keepdims=True)
            decay = jnp.exp(jnp.where(causal, gc_col - gt8[h:h + 1, :], -jnp.inf))
            e_col = jnp.exp(gc_col)
            q_l.append(q * e_col)
            k_l.append(k)
            rhs_l.append(jnp.concatenate([v * beta_col, k * (beta_col * e_col)], axis=1).astype(BF16))
            kbf = k.astype(BF16)
            kbf_l.append(kbf)
            decay_l.append(decay)
            m1_l.append(beta_col * decay)
            lhs_l.append(jnp.concatenate([q.astype(BF16), kbf], axis=0))

        r_l = [_dot_nt(lhs_l[h], kbf_l[h]) for h in heads]
        qk_l = [(r_l[h][0:PAIR] * decay_l[h]).astype(BF16) for h in heads]
        a_l = [jnp.where(strict, r_l[h][PAIR:2 * PAIR] * m1_l[h], 0.0) for h in heads]
        x_l = [eye - a_l[h] for h in heads]
        pb_l = [a_l[h].astype(BF16) for h in heads]
        for _ in range(5):
            pb_l = [_dot(pb_l[h], pb_l[h]).astype(BF16) for h in heads]
            x_l = [x_l[h] + _dot(x_l[h].astype(BF16), pb_l[h]) for h in heads]

        uw_l = [_dot(x_l[h].astype(BF16), rhs_l[h]) for h in heads]
        u_l = [uw_l[h][:, 0:HEAD_DIM] for h in heads]
        w_l = [uw_l[h][:, HEAD_DIM:2 * HEAD_DIM] for h in heads]
        kt_l = [(k_l[h].T * k_tail8[h:h + 1, :]).astype(BF16) for h in heads]

        v_new_l = [[], []]
        qs_l = [[], []]
        for a in range(2):
            ra = slice(a * LA_CHUNK, (a + 1) * LA_CHUNK)
            r2_l = [_dot(jnp.concatenate([w_l[h][ra], q_l[h][ra]], axis=0).astype(BF16), s[h].astype(BF16))
                    for h in heads]
            for h in heads:
                v_new = u_l[h][ra] - r2_l[h][0:LA_CHUNK]
                v_new_l[a].append(v_new)
                qs_l[a].append(r2_l[h][LA_CHUNK:2 * LA_CHUNK])
            upd_l = [_dot(kt_l[h], _chunk_halves(v_new_l[a][h], a).astype(BF16)) for h in heads]
            s = [s[h] * chunk_decay[a][h:h + 1, :] + upd_l[h] for h in heads]

        o_l = [jnp.concatenate([qs_l[0][h], qs_l[1][h]], axis=0)
               + _dot(qk_l[h], jnp.concatenate([v_new_l[0][h], v_new_l[1][h]], axis=0).astype(BF16))
               for h in heads]
        for h in heads:
            cs = slice(h * HEAD_DIM, (h + 1) * HEAD_DIM)
            o = o_l[h]
            o = o * lax.rsqrt(jnp.mean(o * o, axis=-1, keepdims=True) + EPS) * onw * _silu(z_ref[rows, cs].astype(F32))
            o_ref[rows, cs] = o.astype(o_ref.dtype)

    for h in heads:
        s_ref[h] = s[h]
    xpad_ref[:, 0:HALO, :] = xpad_ref[:, tl:tl + HALO, :]


def _ret_body(q_ref, k_ref, v_ref, g_ref, cos_a_ref, sin_a_ref, cos_b_ref, sin_b_ref, cos_bs_ref, sin_bs_ref,
              dmat_ref, qs_ref, ks_ref, cd_ref, o_ref, s_ref, *, tl):
    heads = range(N_HEADS)
    blk = pl.ds(pl.program_id(1), 1)
    cos_a = cos_a_ref[blk, :]
    sin_a = sin_a_ref[blk, :]

    s = [s_ref[h] for h in heads]

    for p in range(tl // PAIR):
        rows = slice(p * PAIR, (p + 1) * PAIR)
        cosf = cos_a * cos_b_ref[rows, :] - sin_a * sin_b_ref[rows, :]
        sinf = sin_a * cos_bs_ref[rows, :] + cos_a * sin_bs_ref[rows, :]

        q_l, kb_l, kdt_l, v_l = [], [], [], []
        for h in heads:
            cs = slice(h * HEAD_DIM, (h + 1) * HEAD_DIM)
            q = q_ref[rows, cs].astype(F32)
            k = k_ref[rows, cs].astype(F32)
            q = q * cosf + pltpu.roll(q, HEAD_DIM // 2, axis=1) * sinf
            k = (k * cosf + pltpu.roll(k, HEAD_DIM // 2, axis=1) * sinf) * (HEAD_DIM ** -0.5)
            q_l.append(q)
            kb_l.append(k.astype(BF16))
            kdt_l.append((k * ks_ref[h]).T.astype(BF16))
            v_l.append(v_ref[rows, cs])

        qk_l = [(_dot_nt(q_l[h].astype(BF16), kb_l[h]) * dmat_ref[h]).astype(BF16) for h in heads]
        inner_l = [_dot(qk_l[h], v_l[h].astype(BF16)) for h in heads]
        qd_l = [(q_l[h] * qs_ref[h]).astype(BF16) for h in heads]

        cross_l = [[], []]
        for a in range(2):
            ra = slice(a * LA_CHUNK, (a + 1) * LA_CHUNK)
            cross_l[a] = [_dot(qd_l[h][ra], s[h].astype(BF16)) for h in heads]
            upd_l = [_dot(kdt_l[h], _chunk_halves(v_l[h][ra], a).astype(BF16)) for h in heads]
            s = [s[h] * cd_ref[h] + upd_l[h] for h in heads]

        for h in heads:
            cs = slice(h * HEAD_DIM, (h + 1) * HEAD_DIM)
            o = inner_l[h] + jnp.concatenate([cross_l[0][h], cross_l[1][h]], axis=0)
            o = o * lax.rsqrt(jnp.mean(o * o, axis=-1, keepdims=True) + EPS)
            o_ref[rows, cs] = (_silu(g_ref[rows, cs].astype(F32)) * o).astype(o_ref.dtype)

    for h in heads:
        s_ref[h] = s[h]


def _mixer(proj, gates, cw, gp, onw, cast_arrays, *, batch, seq, tl=256):
    nblk = seq // tl
    tables = _retention_tables(seq, tl)
    cast_in_specs, cast_out_specs, cast_shapes = _cast_specs(cast_arrays, batch * nblk, lambda b, i: b * nblk + i)

    def seg(s_idx):
        return pl.BlockSpec((tl, QK_WIDTH), lambda b, i: (b * nblk + i, s_idx))

    def const(shape):
        return pl.BlockSpec(shape, lambda b, i: (0,) * len(shape))

    head_table = (N_HEADS, PAIR, HEAD_DIM)
    out_spec = pl.BlockSpec((tl, QK_WIDTH), lambda b, i: (b * nblk + i, 0))
    out_shape = jax.ShapeDtypeStruct((batch * seq, QK_WIDTH), BF16)
    return pl.pallas_call(
        functools.partial(_mixer_kernel, tl=tl, n_cast=len(cast_arrays)),
        grid=(batch, nblk),
        in_specs=[
            seg(0), seg(1), seg(2), seg(3),
            pl.BlockSpec((tl, LANES), lambda b, i: (b * nblk + i, 0)),
            const((3, CONV_WIDTH, QK_WIDTH)), const((8, LANES)), const((1, HEAD_DIM)),
            seg(4), seg(5), seg(6), seg(7),
            const((nblk, HEAD_DIM)), const((nblk, HEAD_DIM)),
            const((tl, HEAD_DIM)), const((tl, HEAD_DIM)), const((tl, HEAD_DIM)), const((tl, HEAD_DIM)),
            const(head_table), const(head_table), const(head_table), const(head_table),
        ] + cast_in_specs,
        out_specs=[out_spec, out_spec] + cast_out_specs,
        out_shape=[out_shape, out_shape] + cast_shapes,
        scratch_shapes=[pltpu.VMEM((3, HALO + tl, QK_WIDTH), F32),
                        pltpu.VMEM((N_HEADS, HEAD_DIM, HEAD_DIM), F32),
                        pltpu.VMEM((N_HEADS, HEAD_DIM, HEAD_DIM), F32)],
        compiler_params=_params(2, 56),
        name="mixer",
    )(proj, proj, proj, proj, gates, cw, gp, onw, proj, proj, proj, proj, *tables, *cast_arrays)


def _sg_gate_kernel(u_ref, v_ref, lnw_ref, lnb_ref, ws_ref, bs_ref, o_ref, vn_ref, *, tm):
    lnw = lnw_ref[...]
    lnb = lnb_ref[...]
    step = 16

    def body(r, carry):
        sl = pl.ds(pl.multiple_of(r * step, step), step)
        x = v_ref[sl, :].astype(F32)
        mu = jnp.mean(x, axis=-1, keepdims=True)
        xc = x - mu
        var = jnp.mean(xc * xc, axis=-1, keepdims=True)
        vn_ref[sl, :] = (xc * lax.rsqrt(var + EPS) * lnw + lnb).astype(vn_ref.dtype)
        return carry

    lax.fori_loop(0, tm // step, body, 0, unroll=ROW_LOOP_UNROLL)

    row = lax.broadcasted_iota(jnp.int32, (SG_CHUNK, SG_CHUNK), 0)
    col = lax.broadcasted_iota(jnp.int32, (SG_CHUNK, SG_CHUNK), 1)
    for g in range(SG_GROUPS):
        wg = jnp.where(col <= row, ws_ref[g], 0.0).astype(BF16)
        bias = bs_ref[:, g:g + 1]
        cols = slice(g * SG_GROUP_DIM, (g + 1) * SG_GROUP_DIM)
        for c in range(tm // SG_CHUNK):
            rows = slice(c * SG_CHUNK, (c + 1) * SG_CHUNK)
            sgate = _dot(wg, vn_ref[rows, cols]) + bias
            o_ref[rows, cols] = (u_ref[rows, cols].astype(F32) * sgate).astype(o_ref.dtype)


def _sg_mixer_kernel(u_ref, v_ref, lnw_ref, lnb_ref, ws_ref, bs_ref, w_ref, h_ref, nw_ref, o_ref,
                     vn_ref, gated_ref, acc_ref, *, tm):
    _sg_gate_kernel(u_ref, v_ref, lnw_ref, lnb_ref, ws_ref, bs_ref, gated_ref, vn_ref, tm=tm)
    acc_ref[...] = _dot(gated_ref[...], w_ref[...])
    _rms_residual_rows(acc_ref, h_ref, nw_ref, o_ref, tm, DEEP_UNROLL)


def _sg_mixer(proj, lnw, lnb, ws, bs_t, w, h, nw, *, tm=256):
    t, n = h.shape
    return pl.pallas_call(
        functools.partial(_sg_mixer_kernel, tm=tm),
        grid=(t // tm,),
        in_specs=[
            pl.BlockSpec((tm, SG_WIDTH), lambda i: (i, 0)),
            pl.BlockSpec((tm, SG_WIDTH), lambda i: (i, 1)),
            pl.BlockSpec((1, SG_WIDTH), lambda i: (0, 0)),
            pl.BlockSpec((1, SG_WIDTH), lambda i: (0, 0)),
            pl.BlockSpec((SG_GROUPS, SG_CHUNK, SG_CHUNK), lambda i: (0, 0, 0)),
            pl.BlockSpec((SG_CHUNK, SG_GROUPS), lambda i: (0, 0)),
            pl.BlockSpec((SG_WIDTH, n), lambda i: (0, 0)),
            pl.BlockSpec((tm, n), lambda i: (i, 0)),
            pl.BlockSpec((1, n), lambda i: (0, 0)),
        ],
        out_specs=pl.BlockSpec((tm, n), lambda i: (i, 0)),
        out_shape=jax.ShapeDtypeStruct((t, n), F32),
        scratch_shapes=[pltpu.VMEM((tm, SG_WIDTH), BF16), pltpu.VMEM((tm, SG_WIDTH), BF16), pltpu.VMEM((tm, n), F32)],
        compiler_params=_params(1, 56),
        name="sg_mixer",
    )(proj, proj, lnw, lnb, ws, bs_t, w, h, nw)


def _sg_gate(proj, lnw, lnb, ws, bs_t, *, tm=512):
    t = proj.shape[0]
    return pl.pallas_call(
        functools.partial(_sg_gate_kernel, tm=tm),
        grid=(t // tm,),
        in_specs=[
            pl.BlockSpec((tm, SG_WIDTH), lambda i: (i, 0)),
            pl.BlockSpec((tm, SG_WIDTH), lambda i: (i, 1)),
            pl.BlockSpec((1, SG_WIDTH), lambda i: (0, 0)),
            pl.BlockSpec((1, SG_WIDTH), lambda i: (0, 0)),
            pl.BlockSpec((SG_GROUPS, SG_CHUNK, SG_CHUNK), lambda i: (0, 0, 0)),
            pl.BlockSpec((SG_CHUNK, SG_GROUPS), lambda i: (0, 0)),
        ],
        out_specs=pl.BlockSpec((tm, SG_WIDTH), lambda i: (i, 0)),
        out_shape=jax.ShapeDtypeStruct((t, SG_WIDTH), BF16),
        scratch_shapes=[pltpu.VMEM((tm, SG_WIDTH), BF16)],
        compiler_params=_params(1, 40),
        name="sg_gate",
    )(proj, proj, lnw, lnb, ws, bs_t)


def _retention_tables(seq, tl):
    half = HEAD_DIM // 2
    inv_freq = 1.0 / (ROPE_BASE ** jnp.linspace(0.0, 1.0, half, dtype=F32))
    base = (jnp.arange(seq // tl, dtype=F32) * tl)[:, None] * inv_freq[None, :]
    off = jnp.arange(tl, dtype=F32)[:, None] * inv_freq[None, :]

    def both_halves(a):
        return jnp.concatenate([a, a], axis=1)

    sign = jnp.concatenate([-jnp.ones((half,), F32), jnp.ones((half,), F32)])[None, :]
    cos_a, sin_a = both_halves(jnp.cos(base)), both_halves(jnp.sin(base))
    cos_b, sin_b = both_halves(jnp.cos(off)), both_halves(jnp.sin(off))

    log_gamma = jnp.log1p(-jnp.power(2.0, -5.0 - jnp.arange(N_HEADS, dtype=F32)))
    t = jnp.arange(PAIR)
    pos = (t % LA_CHUNK).astype(F32)
    mask = ((t[:, None] // LA_CHUNK) == (t[None, :] // LA_CHUNK)) & (t[None, :] <= t[:, None])
    dmat = jnp.exp(jnp.where(mask[None], (pos[:, None] - pos[None, :])[None] * log_gamma[:, None, None], -jnp.inf))
    full = (N_HEADS, PAIR, HEAD_DIM)
    q_scale = jnp.broadcast_to(jnp.exp((pos[None, :] + 1.0) * log_gamma[:, None])[:, :, None], full)
    k_scale = jnp.broadcast_to(jnp.exp((LA_CHUNK - 1.0 - pos[None, :]) * log_gamma[:, None])[:, :, None], full)
    chunk_decay = jnp.broadcast_to(jnp.exp(LA_CHUNK * log_gamma)[:, None, None], full)
    return cos_a, sin_a, cos_b, sin_b, cos_b * sign, sin_b * sign, dmat, q_scale, k_scale, chunk_decay


def kernel(x, norm_w, la_w_in, la_conv_w, la_a_log, la_dt_bias, la_out_norm_w, la_w_out, sg_w_in, sg_ln_w,
           sg_ln_b, sg_w_s, sg_b_s, sg_w_out, ffn_w_up, ffn_w_down):
    batch, seq, d = x.shape
    t = batch * seq
    h = x.reshape(t, d)
    gate0 = 4 * QK_WIDTH

    w_t = jnp.swapaxes(la_w_in[0], 0, 1).astype(BF16)
    proj, gates = _inproj_la(h, norm_w[0, 0][None, :], w_t, gate0=gate0, n_gate=2 * N_HEADS)

    cw = la_conv_w[0].reshape(3, QK_WIDTH, CONV_WIDTH).transpose(0, 2, 1)
    gp = jnp.zeros((8, LANES), F32)
    gp = gp.at[0, N_HEADS:2 * N_HEADS].set(la_a_log[0]).at[1, N_HEADS:2 * N_HEADS].set(la_dt_bias[0])
    n_layers, _, hidden = ffn_w_up.shape
    o_a, o_b, w_up, w_down, w_out, w_sg_in, w_sg_out = _mixer(
        proj, gates, cw, gp, la_out_norm_w[0][None, :],
        [ffn_w_up.reshape(n_layers * d, hidden), ffn_w_down.reshape(n_layers * hidden, d),
         la_w_out[0], sg_w_in[0], sg_w_out[0]], batch=batch, seq=seq)
    w_up = w_up.reshape(n_layers, d, hidden)
    w_down = w_down.reshape(n_layers, hidden, d)

    h = _out_rms_res([o_a, o_b], [w_out, w_out], h, norm_w[0, 1][None, :], tm=512, vmem_mib=48)
    h = _ffn(h, norm_w[0, 2][None, :], w_up, w_down, norm_w[0, 3][None, :], 0)

    proj1 = _inproj_gelu(h, norm_w[1, 0][None, :], w_sg_in)
    h = _sg_mixer(proj1, sg_ln_w[0][None, :], sg_ln_b[0][None, :], sg_w_s[0], sg_b_s[0].T, w_sg_out, h,
                  norm_w[1, 1][None, :])
    h = _ffn(h, norm_w[1, 2][None, :], w_up, w_down, norm_w[1, 3][None, :], 1)
    return h.reshape(batch, seq, d)
```
